```python
import jax
import jax.numpy as jnp
from jax import lax
import numpy as np

D_MODEL = 1024
BATCH = 16
SEQ = 256
DEPTH = 4
DEC_BATCH = 8
DEC_SEQ = 4096
PAST_LEN = 256

GRID_W = 64
N_EVEN = (DEPTH + 1) // 2
N_ODD = DEPTH // 2
EPS = 1e-6

GDN_HEADS = 4
GDN_DK = 128
GDN_DV = 128
GDN_CONV = 3
GDN_CHUNK = 64
A_QK = GDN_HEADS * GDN_DK
A_V = GDN_HEADS * GDN_DV
A_CONV = 2 * A_QK + A_V
FNET_GROUPS = 4
FNET_GROUP_DIM = 128
B_W = FNET_GROUPS * FNET_GROUP_DIM
EVEN_IN = A_CONV + A_V + 4 * GDN_HEADS + B_W
EVEN_MIX = A_V + B_W
MLA_HEADS = 8
MLA_Q_LORA = 512
MLA_KV_LORA = 256
MLA_NOPE = 128
MLA_ROPE = 64
MLA_V = 128
MLA_QK = MLA_NOPE + MLA_ROPE
ODD_IN = MLA_Q_LORA + MLA_KV_LORA + MLA_ROPE
ROPE_BASE = 10000.0
Q_BLOCK = 128
PEER_HEADS = 8
PEER_NKEYS = 128
PEER_EXPERTS = PEER_NKEYS * PEER_NKEYS
PEER_DQ = 256
PEER_HALF = PEER_DQ // 2
PEER_TOPK = 16
PEER_BLOCK = 128

kernel_name = 'gdn_fnet_mla_peer_flow_step'


def rmsnorm(x, w):
    xf = x.astype(jnp.float32)
    y = xf * lax.rsqrt(jnp.mean(xf * xf, -1, keepdims=True) + EPS)
    return (y * w.astype(jnp.float32)).astype(x.dtype)


def l2norm(x):
    xf = x.astype(jnp.float32)
    return (xf * lax.rsqrt(jnp.sum(xf * xf, -1, keepdims=True) + EPS)).astype(x.dtype)


def ada_mods(cvec, w_mod, b_mod):
    m = (jax.nn.silu(cvec) @ w_mod + b_mod)[:, None, :]
    return jnp.split(m, 6, axis=-1)


def short_conv(x, w):
    pad = (GDN_CONV - 1) // 2
    L = x.shape[1]
    xp = jnp.pad(x, ((0, 0), (pad, pad), (0, 0)))
    y = xp[:, 0:L] * w[0]
    for i in range(1, GDN_CONV):
        y = y + xp[:, i:i + L] * w[i]
    return jax.nn.silu(y)


def gdn_chunked(q, k, v, g, beta, s0):
    f32 = jnp.float32
    B, L, H, DK = q.shape
    DV = v.shape[-1]
    C = GDN_CHUNK
    N = L // C
    qc = q.astype(f32).reshape(B, N, C, H, DK).transpose(1, 0, 3, 2, 4) * (DK ** -0.5)
    kc = k.astype(f32).reshape(B, N, C, H, DK).transpose(1, 0, 3, 2, 4)
    vc = v.astype(f32).reshape(B, N, C, H, DV).transpose(1, 0, 3, 2, 4)
    gc = jnp.cumsum(g.astype(f32).reshape(B, N, C, H).transpose(1, 0, 3, 2), axis=-1)
    bc = beta.astype(f32).reshape(B, N, C, H).transpose(1, 0, 3, 2)
    tri = jnp.tril(jnp.ones((C, C), bool))
    strict = jnp.tril(jnp.ones((C, C), bool), -1)
    decay = jnp.exp(jnp.where(tri, gc[..., :, None] - gc[..., None, :], -jnp.inf))
    kb = kc * bc[..., None]
    a = jnp.where(strict, jnp.einsum('nbhid,nbhjd->nbhij', kb, kc) * decay, 0.0)
    rhs = jnp.concatenate([vc * bc[..., None], kb * jnp.exp(gc)[..., None]], -1)
    sol = lax.linalg.triangular_solve(a, rhs, left_side=True, lower=True, unit_diagonal=True)
    u = sol[..., :DV]
    w = sol[..., DV:]
    qk = jnp.einsum('nbhid,nbhjd->nbhij', qc, kc) * decay

    def step(S, xs):
        q_i, k_i, u_i, w_i, g_i, qk_i = xs
        v_new = u_i - jnp.einsum('bhcd,bhde->bhce', w_i, S)
        o_i = (jnp.einsum('bhcd,bhde->bhce', q_i * jnp.exp(g_i)[..., None], S)
               + jnp.einsum('bhij,bhje->bhie', qk_i, v_new))
        g_last = g_i[..., -1:]
        S = (S * jnp.exp(g_last)[..., None]
             + jnp.einsum('bhcd,bhce->bhde', k_i * jnp.exp(g_last - g_i)[..., None], v_new))
        return S, o_i

    S, o = lax.scan(step, s0.astype(f32), (qc, kc, u, w, gc, qk))
    o = o.transpose(1, 0, 3, 2, 4).reshape(B, L, H, DV)
    return o.astype(q.dtype), S


def even_mixer(h, s_fwd, s_bwd, w_in, conv_w, a_log, dt_bias, o_norm_w, w_out):
    f32 = jnp.float32
    B, L, _ = h.shape
    p = h @ w_in
    qkv = short_conv(p[..., :A_CONV], conv_w)
    q = l2norm(qkv[..., :A_QK].reshape(B, L, GDN_HEADS, GDN_DK))
    k = l2norm(qkv[..., A_QK:2 * A_QK].reshape(B, L, GDN_HEADS, GDN_DK))
    v = qkv[..., 2 * A_QK:].reshape(B, L, GDN_HEADS, GDN_DV)
    gate = p[..., A_CONV:A_CONV + A_V].reshape(B, L, GDN_HEADS, GDN_DV)
    off = A_CONV + A_V
    ba = p[..., off:off + 4 * GDN_HEADS].astype(f32).reshape(B, L, 4, GDN_HEADS)
    beta = jax.nn.sigmoid(ba[:, :, :2])
    g = -jnp.exp(a_log.astype(f32)) * jax.nn.softplus(ba[:, :, 2:] + dt_bias.astype(f32))
    flip = lambda t: jnp.flip(t, 1)
    o_f, s_fwd = gdn_chunked(q, k, v, g[:, :, 0], beta[:, :, 0], s_fwd)
    o_b, s_bwd = gdn_chunked(flip(q), flip(k), flip(v), flip(g[:, :, 1]), flip(beta[:, :, 1]), s_bwd)
    o = o_f + flip(o_b)
    o = rmsnorm(o, o_norm_w) * jax.nn.silu(gate)
    xb = p[..., off + 4 * GDN_HEADS:].reshape(B, L, FNET_GROUPS, FNET_GROUP_DIM)
    fb = jnp.fft.fft2(xb.astype(f32), axes=(1, 3), norm='ortho').real.astype(h.dtype)
    mix = jnp.concatenate([o.reshape(B, L, A_V), fb.reshape(B, L, B_W)], -1)
    return mix @ w_out, s_fwd, s_bwd


def axial_angles(L):
    f32 = jnp.float32
    rows = L // GRID_W
    row = jnp.repeat(jnp.arange(rows, dtype=f32), GRID_W)
    col = jnp.tile(jnp.arange(GRID_W, dtype=f32), rows)
    n = MLA_ROPE // 4
    inv = jnp.power(ROPE_BASE, -jnp.arange(n, dtype=f32) / n)
    return row[:, None] * inv, col[:, None] * inv


def rotate(x, ang):
    n = x.shape[-1] // 2
    cos = jnp.cos(ang)[None, :, None, :]
    sin = jnp.sin(ang)[None, :, None, :]
    x1 = x[..., :n].astype(jnp.float32)
    x2 = x[..., n:].astype(jnp.float32)
    return jnp.concatenate([x1 * cos - x2 * sin, x1 * sin + x2 * cos], -1)


def rope_latent(x, ang_r, ang_c):
    half = MLA_ROPE // 2
    xr = x[..., MLA_NOPE:]
    rot = jnp.concatenate([rotate(xr[..., :half], ang_r), rotate(xr[..., half:], ang_c)], -1)
    return jnp.concatenate([x[..., :MLA_NOPE], rot.astype(x.dtype)], -1)


def mla_compress(h, w_in, q_norm_w, kv_norm_w):
    p = h @ w_in
    cq = rmsnorm(p[..., :MLA_Q_LORA], q_norm_w)
    ckv = rmsnorm(p[..., MLA_Q_LORA:MLA_Q_LORA + MLA_KV_LORA], kv_norm_w)
    kr = p[..., MLA_Q_LORA + MLA_KV_LORA:]
    return cq, ckv, kr


def mla_queries(cq, w_uq, qn_w):
    B, L, _ = cq.shape
    return rmsnorm((cq @ w_uq).reshape(B, L, MLA_HEADS, MLA_QK), qn_w)


def mla_keys_values(ckv, kr, w_ukv, kn_w):
    B, L, _ = ckv.shape
    kv = (ckv @ w_ukv).reshape(B, L, MLA_HEADS, MLA_NOPE + MLA_V)
    k_rope = jnp.broadcast_to(kr[:, :, None, :], (B, L, MLA_HEADS, MLA_ROPE))
    k = rmsnorm(jnp.concatenate([kv[..., :MLA_NOPE], k_rope], -1), kn_w)
    return k, kv[..., MLA_NOPE:]


def attend(q, k, v):
    B, Lq, H, dq = q.shape
    dv = v.shape[-1]
    nb = Lq // Q_BLOCK
    qb = q.reshape(B, nb, Q_BLOCK, H, dq).transpose(1, 0, 2, 3, 4)
    scale = dq ** -0.5

    def one(qi):
        s = jnp.einsum('bqhd,bkhd->bhqk', qi, k).astype(jnp.float32) * scale
        p = jax.nn.softmax(s, -1).astype(v.dtype)
        return jnp.einsum('bhqk,bkhd->bqhd', p, v)

    o = lax.map(one, qb)
    return o.transpose(1, 0, 2, 3, 4).reshape(B, Lq, H * dv)


def peer(h, w_q, sub_keys, u_tab, v_tab):
    B, L, D = h.shape
    x = h.reshape(-1, D)
    nb = x.shape[0] // PEER_BLOCK
    K = PEER_TOPK

    def block(xb):
        T = xb.shape[0]
        q = (xb @ w_q).reshape(T, PEER_HEADS, 2, PEER_HALF)
        s = jnp.einsum('thpd,hpkd->thpk', q, sub_keys).astype(jnp.float32)
        sv, si = lax.top_k(s, K)
        cand = (sv[:, :, 0, :, None] + sv[:, :, 1, None, :]).reshape(T, PEER_HEADS, K * K)
        cidx = (si[:, :, 0, :, None] * PEER_NKEYS + si[:, :, 1, None, :]).reshape(T, PEER_HEADS, K * K)
        cs, ci = lax.top_k(cand, K)
        eidx = jnp.take_along_axis(cidx, ci, -1)
        gate = jax.nn.softmax(cs, -1)
        act = jax.nn.gelu(jnp.einsum('thkd,td->thk', u_tab[eidx], xb).astype(jnp.float32))
        return jnp.einsum('thk,thkd->td', (gate * act).astype(xb.dtype), v_tab[eidx])

    y = lax.map(block, x.reshape(nb, PEER_BLOCK, D))
    return y.reshape(B, L, D)


def setup_inputs(seed: int = 0) -> dict:
    key = jax.random.key(seed)
    ks = jax.random.split(key, 30)
    f32 = jnp.float32
    D = D_MODEL

    def nrm(k, shape, scale):
        return jax.random.normal(k, shape, f32) * scale

    def gain(k, shape):
        return 1.0 + 0.05 * jax.random.normal(k, shape, f32)

    col_scale = jnp.concatenate([
        jnp.ones((A_CONV + A_V + 2 * GDN_HEADS,), f32),
        jnp.full((2 * GDN_HEADS,), 0.1, f32),
        jnp.ones((B_W,), f32)])
    dt = jnp.exp(jax.random.uniform(ks[14], (N_EVEN, 2, GDN_HEADS), f32,
                                    float(np.log(0.001)), float(np.log(0.1))))
    return {
        'x_prompt': nrm(ks[0], (BATCH, SEQ, D), 1.0),
        'x_sample': nrm(ks[1], (DEC_BATCH, DEC_SEQ, D), 1.0),
        'state_gdn': nrm(ks[2], (DEC_BATCH, N_EVEN, 2, GDN_HEADS, GDN_DK, GDN_DV), GDN_DK ** -0.5),
        'cache_mla_ckv': nrm(ks[3], (DEC_BATCH, N_ODD, PAST_LEN, MLA_KV_LORA), 1.0),
        'cache_mla_krope': nrm(ks[4], (DEC_BATCH, N_ODD, PAST_LEN, MLA_ROPE), 1.0),
        'c': nrm(ks[5], (DEC_BATCH, D), 1.0),
        'c_ctx': nrm(ks[6], (D,), 1.0),
        'w_mod': nrm(ks[7], (DEPTH, D, 6 * D), 0.5 * D ** -0.5),
        'b_mod': nrm(ks[8], (DEPTH, 6 * D), 0.02),
        'norm_mix': gain(ks[9], (DEPTH, D)),
        'norm_ffn': gain(ks[10], (DEPTH, D)),
        'even_w_in': nrm(ks[11], (N_EVEN, D, EVEN_IN), D ** -0.5) * col_scale,
        'even_conv_w': nrm(ks[12], (N_EVEN, GDN_CONV, A_CONV), GDN_CONV ** -0.5),
        'gdn_a_log': jnp.log(jax.random.uniform(ks[13], (N_EVEN, 2, GDN_HEADS), f32, 1.0, 16.0)),
        'gdn_dt_bias': dt + jnp.log(-jnp.expm1(-dt)),
        'gdn_o_norm': gain(ks[15], (N_EVEN, GDN_DV)),
        'even_w_out': nrm(ks[16], (N_EVEN, EVEN_MIX, D), EVEN_MIX ** -0.5),
        'odd_w_in': nrm(ks[17], (N_ODD, D, ODD_IN), D ** -0.5),
        'mla_q_norm': gain(ks[18], (N_ODD, MLA_Q_LORA)),
        'mla_kv_norm': gain(ks[19], (N_ODD, MLA_KV_LORA)),
        'mla_w_uq': nrm(ks[20], (N_ODD, MLA_Q_LORA, MLA_HEADS * MLA_QK), MLA_Q_LORA ** -0.5),
        'mla_w_ukv': nrm(ks[21], (N_ODD, MLA_KV_LORA, MLA_HEADS * (MLA_NOPE + MLA_V)), MLA_KV_LORA ** -0.5),
        'mla_q_headnorm': gain(ks[22], (N_ODD, MLA_QK)),
        'mla_k_headnorm': gain(ks[23], (N_ODD, MLA_QK)),
        'odd_w_out': nrm(ks[24], (N_ODD, MLA_HEADS * MLA_V, D), (MLA_HEADS * MLA_V) ** -0.5),
        'peer_w_q': nrm(ks[25], (DEPTH, D, PEER_HEADS * PEER_DQ), D ** -0.5),
        'peer_sub_keys': nrm(ks[26], (DEPTH, PEER_HEADS, 2, PEER_NKEYS, PEER_HALF), PEER_HALF ** -0.5),
        'peer_u': nrm(ks[27], (DEPTH, PEER_EXPERTS, D), D ** -0.5),
        'peer_v': nrm(ks[28], (DEPTH, PEER_EXPERTS, D), PEER_HEADS ** -0.5),
    }


def reference(x_prompt, x_sample, state_gdn, cache_mla_ckv, cache_mla_krope, c, c_ctx,
              w_mod, b_mod, norm_mix, norm_ffn,
              even_w_in, even_conv_w, gdn_a_log, gdn_dt_bias, gdn_o_norm, even_w_out,
              odd_w_in, mla_q_norm, mla_kv_norm, mla_w_uq, mla_w_ukv, mla_q_headnorm, mla_k_headnorm, odd_w_out,
              peer_w_q, peer_sub_keys, peer_u, peer_v):
    xp, xs = x_prompt, x_sample
    ang_r, ang_c = axial_angles(xs.shape[1])
    c_ctx_b = c_ctx[None, :]
    new_gdn, new_ckv, new_kr = [], [], []
    for l in range(DEPTH):
        j = l // 2
        sh1p, sc1p, g1p, sh2p, sc2p, g2p = ada_mods(c_ctx_b, w_mod[l], b_mod[l])
        sh1s, sc1s, g1s, sh2s, sc2s, g2s = ada_mods(c, w_mod[l], b_mod[l])
        hp = rmsnorm(xp, norm_mix[l]) * (1 + sc1p) + sh1p
        hs = rmsnorm(xs, norm_mix[l]) * (1 + sc1s) + sh1s
        if l % 2 == 0:
            zero = jnp.zeros((xp.shape[0], GDN_HEADS, GDN_DK, GDN_DV), jnp.float32)
            yp, sf, sb = even_mixer(hp, zero, zero, even_w_in[j], even_conv_w[j], gdn_a_log[j],
                                    gdn_dt_bias[j], gdn_o_norm[j], even_w_out[j])
            ys, _, _ = even_mixer(hs, state_gdn[:, j, 0], state_gdn[:, j, 1], even_w_in[j], even_conv_w[j],
                                  gdn_a_log[j], gdn_dt_bias[j], gdn_o_norm[j], even_w_out[j])
            new_gdn.append(jnp.stack([sf, sb], 1).astype(xp.dtype))
        else:
            cq_p, ckv_p, kr_p = mla_compress(hp, odd_w_in[j], mla_q_norm[j], mla_kv_norm[j])
            q_p = mla_queries(cq_p, mla_w_uq[j], mla_q_headnorm[j])
            k_p, v_p = mla_keys_values(ckv_p, kr_p, mla_w_ukv[j], mla_k_headnorm[j])
            yp = attend(q_p, k_p, v_p) @ odd_w_out[j]
            new_ckv.append(ckv_p)
            new_kr.append(kr_p)
            cq_s, ckv_s, kr_s = mla_compress(hs, odd_w_in[j], mla_q_norm[j], mla_kv_norm[j])
            q_s = rope_latent(mla_queries(cq_s, mla_w_uq[j], mla_q_headnorm[j]), ang_r, ang_c)
            k_s, v_s = mla_keys_values(ckv_s, kr_s, mla_w_ukv[j], mla_k_headnorm[j])
            k_s = rope_latent(k_s, ang_r, ang_c)
            k_c, v_c = mla_keys_values(cache_mla_ckv[:, j], cache_mla_krope[:, j], mla_w_ukv[j], mla_k_headnorm[j])
            ys = attend(q_s, jnp.concatenate([k_s, k_c], 1), jnp.concatenate([v_s, v_c], 1)) @ odd_w_out[j]
        xp = xp + g1p * yp
        xs = xs + g1s * ys
        hp = rmsnorm(xp, norm_ffn[l]) * (1 + sc2p) + sh2p
        hs = rmsnorm(xs, norm_ffn[l]) * (1 + sc2s) + sh2s
        xp = xp + g2p * peer(hp, peer_w_q[l], peer_sub_keys[l], peer_u[l], peer_v[l])
        xs = xs + g2s * peer(hs, peer_w_q[l], peer_sub_keys[l], peer_u[l], peer_v[l])
    new_state_gdn = jnp.stack(new_gdn, 1)
    new_mla_ckv = jnp.stack(new_ckv, 1)
    new_mla_krope = jnp.stack(new_kr, 1)
    return (xp, xs, new_state_gdn, new_mla_ckv, new_mla_krope)
```

```python
import functools
import math

import jax
import jax.numpy as jnp
from jax import lax
from jax.experimental import pallas as pl
from jax.experimental.pallas import tpu as pltpu

f32 = jnp.float32
bf16 = jnp.bfloat16
i32 = jnp.int32

LANES = 128
SUBLANES = 8
VMEM_LIMIT = 56 * 1024 * 1024

EPS = 1e-6
D = 1024
GRID_W = 64
GDN_H = 4
GDN_DK = 128
GDN_CHUNK = 64
A_QK = GDN_H * GDN_DK
FN_W = 512
MLA_H = 8
MLA_QL = 512
MLA_KVL = 256
MLA_NOPE = 128
MLA_ROPE = 64
MLA_QK = MLA_NOPE + MLA_ROPE
ROPE_BASE = 10000.0
PEER_H = 8
PEER_NK = 128
PEER_K = 16
PEER_SEL = PEER_H * PEER_K

HI = lax.Precision.HIGHEST
NT = (((1,), (1,)), ((), ()))
TN = (((0,), (0,)), ((), ()))
NN = (((1,), (0,)), ((), ()))


def _dot(a, b, dims=NN, precision=None):
    return lax.dot_general(a, b, dims, preferred_element_type=f32, precision=precision)


def _cp(*sem):
    return pltpu.CompilerParams(dimension_semantics=sem, vmem_limit_bytes=VMEM_LIMIT)


def _mod_row(i, prompt_tiles, tiles_per_seq):
    return jnp.where(i < prompt_tiles, 0, 1 + (i - prompt_tiles) // tiles_per_seq)


def _mods_body(c_ref, w_ref, b_ref, o_ref):
    c = c_ref[...]
    o_ref[...] = _dot(c * jax.nn.sigmoid(c), w_ref[...], precision=HI) + b_ref[...]


def ada_mods_all(cv, w_mod, b_mod):
    depth, _, n6 = w_mod.shape
    r = cv.shape[0]
    tn = 1536
    return pl.pallas_call(
        _mods_body,
        grid=(depth, n6 // tn),
        in_specs=[pl.BlockSpec((r, D), lambda l, j: (0, 0)),
                  pl.BlockSpec((None, D, tn), lambda l, j: (l, 0, j)),
                  pl.BlockSpec((None, 1, tn), lambda l, j: (l, 0, j))],
        out_specs=pl.BlockSpec((None, r, tn), lambda l, j: (l, 0, j)),
        out_shape=jax.ShapeDtypeStruct((depth, r, n6), f32),
        compiler_params=_cp("arbitrary", "arbitrary"),
        name="ada_mods",
    )(cv, w_mod, b_mod.reshape(depth, 1, n6))


def _normmm_body(x_ref, m_ref, nw_ref, w_ref, o_ref, *h_refs, shift_idx, precise):
    x = x_ref[...]
    y = x * lax.rsqrt(jnp.mean(x * x, -1, keepdims=True) + EPS) * nw_ref[...]
    h = y * (1.0 + m_ref[shift_idx + 1:shift_idx + 2, :]) + m_ref[shift_idx:shift_idx + 1, :]
    if precise:
        o_ref[...] = _dot(h, w_ref[...], precision=HI)
    else:
        o_ref[...] = _dot(h.astype(bf16), w_ref[...])
    if h_refs:
        h_refs[0][...] = h


def normmm(x, mods_l, shift_idx, norm_w, w, *, prompt_rows, seq_len, tm=256, precise=False, with_h=False):
    n = x.shape[0]
    nout = w.shape[1]
    pt, tps = prompt_rows // tm, seq_len // tm
    out_shape = [jax.ShapeDtypeStruct((n, nout), f32)]
    out_specs = [pl.BlockSpec((tm, nout), lambda i: (i, 0))]
    if with_h:
        out_shape.append(jax.ShapeDtypeStruct((n, D), f32))
        out_specs.append(pl.BlockSpec((tm, D), lambda i: (i, 0)))
    res = pl.pallas_call(
        functools.partial(_normmm_body, shift_idx=shift_idx, precise=precise),
        grid=(n // tm,),
        in_specs=[pl.BlockSpec((tm, D), lambda i: (i, 0)),
                  pl.BlockSpec((None, 6, D), lambda i: (_mod_row(i, pt, tps), 0, 0)),
                  pl.BlockSpec((1, D), lambda i: (0, 0)),
                  pl.BlockSpec((D, nout), lambda i: (0, 0))],
        out_specs=out_specs,
        out_shape=out_shape,
        compiler_params=_cp("arbitrary"),
        name="normmm",
    )(x, mods_l, norm_w.reshape(1, D), w)
    return res if with_h else res[0]


def _outproj_body(*refs, n_in, gate_idx):
    x_ref, m_ref = refs[0], refs[1]
    a_refs = refs[2:2 + n_in]
    w_refs = refs[2 + n_in:2 + 2 * n_in]
    o_ref = refs[2 + 2 * n_in]
    y = _dot(a_refs[0][...].astype(bf16), w_refs[0][...])
    for a_ref, w_ref in zip(a_refs[1:], w_refs[1:]):
        y = y + _dot(a_ref[...].astype(bf16), w_ref[...])
    o_ref[...] = x_ref[...] + m_ref[gate_idx:gate_idx + 1, :] * y


def outproj(x, mods_l, gate_idx, acts, ws, *, prompt_rows, seq_len, tm=256):
    n = x.shape[0]
    pt, tps = prompt_rows // tm, seq_len // tm
    in_specs = [pl.BlockSpec((tm, D), lambda i: (i, 0)),
                pl.BlockSpec((None, 6, D), lambda i: (_mod_row(i, pt, tps), 0, 0))]
    in_specs += [pl.BlockSpec((tm, a.shape[1]), lambda i: (i, 0)) for a in acts]
    in_specs += [pl.BlockSpec(w.shape, lambda i: (0, 0)) for w in ws]
    return pl.pallas_call(
        functools.partial(_outproj_body, n_in=len(acts), gate_idx=gate_idx),
        grid=(n // tm,),
        in_specs=in_specs,
        out_specs=pl.BlockSpec((tm, D), lambda i: (i, 0)),
        out_shape=jax.ShapeDtypeStruct((n, D), f32),
        compiler_params=_cp("arbitrary"),
        name="outproj",
    )(x, mods_l, *acts, *ws)


def _softplus(x):
    return jnp.maximum(x, 0.0) + jnp.log1p(jnp.exp(-jnp.abs(x)))


def _gdn_body(nega_ref, dt_ref, q_ref, k_ref, v_ref, gate_ref, ba_ref, cwq_ref, cwk_ref, cwv_ref, onw_ref,
              *rest, seq, has_s0):
    if has_s0:
        s0_ref, rest = rest[0], rest[1:]
    out_ref, sout_ref, qs, ks, vs, o_scr = rest
    h = pl.program_id(1)
    d = pl.program_id(2)
    C = GDN_CHUNK
    nch = seq // C

    @pl.when(d == 0)
    def _prep():
        row = lax.broadcasted_iota(i32, (seq, LANES), 0)
        first = row == 0
        last = row == seq - 1

        def conv_silu(x_ref, w_ref):
            x = x_ref[...]
            w = w_ref[...]
            prev = jnp.where(first, 0.0, pltpu.roll(x, 1, 0))
            nxt = jnp.where(last, 0.0, pltpu.roll(x, seq - 1, 0))
            y = prev * w[0:1, :] + x * w[1:2, :] + nxt * w[2:3, :]
            return y * jax.nn.sigmoid(y)

        def l2n(x):
            return x * lax.rsqrt(jnp.sum(x * x, -1, keepdims=True) + EPS)

        qs[...] = l2n(conv_silu(q_ref, cwq_ref)) * (GDN_DK ** -0.5)
        ks[...] = l2n(conv_silu(k_ref, cwk_ref))
        vs[...] = conv_silu(v_ref, cwv_ref)

    sgn = 1 - 2 * d
    ii = lax.broadcasted_iota(i32, (C, C), 0)
    jj = lax.broadcasted_iota(i32, (C, C), 1)
    rel = (ii - jj) * sgn
    m_incl = rel >= 0
    m_strict = rel > 0
    tri = m_incl.astype(f32)
    eye = (ii == jj).astype(f32)
    sel0 = (lax.broadcasted_iota(i32, (C, LANES), 1) == 0).astype(f32)
    nega = nega_ref[d, h]
    dtb = dt_ref[d, h]
    is_f = d == 0

    def chunk(i, S):
        c = jnp.where(is_f, i, nch - 1 - i)
        r0 = pl.multiple_of(c * C, C)
        qc = qs[pl.ds(r0, C), :]
        kc = ks[pl.ds(r0, C), :]
        vc = vs[pl.ds(r0, C), :]
        ba = ba_ref[pl.ds(r0, C), :]
        beta = jax.nn.sigmoid(jnp.where(is_f, ba[:, 0:1], ba[:, 1:2]))
        araw = jnp.where(is_f, ba[:, 2:3], ba[:, 3:4])
        g = nega * _softplus(araw + dtb)
        gB = jnp.broadcast_to(g, (C, LANES))
        gcB = _dot(tri, gB, precision=HI)
        gc = gcB[:, 0:1]
        gc_row = _dot(sel0, gcB, NT, precision=HI)
        decay = jnp.exp(jnp.where(m_incl, gc - gc_row, -jnp.inf))
        kb = kc * beta
        A = jnp.where(m_strict, _dot(kb, kc, NT, precision=HI) * decay, 0.0)
        X = eye - A
        P = _dot(A, A, precision=HI)
        for _ in range(4):
            X = X + _dot(X, P, precision=HI)
            P = _dot(P, P, precision=HI)
        X = X + _dot(X, P, precision=HI)
        rhs = jnp.concatenate([vc * beta, kb * jnp.exp(gc)], axis=1)
        sol = _dot(X, rhs, precision=HI)
        u = sol[:, :GDN_DK]
        w = sol[:, GDN_DK:]
        qk = jnp.where(m_incl, _dot(qc, kc, NT, precision=HI) * decay, 0.0)
        v_new = u - _dot(w, S, precision=HI)
        o = _dot(qc * jnp.exp(gc), S, precision=HI) + _dot(qk, v_new, precision=HI)
        o_scr[d, pl.ds(r0, C), :] = o
        g_last = jnp.where(is_f, gcB[C - 1:C, :], gcB[0:1, :])
        ke = kc * jnp.exp(g_last - gcB)
        return S * jnp.exp(g_last) + _dot(ke, v_new, TN, precision=HI)

    if has_s0:
        S0 = s0_ref[...]
    else:
        S0 = jnp.zeros((GDN_DK, GDN_DK), f32)
    S = lax.fori_loop(0, nch, chunk, S0)
    sout_ref[...] = S

    @pl.when(d == 1)
    def _fin():
        o = o_scr[0] + o_scr[1]
        y = o * lax.rsqrt(jnp.mean(o * o, -1, keepdims=True) + EPS) * onw_ref[...]
        gt = gate_ref[...]
        out_ref[...] = y * (gt * jax.nn.sigmoid(gt))


def gdn(p, nega, dtb, conv_w, onw, s0, *, row_off, batch, seq, ba_col0):
    rb = row_off // seq
    H = GDN_H

    def col(cb):
        return pl.BlockSpec((seq, LANES), lambda b, h, d: (rb + b, cb + h))

    in_specs = [pl.BlockSpec(memory_space=pltpu.SMEM), pl.BlockSpec(memory_space=pltpu.SMEM),
                col(0), col(H), col(2 * H), col(3 * H), col(ba_col0),
                pl.BlockSpec((3, LANES), lambda b, h, d: (0, h)),
                pl.BlockSpec((3, LANES), lambda b, h, d: (0, H + h)),
                pl.BlockSpec((3, LANES), lambda b, h, d: (0, 2 * H + h)),
                pl.BlockSpec((1, LANES), lambda b, h, d: (0, 0))]
    args = [nega, dtb, p, p, p, p, p, conv_w, conv_w, conv_w, onw.reshape(1, LANES)]
    if s0 is not None:
        in_specs.append(pl.BlockSpec((None, None, None, GDN_DK, GDN_DK), lambda b, h, d: (b, d, h, 0, 0)))
        args.append(s0)
    return pl.pallas_call(
        functools.partial(_gdn_body, seq=seq, has_s0=s0 is not None),
        grid=(batch, H, 2),
        in_specs=in_specs,
        out_specs=[pl.BlockSpec((seq, LANES), lambda b, h, d: (b, h)),
                   pl.BlockSpec((None, None, None, GDN_DK, GDN_DK), lambda b, h, d: (b, d, h, 0, 0))],
        out_shape=[jax.ShapeDtypeStruct((batch * seq, H * LANES), f32),
                   jax.ShapeDtypeStruct((batch, 2, H, GDN_DK, GDN_DK), f32)],
        scratch_shapes=[pltpu.VMEM((seq, LANES), f32), pltpu.VMEM((seq, LANES), f32), pltpu.VMEM((seq, LANES), f32),
                        pltpu.VMEM((2, seq, LANES), f32)],
        compiler_params=_cp("arbitrary", "arbitrary", "arbitrary"),
        name="gdn",
    )(*args)


def _fnet_body(x_ref, cd_ref, f_ref, o_ref, z_scr, *, seq):
    r = pl.program_id(1)

    @pl.when(r == 0)
    def _stage1():
        for g in range(FN_W // LANES):
            xg = x_ref[:, g * LANES:(g + 1) * LANES].astype(bf16)
            y = _dot(xg, cd_ref[...])
            z_scr[0:seq, g * LANES:(g + 1) * LANES] = y[:, :LANES].astype(bf16)
            z_scr[seq:2 * seq, g * LANES:(g + 1) * LANES] = y[:, LANES:].astype(bf16)

    scale = 1.0 / math.sqrt(seq * LANES)
    o_ref[...] = _dot(f_ref[...], z_scr[...]) * scale


def fnet(p, cd, fmat, *, row_off, batch, seq, col_block, tr):
    rb = row_off // seq
    return pl.pallas_call(
        functools.partial(_fnet_body, seq=seq),
        grid=(batch, seq // tr),
        in_specs=[pl.BlockSpec((seq, FN_W), lambda b, r: (rb + b, col_block)),
                  pl.BlockSpec((LANES, 2 * LANES), lambda b, r: (0, 0)),
                  pl.BlockSpec((tr, 2 * seq), lambda b, r: (r, 0))],
        out_specs=pl.BlockSpec((tr, FN_W), lambda b, r: (b * (seq // tr) + r, 0)),
        out_shape=jax.ShapeDtypeStruct((batch * seq, FN_W), f32),
        scratch_shapes=[pltpu.VMEM((2 * seq, FN_W), bf16)],
        compiler_params=_cp("arbitrary", "arbitrary"),
        name="fnet",
    )(p, cd, fmat)


def _dft_mats(n):
    k = jnp.arange(n, dtype=i32)
    ang = ((k[:, None] * k[None, :]) % n).astype(f32) * (2.0 * math.pi / n)
    return jnp.cos(ang), jnp.sin(ang)


def _rope(x, cos, sin_signed, first_half):
    partner = jnp.where(first_half, pltpu.roll(x, LANES - MLA_ROPE // 4, 1), pltpu.roll(x, MLA_ROPE // 4, 1))
    return x * cos + partner * sin_signed


def _mla_body(*refs, has_q, normalize, use_rope):
    it = iter(refs)
    if normalize:
        p_ref = next(it)
        qnw_ref = next(it)
        kvnw_ref = next(it)
    else:
        ckv_ref = next(it)
        krp_ref = next(it)
    if has_q:
        wqn_ref, wqr_ref, qhn_ref, qhr_ref = next(it), next(it), next(it), next(it)
    wkn_ref, wv_ref, khn_ref, khr_ref = next(it), next(it), next(it), next(it)
    if use_rope:
        cos_ref, sin_ref = next(it), next(it)
    if has_q:
        qf_ref = next(it)
    kf_ref, v_ref = next(it), next(it)
    if normalize:
        ckv_out, krp_out = next(it), next(it)

    def rms(x):
        return x * lax.rsqrt(jnp.mean(x * x, -1, keepdims=True) + EPS)

    if normalize:
        p = p_ref[...]
        cq = rms(p[:, :MLA_QL]) * qnw_ref[...]
        ckv = rms(p[:, MLA_QL:MLA_QL + MLA_KVL]) * kvnw_ref[...]
        krp = p[:, MLA_QL + MLA_KVL:]
        ckv_out[...] = ckv
        krp_out[...] = krp
    else:
        ckv = ckv_ref[...]
        krp = krp_ref[...]
    if use_rope:
        cos = cos_ref[...]
        sin = sin_ref[...]
        lane = lax.broadcasted_iota(i32, cos.shape, 1)
        first_half = (lane % (MLA_ROPE // 2)) < (MLA_ROPE // 4)

    def head_norm(a, b, wn, wr, scale):
        ss = jnp.sum(a * a, -1, keepdims=True) + jnp.sum(b * b, -1, keepdims=True)
        r = lax.rsqrt(ss * (1.0 / MLA_QK) + EPS)
        a = a * r * wn
        b = b * r * wr
        if use_rope:
            b = _rope(b, cos, sin, first_half)
        return (a * scale).astype(bf16), (b * scale).astype(bf16)

    if has_q:
        cqb = cq.astype(bf16)
        qn = _dot(cqb, wqn_ref[...])
        qr = _dot(cqb, wqr_ref[...])
        for h in range(MLA_H):
            a, b = head_norm(qn[:, h * LANES:(h + 1) * LANES], qr[:, h * LANES:(h + 1) * LANES],
                             qhn_ref[...], qhr_ref[...], MLA_QK ** -0.5)
            qf_ref[:, 2 * h * LANES:(2 * h + 1) * LANES] = a
            qf_ref[:, (2 * h + 1) * LANES:(2 * h + 2) * LANES] = b
    ckvb = ckv.astype(bf16)
    kn = _dot(ckvb, wkn_ref[...])
    v_ref[...] = _dot(ckvb, wv_ref[...]).astype(bf16)
    for h in range(MLA_H):
        a, b = head_norm(kn[:, h * LANES:(h + 1) * LANES], krp, khn_ref[...], khr_ref[...], 1.0)
        kf_ref[:, 2 * h * LANES:(2 * h + 1) * LANES] = a
        kf_ref[:, (2 * h + 1) * LANES:(2 * h + 2) * LANES] = b


def mla_prep(srcs, norm_ws, q_ws, kv_ws, rope_tabs, *, row_off, rows, seq, tm=256):
    normalize = norm_ws is not None
    has_q = q_ws is not None
    use_rope = rope_tabs is not None
    ro = row_off // tm
    nt = rows // tm
    hd = MLA_H * 2 * LANES

    def full(a):
        return pl.BlockSpec(a.shape, lambda i: (0,) * a.ndim)

    args, in_specs = [], []
    for s in srcs:
        args.append(s)
        in_specs.append(pl.BlockSpec((tm, s.shape[1]), lambda i: (ro + i, 0)))
    for group in (norm_ws, q_ws, kv_ws):
        if group is not None:
            for a in group:
                args.append(a)
                in_specs.append(full(a))
    if use_rope:
        tps = seq // tm
        for a in rope_tabs:
            args.append(a)
            in_specs.append(pl.BlockSpec((tm, LANES), lambda i: (i % tps, 0)))
    out_shape, out_specs = [], []

    def out(cols, dt):
        out_shape.append(jax.ShapeDtypeStruct((rows, cols), dt))
        out_specs.append(pl.BlockSpec((tm, cols), lambda i: (i, 0)))

    if has_q:
        out(hd, bf16)
    out(hd, bf16)
    out(MLA_H * LANES, bf16)
    if normalize:
        out(MLA_KVL, f32)
        out(LANES, f32)
    return pl.pallas_call(
        functools.partial(_mla_body, has_q=has_q, normalize=normalize, use_rope=use_rope),
        grid=(nt,),
        in_specs=in_specs,
        out_specs=out_specs,
        out_shape=out_shape,
        compiler_params=_cp("arbitrary"),
        name="mla_prep",
    )(*args)


def _attn_body(*refs, n_kv):
    q_ref = refs[0]
    k_refs = refs[1:1 + n_kv]
    v_refs = refs[1 + n_kv:1 + 2 * n_kv]
    o_ref = refs[1 + 2 * n_kv]
    q = q_ref[...]
    ss = [_dot(q, k_ref[...], NT) for k_ref in k_refs]
    m = jnp.max(ss[0], -1, keepdims=True)
    for s in ss[1:]:
        m = jnp.maximum(m, jnp.max(s, -1, keepdims=True))
    l = None
    acc = None
    for s, v_ref in zip(ss, v_refs):
        e = jnp.exp(s - m)
        ls = jnp.sum(e, -1, keepdims=True)
        l = ls if l is None else l + ls
        pv = _dot(e.astype(bf16), v_ref[...])
        acc = pv if acc is None else acc + pv
    o_ref[...] = acc / l


def attend(qf, kfs, vs, *, batch, lq, lks, tq=256):
    n_kv = len(kfs)
    nq = lq // tq
    in_specs = [pl.BlockSpec((tq, 2 * LANES), lambda b, h, i: (b * nq + i, h))]
    in_specs += [pl.BlockSpec((lk, 2 * LANES), lambda b, h, i: (b, h)) for lk in lks]
    in_specs += [pl.BlockSpec((lk, LANES), lambda b, h, i: (b, h)) for lk in lks]
    return pl.pallas_call(
        functools.partial(_attn_body, n_kv=n_kv),
        grid=(batch, MLA_H, nq),
        in_specs=in_specs,
        out_specs=pl.BlockSpec((tq, LANES), lambda b, h, i: (b * nq + i, h)),
        out_shape=jax.ShapeDtypeStruct((batch * lq, MLA_H * LANES), f32),
        compiler_params=_cp("arbitrary", "arbitrary", "arbitrary"),
        name="attend",
    )(qf, *kfs, *vs)


def _topk_rows(s, k, val_scr, idx_scr, payload=None):
    rows = s.shape[0]
    iota = lax.broadcasted_iota(i32, s.shape, 0)
    for r in range(k):
        m = jnp.max(s, axis=0, keepdims=True)
        idx = jnp.min(jnp.where(s == m, iota, rows), axis=0, keepdims=True)
        hit = iota == idx
        val_scr[r:r + 1, :] = m
        if payload is None:
            idx_scr[r:r + 1, :] = idx
        else:
            idx_scr[r:r + 1, :] = jnp.max(jnp.where(hit, payload, -1), axis=0, keepdims=True)
        s = jnp.where(hit, -jnp.inf, s)


def _peer_topk_body(q_ref, keys_ref, eidx_ref, gate_ref, sv, si, cv, ci, e_all, g_all):
    K = PEER_K
    for h in range(PEER_H):
        for p in range(2):
            qhp = q_ref[:, (2 * h + p) * LANES:(2 * h + p + 1) * LANES]
            s = _dot(keys_ref[h, p], qhp, NT, precision=HI)
            _topk_rows(s, K, sv.at[p], si.at[p])
        sv0, sv1 = sv[0], sv[1]
        si0, si1 = si[0], si[1]
        cand = jnp.concatenate([sv0[a:a + 1, :] + sv1 for a in range(K)], axis=0)
        cidx = jnp.concatenate([si0[a:a + 1, :] * PEER_NK + si1 for a in range(K)], axis=0)
        _topk_rows(cand, K, cv, ci, payload=cidx)
        cs = cv[...]
        e = jnp.exp(cs - cs[0:1, :])
        g_all[h * K:(h + 1) * K, :] = e / jnp.sum(e, axis=0, keepdims=True)
        e_all[h * K:(h + 1) * K, :] = ci[...]
    eidx_ref[...] = e_all[...].T
    gate_ref[...] = g_all[...].T


def peer_topk(q, sub_keys, *, tt=256):
    n = q.shape[0]
    K = PEER_K
    return pl.pallas_call(
        _peer_topk_body,
        grid=(n // tt,),
        in_specs=[pl.BlockSpec((tt, q.shape[1]), lambda i: (i, 0)),
                  pl.BlockSpec(sub_keys.shape, lambda i: (0, 0, 0, 0))],
        out_specs=[pl.BlockSpec((tt, PEER_SEL), lambda i: (i, 0)),
                   pl.BlockSpec((tt, PEER_SEL), lambda i: (i, 0))],
        out_shape=[jax.ShapeDtypeStruct((n, PEER_SEL), i32), jax.ShapeDtypeStruct((n, PEER_SEL), f32)],
        scratch_shapes=[pltpu.VMEM((2, K, tt), f32), pltpu.VMEM((2, K, tt), i32),
                        pltpu.VMEM((K, tt), f32), pltpu.VMEM((K, tt), i32),
                        pltpu.VMEM((PEER_SEL, tt), i32), pltpu.VMEM((PEER_SEL, tt), f32)],
        compiler_params=_cp("arbitrary"),
        name="peer_topk",
    )(q, sub_keys)


def _expert_row(tab_ref, e):
    pair = tab_ref[pl.ds(pl.multiple_of((e >> 1) * (2 * SUBLANES), 2 * SUBLANES), 2 * SUBLANES), :].astype(f32)
    return jnp.where((e & 1) == 1, pair[SUBLANES:], pair[:SUBLANES])


def _gelu_tanh(x):
    return 0.5 * x * (1.0 + jnp.tanh(math.sqrt(2.0 / math.pi) * (x + 0.044715 * (x * x * x))))


def _peer_u_body(idx_ref, gate_ref, h_ref, tab_ref, w_ref, *, tb):
    ones = jnp.ones((SUBLANES, LANES), f32)

    def token(t, carry):
        r8 = pl.multiple_of(t * SUBLANES, SUBLANES)
        hrow = h_ref[pl.ds(r8, SUBLANES), :]
        groups = []
        for g in range(PEER_SEL // SUBLANES):
            rows = []
            for j in range(SUBLANES):
                e = idx_ref[t, g * SUBLANES + j]
                rows.append(jnp.sum(_expert_row(tab_ref, e) * hrow, axis=0, keepdims=True))
            groups.append(jnp.concatenate(rows, axis=0))
        z = jnp.concatenate(groups, axis=0)
        act = _dot(ones, z, NT, precision=HI)[0:1, :]
        w_ref[pl.ds(t, 1), :] = gate_ref[pl.ds(t, 1), :] * _gelu_tanh(act)
        return carry

    lax.fori_loop(0, tb, token, 0)


def peer_u_gather(eidx, gate, h8, tab, *, tb=256):
    n = eidx.shape[0]
    return pl.pallas_call(
        functools.partial(_peer_u_body, tb=tb),
        grid=(n // tb,),
        in_specs=[pl.BlockSpec((tb, PEER_SEL), lambda i: (i, 0), memory_space=pltpu.SMEM),
                  pl.BlockSpec((tb, PEER_SEL), lambda i: (i, 0)),
                  pl.BlockSpec((tb * SUBLANES, LANES), lambda i: (i, 0)),
                  pl.BlockSpec(tab.shape, lambda i: (0, 0), pipeline_mode=pl.Buffered(1))],
        out_specs=pl.BlockSpec((tb, PEER_SEL), lambda i: (i, 0)),
        out_shape=jax.ShapeDtypeStruct((n, PEER_SEL), f32),
        compiler_params=_cp("arbitrary"),
        name="peer_u",
    )(eidx, gate, h8, tab)


def _peer_v_body(idx_ref, w_ref, x_ref, g_ref, tab_ref, o_ref, *, tb):
    g8 = g_ref[...]

    def token(t, carry):
        r8 = pl.multiple_of(t * SUBLANES, SUBLANES)
        acc = jnp.zeros((SUBLANES, LANES), f32)
        for j in range(PEER_SEL):
            acc = acc + w_ref[t, j] * _expert_row(tab_ref, idx_ref[t, j])
        o_ref[pl.ds(r8, SUBLANES), :] = x_ref[pl.ds(r8, SUBLANES), :] + g8 * acc
        return carry

    lax.fori_loop(0, tb, token, 0)


def peer_v_gather(eidx, w, x8, g8_all, tab, *, prompt_rows, seq_len, tb=256):
    n = eidx.shape[0]
    pt, tps = prompt_rows // tb, seq_len // tb
    return pl.pallas_call(
        functools.partial(_peer_v_body, tb=tb),
        grid=(n // tb,),
        in_specs=[pl.BlockSpec((tb, PEER_SEL), lambda i: (i, 0), memory_space=pltpu.SMEM),
                  pl.BlockSpec((tb, PEER_SEL), lambda i: (i, 0), memory_space=pltpu.SMEM),
                  pl.BlockSpec((tb * SUBLANES, LANES), lambda i: (i, 0)),
                  pl.BlockSpec((None, SUBLANES, LANES), lambda i: (_mod_row(i, pt, tps), 0, 0)),
                  pl.BlockSpec(tab.shape, lambda i: (0, 0), pipeline_mode=pl.Buffered(1))],
        out_specs=pl.BlockSpec((tb * SUBLANES, LANES), lambda i: (i, 0)),
        out_shape=jax.ShapeDtypeStruct((n * SUBLANES, LANES), f32),
        compiler_params=_cp("arbitrary"),
        name="peer_v",
    )(eidx, w, x8, g8_all, tab)


def _rope_tables(seq):
    rows = seq // GRID_W
    row = jnp.repeat(jnp.arange(rows, dtype=f32), GRID_W)
    col = jnp.tile(jnp.arange(GRID_W, dtype=f32), rows)
    nfreq = MLA_ROPE // 4
    inv = jnp.power(ROPE_BASE, -jnp.arange(nfreq, dtype=f32) / nfreq)
    ar = row[:, None] * inv
    ac = col[:, None] * inv
    pad1 = jnp.ones((seq, LANES - MLA_ROPE), f32)
    pad0 = jnp.zeros((seq, LANES - MLA_ROPE), f32)
    cos = jnp.concatenate([jnp.cos(ar), jnp.cos(ar), jnp.cos(ac), jnp.cos(ac), pad1], -1)
    sin = jnp.concatenate([-jnp.sin(ar), jnp.sin(ar), -jnp.sin(ac), jnp.sin(ac), pad0], -1)
    return cos, sin


def _pad_heads(w, head_w, lo, hi):
    k = w.shape[0]
    w = w.reshape(k, -1, head_w)[:, :, lo:hi]
    return jnp.pad(w, ((0, 0), (0, 0), (0, LANES - (hi - lo)))).reshape(k, -1)


def kernel(x_prompt, x_sample, state_gdn, cache_mla_ckv, cache_mla_krope, c, c_ctx, w_mod, b_mod, norm_mix, norm_ffn, even_w_in, even_conv_w, gdn_a_log, gdn_dt_bias, gdn_o_norm, even_w_out, odd_w_in, mla_q_norm, mla_kv_norm, mla_w_uq, mla_w_ukv, mla_q_headnorm, mla_k_headnorm, odd_w_out, peer_w_q, peer_sub_keys, peer_u, peer_v):
    B, L, _ = x_prompt.shape
    BS, LS, _ = x_sample.shape
    depth = w_mod.shape[0]
    NP, NS = B * L, BS * LS
    past = cache_mla_ckv.shape[2]
    geo = dict(prompt_rows=NP, seq_len=LS)

    x = jnp.concatenate([x_prompt.reshape(NP, D), x_sample.reshape(NS, D)], 0)
    nrow = 1 + BS
    rpad = -nrow % SUBLANES
    cv = jnp.concatenate([c_ctx[None, :], c, jnp.zeros((rpad, D), f32)], 0)
    mods = ada_mods_all(cv, w_mod, b_mod).reshape(depth, nrow + rpad, 6, D)

    cd_c, cd_s = _dft_mats(LANES)
    cd = jnp.concatenate([cd_c, cd_s], 1).astype(bf16)
    fmats = {}
    for n in (L, LS):
        cl, sl = _dft_mats(n)
        fmats[n] = jnp.concatenate([cl, -sl], 1).astype(bf16)
    cos_t, sin_t = _rope_tables(LS)

    new_gdn, new_ckv, new_kr = [], [], []
    for l in range(depth):
        j = l // 2
        m_l = mods[l]
        if l % 2 == 0:
            wi = even_w_in[j]
            o0 = 3 * A_QK + A_QK
            ba = wi[:, o0:o0 + 4 * GDN_H].reshape(D, 4, GDN_H).transpose(0, 2, 1)
            ba = jnp.pad(ba, ((0, 0), (0, 0), (0, LANES - 4))).reshape(D, GDN_H * LANES)
            w_in = jnp.concatenate([wi[:, :o0], wi[:, o0 + 4 * GDN_H:], ba], 1).astype(bf16)
            p = normmm(x, m_l, 0, norm_mix[l], w_in, **geo)
            nega = -jnp.exp(gdn_a_log[j])
            kw = dict(ba_col0=(o0 + FN_W) // LANES)
            mix_p, st = gdn(p, nega, gdn_dt_bias[j], even_conv_w[j], gdn_o_norm[j], None,
                            row_off=0, batch=B, seq=L, **kw)
            mix_s, _ = gdn(p, nega, gdn_dt_bias[j], even_conv_w[j], gdn_o_norm[j], state_gdn[:, j],
                           row_off=NP, batch=BS, seq=LS, **kw)
            new_gdn.append(st)
            fb_p = fnet(p, cd, fmats[L], row_off=0, batch=B, seq=L, col_block=o0 // FN_W, tr=L)
            fb_s = fnet(p, cd, fmats[LS], row_off=NP, batch=BS, seq=LS, col_block=o0 // FN_W, tr=256)
            wo = even_w_out[j].astype(bf16)
            x = outproj(x, m_l, 2, [jnp.concatenate([mix_p, mix_s], 0), jnp.concatenate([fb_p, fb_s], 0)],
                        [wo[:A_QK], wo[A_QK:]], **geo)
        else:
            wi = odd_w_in[j]
            w_in = jnp.pad(wi, ((0, 0), (0, LANES - MLA_ROPE))).astype(bf16)
            p = normmm(x, m_l, 0, norm_mix[l], w_in, **geo)
            norm_ws = (mla_q_norm[j].reshape(1, -1), mla_kv_norm[j].reshape(1, -1))
            qh, kh = mla_q_headnorm[j], mla_k_headnorm[j]

            def split_hw(hw):
                return (hw[:MLA_NOPE].reshape(1, LANES),
                        jnp.pad(hw[MLA_NOPE:], (0, LANES - MLA_ROPE)).reshape(1, LANES))

            q_ws = (_pad_heads(mla_w_uq[j], MLA_QK, 0, MLA_NOPE).astype(bf16),
                    _pad_heads(mla_w_uq[j], MLA_QK, MLA_NOPE, MLA_QK).astype(bf16)) + split_hw(qh)
            kv_ws = (_pad_heads(mla_w_ukv[j], 2 * LANES, 0, LANES).astype(bf16),
                     _pad_heads(mla_w_ukv[j], 2 * LANES, LANES, 2 * LANES).astype(bf16)) + split_hw(kh)
            qf_p, kf_p, v_p, ckv_p, krp_p = mla_prep((p,), norm_ws, q_ws, kv_ws, None, row_off=0, rows=NP, seq=L)
            qf_s, kf_s, v_s, _, _ = mla_prep((p,), norm_ws, q_ws, kv_ws, (cos_t, sin_t), row_off=NP, rows=NS, seq=LS)
            ckv_c = cache_mla_ckv[:, j].reshape(BS * past, MLA_KVL)
            krp_c = jnp.pad(cache_mla_krope[:, j].reshape(BS * past, MLA_ROPE), ((0, 0), (0, LANES - MLA_ROPE)))
            kf_c, v_c = mla_prep((ckv_c, krp_c), None, None, kv_ws, None, row_off=0, rows=BS * past, seq=past)
            o_p = attend(qf_p, [kf_p], [v_p], batch=B, lq=L, lks=[L])
            o_s = attend(qf_s, [kf_s, kf_c], [v_s, v_c], batch=BS, lq=LS, lks=[LS, past])
            new_ckv.append(ckv_p.reshape(B, L, MLA_KVL))
            new_kr.append(krp_p[:, :MLA_ROPE].reshape(B, L, MLA_ROPE))
            x = outproj(x, m_l, 2, [jnp.concatenate([o_p, o_s], 0)], [odd_w_out[j].astype(bf16)], **geo)
        q, h = normmm(x, m_l, 3, norm_ffn[l], peer_w_q[l], precise=True, with_h=True, **geo)
        eidx, gate = peer_topk(q, peer_sub_keys[l])
        nexp = peer_u.shape[1]
        u8 = peer_u[l].astype(bf16).reshape(nexp * SUBLANES, LANES)
        v8 = peer_v[l].astype(bf16).reshape(nexp * SUBLANES, LANES)
        w = peer_u_gather(eidx, gate, h.reshape(-1, LANES), u8)
        g8 = m_l[:, 5].reshape(-1, SUBLANES, LANES)
        x = peer_v_gather(eidx, w, x.reshape(-1, LANES), g8, v8, **geo).reshape(-1, D)

    y_prompt = x[:NP].reshape(B, L, D)
    y_sample = x[NP:].reshape(BS, LS, D)
    return (y_prompt, y_sample, jnp.stack(new_gdn, 1), jnp.stack(new_ckv, 1), jnp.stack(new_kr, 1))
```

```python
import functools
import math

import jax
import jax.numpy as jnp
from jax import lax
from jax.experimental import pallas as pl
from jax.experimental.pallas import tpu as pltpu

f32 = jnp.float32
bf16 = jnp.bfloat16
i32 = jnp.int32

LANES = 128
SUBLANES = 8
VMEM_LIMIT = 56 * 1024 * 1024

EPS = 1e-6
D = 1024
GRID_W = 64
GDN_H = 4
GDN_DK = 128
GDN_CHUNK = 64
A_QK = GDN_H * GDN_DK
FN_W = 512
MLA_H = 8
MLA_QL = 512
MLA_KVL = 256
MLA_NOPE = 128
MLA_ROPE = 64
MLA_QK = MLA_NOPE + MLA_ROPE
ROPE_BASE = 10000.0
PEER_H = 8
PEER_NK = 128
PEER_K = 16
PEER_SEL = PEER_H * PEER_K

HI = lax.Precision.HIGHEST
NT = (((1,), (1,)), ((), ()))
TN = (((0,), (0,)), ((), ()))
NN = (((1,), (0,)), ((), ()))


def _dot(a, b, dims=NN, precision=None):
    return lax.dot_general(a, b, dims, preferred_element_type=f32, precision=precision)


def _cp(*sem):
    return pltpu.CompilerParams(dimension_semantics=sem, vmem_limit_bytes=VMEM_LIMIT)


def _mod_row(i, prompt_tiles, tiles_per_seq):
    return jnp.where(i < prompt_tiles, 0, 1 + (i - prompt_tiles) // tiles_per_seq)


def _mods_body(c_ref, w_ref, b_ref, o_ref):
    c = c_ref[...]
    o_ref[...] = _dot(c * jax.nn.sigmoid(c), w_ref[...], precision=HI) + b_ref[...]


def ada_mods_all(cv, w_mod, b_mod):
    depth, _, n6 = w_mod.shape
    r = cv.shape[0]
    tn = 1536
    return pl.pallas_call(
        _mods_body,
        grid=(depth, n6 // tn),
        in_specs=[pl.BlockSpec((r, D), lambda l, j: (0, 0)),
                  pl.BlockSpec((None, D, tn), lambda l, j: (l, 0, j)),
                  pl.BlockSpec((None, 1, tn), lambda l, j: (l, 0, j))],
        out_specs=pl.BlockSpec((None, r, tn), lambda l, j: (l, 0, j)),
        out_shape=jax.ShapeDtypeStruct((depth, r, n6), f32),
        compiler_params=_cp("arbitrary", "arbitrary"),
        name="ada_mods",
    )(cv, w_mod, b_mod.reshape(depth, 1, n6))


def _normmm_body(x_ref, m_ref, nw_ref, w_ref, o_ref, *h_refs, shift_idx, precise):
    x = x_ref[...]
    y = x * lax.rsqrt(jnp.mean(x * x, -1, keepdims=True) + EPS) * nw_ref[...]
    h = y * (1.0 + m_ref[shift_idx + 1:shift_idx + 2, :]) + m_ref[shift_idx:shift_idx + 1, :]
    if precise:
        o_ref[...] = _dot(h, w_ref[...], precision=HI)
    else:
        o_ref[...] = _dot(h.astype(bf16), w_ref[...])
    if h_refs:
        h_refs[0][...] = h


def normmm(x, mods_l, shift_idx, norm_w, w, *, prompt_rows, seq_len, tm=256, precise=False, with_h=False):
    n = x.shape[0]
    nout = w.shape[1]
    pt, tps = prompt_rows // tm, seq_len // tm
    out_shape = [jax.ShapeDtypeStruct((n, nout), f32)]
    out_specs = [pl.BlockSpec((tm, nout), lambda i: (i, 0))]
    if with_h:
        out_shape.append(jax.ShapeDtypeStruct((n, D), f32))
        out_specs.append(pl.BlockSpec((tm, D), lambda i: (i, 0)))
    res = pl.pallas_call(
        functools.partial(_normmm_body, shift_idx=shift_idx, precise=precise),
        grid=(n // tm,),
        in_specs=[pl.BlockSpec((tm, D), lambda i: (i, 0)),
                  pl.BlockSpec((None, 6, D), lambda i: (_mod_row(i, pt, tps), 0, 0)),
                  pl.BlockSpec((1, D), lambda i: (0, 0)),
                  pl.BlockSpec((D, nout), lambda i: (0, 0))],
        out_specs=out_specs,
        out_shape=out_shape,
        compiler_params=_cp("arbitrary"),
        name="normmm",
    )(x, mods_l, norm_w.reshape(1, D), w)
    return res if with_h else res[0]


def _outproj_body(*refs, n_in, gate_idx):
    x_ref, m_ref = refs[0], refs[1]
    a_refs = refs[2:2 + n_in]
    w_refs = refs[2 + n_in:2 + 2 * n_in]
    o_ref = refs[2 + 2 * n_in]
    y = _dot(a_refs[0][...].astype(bf16), w_refs[0][...])
    for a_ref, w_ref in zip(a_refs[1:], w_refs[1:]):
        y = y + _dot(a_ref[...].astype(bf16), w_ref[...])
    o_ref[...] = x_ref[...] + m_ref[gate_idx:gate_idx + 1, :] * y


def outproj(x, mods_l, gate_idx, acts, ws, *, prompt_rows, seq_len, tm=256):
    n = x.shape[0]
    pt, tps = prompt_rows // tm, seq_len // tm
    in_specs = [pl.BlockSpec((tm, D), lambda i: (i, 0)),
                pl.BlockSpec((None, 6, D), lambda i: (_mod_row(i, pt, tps), 0, 0))]
    in_specs += [pl.BlockSpec((tm, a.shape[1]), lambda i: (i, 0)) for a in acts]
    in_specs += [pl.BlockSpec(w.shape, lambda i: (0, 0)) for w in ws]
    return pl.pallas_call(
        functools.partial(_outproj_body, n_in=len(acts), gate_idx=gate_idx),
        grid=(n // tm,),
        in_specs=in_specs,
        out_specs=pl.BlockSpec((tm, D), lambda i: (i, 0)),
        out_shape=jax.ShapeDtypeStruct((n, D), f32),
        compiler_params=_cp("arbitrary"),
        name="outproj",
    )(x, mods_l, *acts, *ws)


def _softplus(x):
    return jnp.maximum(x, 0.0) + jnp.log1p(jnp.exp(-jnp.abs(x)))


def _gdn_body(nega_ref, dt_ref, q_ref, k_ref, v_ref, gate_ref, ba_ref, cwq_ref, cwk_ref, cwv_ref, onw_ref,
              *rest, seq, has_s0):
    if has_s0:
        s0_ref, rest = rest[0], rest[1:]
    out_ref, sout_ref, qs, ks, vs, us, ws, qes, kes, qks, egs = rest
    h = pl.program_id(1)
    C = GDN_CHUNK
    nch = seq // C

    row = lax.broadcasted_iota(i32, (seq, LANES), 0)
    first = row == 0
    last = row == seq - 1

    def conv_silu(x_ref, w_ref):
        x = x_ref[...]
        w = w_ref[...]
        prev = jnp.where(first, 0.0, pltpu.roll(x, 1, 0))
        nxt = jnp.where(last, 0.0, pltpu.roll(x, seq - 1, 0))
        y = prev * w[0:1, :] + x * w[1:2, :] + nxt * w[2:3, :]
        return y * jax.nn.sigmoid(y)

    def l2n(x):
        return x * lax.rsqrt(jnp.sum(x * x, -1, keepdims=True) + EPS)

    qs[...] = l2n(conv_silu(q_ref, cwq_ref)) * (GDN_DK ** -0.5)
    ks[...] = l2n(conv_silu(k_ref, cwk_ref))
    vs[...] = conv_silu(v_ref, cwv_ref)

    ii = lax.broadcasted_iota(i32, (C, C), 0)
    jj = lax.broadcasted_iota(i32, (C, C), 1)
    eye = (ii == jj).astype(f32)

    ones_cc = jnp.ones((C, C), f32)
    zeros_cc = jnp.zeros((C, C), f32)

    def prep(d, c):
        rel = (ii - jj) if d == 0 else (jj - ii)
        m_incl = rel >= 0
        m_strict = rel > 0
        tri = m_incl.astype(f32)
        tri_t = (rel <= 0).astype(f32)
        r0 = pl.multiple_of(c * C, C)
        qc = qs[pl.ds(r0, C), :]
        kc = ks[pl.ds(r0, C), :]
        vc = vs[pl.ds(r0, C), :]
        ba = ba_ref[pl.ds(r0, C), :]
        beta = jax.nn.sigmoid(ba[:, d:d + 1])
        g = nega_ref[d, h] * _softplus(ba[:, 2 + d:3 + d] + dt_ref[d, h])
        g1 = jnp.broadcast_to(g, (C, C))
        lhs = jnp.concatenate([tri, -ones_cc], axis=1)
        rhs = jnp.concatenate([jnp.concatenate([g1, g1, g1], axis=1),
                               jnp.concatenate([zeros_cc, zeros_cc, g1 * tri_t], axis=1)], axis=0)
        R = _dot(lhs, rhs, precision=HI)
        gcB = R[:, :LANES]
        gc = gcB[:, 0:1]
        decay = jnp.exp(jnp.where(m_incl, R[:, LANES:LANES + C], -jnp.inf))
        kb = kc * beta
        aq = _dot(jnp.concatenate([kb, qc], axis=0), kc, NT, precision=HI)
        A = jnp.where(m_strict, aq[:C] * decay, 0.0)
        qks[d, pl.ds(r0, C), 0:C] = jnp.where(m_incl, aq[C:] * decay, 0.0)
        X = eye - A
        P = _dot(A, A, precision=HI)
        for _ in range(4):
            xp = _dot(jnp.concatenate([X, P], axis=0), P, precision=HI)
            X = X + xp[:C]
            P = xp[C:]
        X = X + _dot(X, P, precision=HI)
        sol = _dot(X, jnp.concatenate([vc * beta, kb * jnp.exp(gc)], axis=1), precision=HI)
        us[d, pl.ds(r0, C), :] = sol[:, :GDN_DK]
        ws[d, pl.ds(r0, C), :] = sol[:, GDN_DK:]
        qes[d, pl.ds(r0, C), :] = qc * jnp.exp(gc)
        g_last = gcB[C - 1:C, :] if d == 0 else gcB[0:1, :]
        kes[d, pl.ds(r0, C), :] = kc * jnp.exp(g_last - gcB)
        egs[d, pl.ds(pl.multiple_of(c * SUBLANES, SUBLANES), SUBLANES), :] = jnp.broadcast_to(
            jnp.exp(g_last), (SUBLANES, LANES))

    def prep_both(c, carry):
        prep(0, c)
        prep(1, c)
        return carry

    lax.fori_loop(0, nch, prep_both, 0, unroll=2)

    def advance(d, c, S):
        r0 = pl.multiple_of(c * C, C)
        wq = _dot(jnp.concatenate([ws[d, pl.ds(r0, C), :], qes[d, pl.ds(r0, C), :]], axis=0), S, precision=HI)
        v_new = us[d, pl.ds(r0, C), :] - wq[:C]
        us[d, pl.ds(r0, C), :] = wq[C:] + _dot(qks[d, pl.ds(r0, C), 0:C], v_new, precision=HI)
        eg = egs[d, pl.ds(pl.multiple_of(c * SUBLANES, SUBLANES), 1), :]
        return S * eg + _dot(kes[d, pl.ds(r0, C), :], v_new, TN, precision=HI)

    def step(i, carry):
        return advance(0, i, carry[0]), advance(1, nch - 1 - i, carry[1])

    if has_s0:
        S0 = (s0_ref[0], s0_ref[1])
    else:
        S0 = (jnp.zeros((GDN_DK, GDN_DK), f32),) * 2
    Sf, Sb = lax.fori_loop(0, nch, step, S0)
    sout_ref[0] = Sf
    sout_ref[1] = Sb

    o = us[0] + us[1]
    y = o * lax.rsqrt(jnp.mean(o * o, -1, keepdims=True) + EPS) * onw_ref[...]
    gt = gate_ref[...]
    out_ref[...] = y * (gt * jax.nn.sigmoid(gt))


def gdn(p, nega, dtb, conv_w, onw, s0, *, row_off, batch, seq, ba_col0):
    rb = row_off // seq
    H = GDN_H
    nch = seq // GDN_CHUNK

    def col(cb):
        return pl.BlockSpec((seq, LANES), lambda b, h: (rb + b, cb + h), pipeline_mode=pl.Buffered(1))

    state_spec = pl.BlockSpec((None, 2, None, GDN_DK, GDN_DK), lambda b, h: (b, 0, h, 0, 0))
    in_specs = [pl.BlockSpec(memory_space=pltpu.SMEM), pl.BlockSpec(memory_space=pltpu.SMEM),
                col(0), col(H), col(2 * H), col(3 * H), col(ba_col0),
                pl.BlockSpec((3, LANES), lambda b, h: (0, h)),
                pl.BlockSpec((3, LANES), lambda b, h: (0, H + h)),
                pl.BlockSpec((3, LANES), lambda b, h: (0, 2 * H + h)),
                pl.BlockSpec((1, LANES), lambda b, h: (0, 0))]
    args = [nega, dtb, p, p, p, p, p, conv_w, conv_w, conv_w, onw.reshape(1, LANES)]
    if s0 is not None:
        in_specs.append(state_spec)
        args.append(s0)
    seq_buf = pltpu.VMEM((seq, LANES), f32)
    dir_buf = pltpu.VMEM((2, seq, LANES), f32)
    return pl.pallas_call(
        functools.partial(_gdn_body, seq=seq, has_s0=s0 is not None),
        grid=(batch, H),
        in_specs=in_specs,
        out_specs=[pl.BlockSpec((seq, LANES), lambda b, h: (b, h)), state_spec],
        out_shape=[jax.ShapeDtypeStruct((batch * seq, H * LANES), f32),
                   jax.ShapeDtypeStruct((batch, 2, H, GDN_DK, GDN_DK), f32)],
        scratch_shapes=[seq_buf, seq_buf, seq_buf, dir_buf, dir_buf, dir_buf, dir_buf, dir_buf,
                        pltpu.VMEM((2, nch * SUBLANES, LANES), f32)],
        compiler_params=_cp("arbitrary", "arbitrary"),
        name="gdn",
    )(*args)


def _fnet_body(x_ref, cd_ref, f_ref, o_ref, z_scr, *, seq):
    r = pl.program_id(1)

    @pl.when(r == 0)
    def _stage1():
        for g in range(FN_W // LANES):
            xg = x_ref[:, g * LANES:(g + 1) * LANES].astype(bf16)
            y = _dot(xg, cd_ref[...])
            z_scr[0:seq, g * LANES:(g + 1) * LANES] = y[:, :LANES].astype(bf16)
            z_scr[seq:2 * seq, g * LANES:(g + 1) * LANES] = y[:, LANES:].astype(bf16)

    scale = 1.0 / math.sqrt(seq * LANES)
    o_ref[...] = _dot(f_ref[...], z_scr[...]) * scale


def fnet(p, cd, fmat, *, row_off, batch, seq, col_block, tr):
    rb = row_off // seq
    return pl.pallas_call(
        functools.partial(_fnet_body, seq=seq),
        grid=(batch, seq // tr),
        in_specs=[pl.BlockSpec((seq, FN_W), lambda b, r: (rb + b, col_block)),
                  pl.BlockSpec((LANES, 2 * LANES), lambda b, r: (0, 0)),
                  pl.BlockSpec((tr, 2 * seq), lambda b, r: (r, 0))],
        out_specs=pl.BlockSpec((tr, FN_W), lambda b, r: (b * (seq // tr) + r, 0)),
        out_shape=jax.ShapeDtypeStruct((batch * seq, FN_W), f32),
        scratch_shapes=[pltpu.VMEM((2 * seq, FN_W), bf16)],
        compiler_params=_cp("arbitrary", "arbitrary"),
        name="fnet",
    )(p, cd, fmat)


def _dft_mats(n):
    k = jnp.arange(n, dtype=i32)
    ang = ((k[:, None] * k[None, :]) % n).astype(f32) * (2.0 * math.pi / n)
    return jnp.cos(ang), jnp.sin(ang)


def _rope(x, cos, sin_signed, first_half):
    partner = jnp.where(first_half, pltpu.roll(x, LANES - MLA_ROPE // 4, 1), pltpu.roll(x, MLA_ROPE // 4, 1))
    return x * cos + partner * sin_signed


def _mla_body(*refs, has_q, normalize, use_rope):
    it = iter(refs)
    if normalize:
        p_ref = next(it)
        qnw_ref = next(it)
        kvnw_ref = next(it)
    else:
        ckv_ref = next(it)
        krp_ref = next(it)
    if has_q:
        wqn_ref, wqr_ref, qhn_ref, qhr_ref = next(it), next(it), next(it), next(it)
    wkn_ref, wv_ref, khn_ref, khr_ref = next(it), next(it), next(it), next(it)
    if use_rope:
        cos_ref, sin_ref = next(it), next(it)
    if has_q:
        qf_ref = next(it)
    kf_ref, v_ref = next(it), next(it)
    if normalize:
        ckv_out, krp_out = next(it), next(it)

    def rms(x):
        return x * lax.rsqrt(jnp.mean(x * x, -1, keepdims=True) + EPS)

    if normalize:
        p = p_ref[...]
        cq = rms(p[:, :MLA_QL]) * qnw_ref[...]
        ckv = rms(p[:, MLA_QL:MLA_QL + MLA_KVL]) * kvnw_ref[...]
        krp = p[:, MLA_QL + MLA_KVL:]
        ckv_out[...] = ckv
        krp_out[...] = krp
    else:
        ckv = ckv_ref[...]
        krp = krp_ref[...]
    if use_rope:
        cos = cos_ref[...]
        sin = sin_ref[...]
        lane = lax.broadcasted_iota(i32, cos.shape, 1)
        first_half = (lane % (MLA_ROPE // 2)) < (MLA_ROPE // 4)

    def head_norm(a, b, wn, wr, scale):
        ss = jnp.sum(a * a, -1, keepdims=True) + jnp.sum(b * b, -1, keepdims=True)
        r = lax.rsqrt(ss * (1.0 / MLA_QK) + EPS)
        a = a * r * wn
        b = b * r * wr
        if use_rope:
            b = _rope(b, cos, sin, first_half)
        return (a * scale).astype(bf16), (b * scale).astype(bf16)

    if has_q:
        cqb = cq.astype(bf16)
        qn = _dot(cqb, wqn_ref[...])
        qr = _dot(cqb, wqr_ref[...])
        for h in range(MLA_H):
            a, b = head_norm(qn[:, h * LANES:(h + 1) * LANES], qr[:, h * LANES:(h + 1) * LANES],
                             qhn_ref[...], qhr_ref[...], MLA_QK ** -0.5)
            qf_ref[:, 2 * h * LANES:(2 * h + 1) * LANES] = a
            qf_ref[:, (2 * h + 1) * LANES:(2 * h + 2) * LANES] = b
    ckvb = ckv.astype(bf16)
    kn = _dot(ckvb, wkn_ref[...])
    v_ref[...] = _dot(ckvb, wv_ref[...]).astype(bf16)
    for h in range(MLA_H):
        a, b = head_norm(kn[:, h * LANES:(h + 1) * LANES], krp, khn_ref[...], khr_ref[...], 1.0)
        kf_ref[:, 2 * h * LANES:(2 * h + 1) * LANES] = a
        kf_ref[:, (2 * h + 1) * LANES:(2 * h + 2) * LANES] = b


def mla_prep(srcs, norm_ws, q_ws, kv_ws, rope_tabs, *, row_off, rows, seq, tm=256):
    normalize = norm_ws is not None
    has_q = q_ws is not None
    use_rope = rope_tabs is not None
    ro = row_off // tm
    nt = rows // tm
    hd = MLA_H * 2 * LANES

    def full(a):
        return pl.BlockSpec(a.shape, lambda i: (0,) * a.ndim)

    args, in_specs = [], []
    for s in srcs:
        args.append(s)
        in_specs.append(pl.BlockSpec((tm, s.shape[1]), lambda i: (ro + i, 0)))
    for group in (norm_ws, q_ws, kv_ws):
        if group is not None:
            for a in group:
                args.append(a)
                in_specs.append(full(a))
    if use_rope:
        tps = seq // tm
        for a in rope_tabs:
            args.append(a)
            in_specs.append(pl.BlockSpec((tm, LANES), lambda i: (i % tps, 0)))
    out_shape, out_specs = [], []

    def out(cols, dt):
        out_shape.append(jax.ShapeDtypeStruct((rows, cols), dt))
        out_specs.append(pl.BlockSpec((tm, cols), lambda i: (i, 0)))

    if has_q:
        out(hd, bf16)
    out(hd, bf16)
    out(MLA_H * LANES, bf16)
    if normalize:
        out(MLA_KVL, f32)
        out(LANES, f32)
    return pl.pallas_call(
        functools.partial(_mla_body, has_q=has_q, normalize=normalize, use_rope=use_rope),
        grid=(nt,),
        in_specs=in_specs,
        out_specs=out_specs,
        out_shape=out_shape,
        compiler_params=_cp("arbitrary"),
        name="mla_prep",
    )(*args)


def _attn_body(*refs, n_kv):
    q_ref = refs[0]
    k_refs = refs[1:1 + n_kv]
    v_refs = refs[1 + n_kv:1 + 2 * n_kv]
    o_ref = refs[1 + 2 * n_kv]
    q = q_ref[...]
    ss = [_dot(q, k_ref[...], NT) for k_ref in k_refs]
    m = jnp.max(ss[0], -1, keepdims=True)
    for s in ss[1:]:
        m = jnp.maximum(m, jnp.max(s, -1, keepdims=True))
    l = None
    acc = None
    for s, v_ref in zip(ss, v_refs):
        e = jnp.exp(s - m)
        ls = jnp.sum(e, -1, keepdims=True)
        l = ls if l is None else l + ls
        pv = _dot(e.astype(bf16), v_ref[...])
        acc = pv if acc is None else acc + pv
    o_ref[...] = acc / l


def attend(qf, kfs, vs, *, batch, lq, lks, tq=256):
    n_kv = len(kfs)
    nq = lq // tq
    in_specs = [pl.BlockSpec((tq, 2 * LANES), lambda b, h, i: (b * nq + i, h))]
    in_specs += [pl.BlockSpec((lk, 2 * LANES), lambda b, h, i: (b, h)) for lk in lks]
    in_specs += [pl.BlockSpec((lk, LANES), lambda b, h, i: (b, h)) for lk in lks]
    return pl.pallas_call(
        functools.partial(_attn_body, n_kv=n_kv),
        grid=(batch, MLA_H, nq),
        in_specs=in_specs,
        out_specs=pl.BlockSpec((tq, LANES), lambda b, h, i: (b * nq + i, h)),
        out_shape=jax.ShapeDtypeStruct((batch * lq, MLA_H * LANES), f32),
        compiler_params=_cp("arbitrary", "arbitrary", "arbitrary"),
        name="attend",
    )(qf, *kfs, *vs)


def _topk_rows(s, k, val_scr, idx_scr, payload=None):
    rows = s.shape[0]
    iota = lax.broadcasted_iota(i32, s.shape, 0)
    for r in range(k):
        m = jnp.max(s, axis=0, keepdims=True)
        idx = jnp.min(jnp.where(s == m, iota, rows), axis=0, keepdims=True)
        hit = iota == idx
        val_scr[r:r + 1, :] = m
        if payload is None:
            idx_scr[r:r + 1, :] = idx
        else:
            idx_scr[r:r + 1, :] = jnp.max(jnp.where(hit, payload, -1), axis=0, keepdims=True)
        s = jnp.where(hit, -jnp.inf, s)


def _pair_candidates(sv0, sv1, si0, si1):
    K = PEER_K
    sub = lax.broadcasted_iota(i32, (SUBLANES, sv0.shape[1]), 0)
    vals, idxs = [], []
    for a in range(SUBLANES):
        nb = K // (a + 1)
        for b0 in range(0, nb, SUBLANES):
            v = sv0[a:a + 1, :] + sv1[b0:b0 + SUBLANES, :]
            e = si0[a:a + 1, :] * PEER_NK + si1[b0:b0 + SUBLANES, :]
            if nb - b0 < SUBLANES:
                v = jnp.where(sub < nb - b0, v, -jnp.inf)
            vals.append(v)
            idxs.append(e)
    vals.append(sv0[SUBLANES:K, :] + sv1[0:1, :])
    idxs.append(si0[SUBLANES:K, :] * PEER_NK + si1[0:1, :])
    return jnp.concatenate(vals, axis=0), jnp.concatenate(idxs, axis=0)


def _peer_topk_body(q_ref, keys_ref, eidx_ref, gate_ref, sv, si, cv, ci, e_all, g_all):
    K = PEER_K
    for h in range(PEER_H):
        for p in range(2):
            qhp = q_ref[:, (2 * h + p) * LANES:(2 * h + p + 1) * LANES]
            s = _dot(keys_ref[h, p], qhp, NT, precision=HI)
            _topk_rows(s, K, sv.at[p], si.at[p])
        cand, cidx = _pair_candidates(sv[0], sv[1], si[0], si[1])
        _topk_rows(cand, K, cv, ci, payload=cidx)
        cs = cv[...]
        e = jnp.exp(cs - cs[0:1, :])
        g_all[h * K:(h + 1) * K, :] = e / jnp.sum(e, axis=0, keepdims=True)
        e_all[h * K:(h + 1) * K, :] = ci[...]
    eidx_ref[...] = e_all[...].T
    gate_ref[...] = g_all[...].T


def peer_topk(q, sub_keys, *, tt=256):
    n = q.shape[0]
    K = PEER_K
    return pl.pallas_call(
        _peer_topk_body,
        grid=(n // tt,),
        in_specs=[pl.BlockSpec((tt, q.shape[1]), lambda i: (i, 0)),
                  pl.BlockSpec(sub_keys.shape, lambda i: (0, 0, 0, 0))],
        out_specs=[pl.BlockSpec((tt, PEER_SEL), lambda i: (i, 0)),
                   pl.BlockSpec((tt, PEER_SEL), lambda i: (i, 0))],
        out_shape=[jax.ShapeDtypeStruct((n, PEER_SEL), i32), jax.ShapeDtypeStruct((n, PEER_SEL), f32)],
        scratch_shapes=[pltpu.VMEM((2, K, tt), f32), pltpu.VMEM((2, K, tt), i32),
                        pltpu.VMEM((K, tt), f32), pltpu.VMEM((K, tt), i32),
                        pltpu.VMEM((PEER_SEL, tt), i32), pltpu.VMEM((PEER_SEL, tt), f32)],
        compiler_params=_cp("arbitrary"),
        name="peer_topk",
    )(q, sub_keys)


HALF = SUBLANES // 2
HI_MASK = -65536


def pack_expert_table(tab):
    e, d = tab.shape
    bits = lax.bitcast_convert_type(tab.astype(bf16), jnp.uint16).astype(jnp.uint32).reshape(e, 2, HALF, LANES)
    words = lax.bitcast_convert_type(bits[:, 0] | (bits[:, 1] << 16), i32)
    return jnp.pad(words, ((1, 1), (0, 0), (0, 0))).reshape((e + 2) * HALF, LANES)


def slot_major(a, nslot):
    n = a.shape[0]
    return a.reshape(n, PEER_SEL // nslot, nslot).transpose(2, 0, 1).reshape(nslot, -1)


def slot_codes(eidx, nslot):
    c = (eidx + 1) * HALF - jnp.where(jnp.arange(PEER_SEL) % 2 == 1, HALF, 0)
    return slot_major(c, nslot)


def _merged(tab_ref, ca, cb, lo_half):
    a = tab_ref[pl.ds(pl.multiple_of(ca, HALF), SUBLANES), :]
    b = tab_ref[pl.ds(pl.multiple_of(cb, HALF), SUBLANES), :]
    return jnp.where(lo_half, a, b)


def _lo(words):
    return lax.bitcast_convert_type(words << 16, f32)


def _hi(words):
    return lax.bitcast_convert_type(words & jnp.int32(HI_MASK), f32)


def _gelu_tanh(x):
    return 0.5 * x * (1.0 + jnp.tanh(math.sqrt(2.0 / math.pi) * (x + 0.044715 * (x * x * x))))


U_SLOTS = 16
U_GROUPS = PEER_SEL // U_SLOTS
U_FIN_UNROLL = 4


def _fold4(ms, sub):
    lo2 = (sub % 4) < 2
    b = [jnp.where(lo2, ms[k] + pltpu.roll(ms[k], SUBLANES - 2, 0), ms[k + 2] + pltpu.roll(ms[k + 2], 2, 0))
         for k in range(2)]
    lo1 = (sub % 2) < 1
    return jnp.where(lo1, b[0] + pltpu.roll(b[0], SUBLANES - 1, 0), b[1] + pltpu.roll(b[1], 1, 0))


def u_phase_perm():
    perm = []
    for g in range(U_GROUPS):
        for fold in range(2):
            for r in range(SUBLANES):
                perm.append(g * U_SLOTS + fold * SUBLANES + 2 * (r % 4) + r // 4)
    return perm


def _peer_u_body(*refs, tb):
    c_refs = refs[:U_SLOTS]
    gate_ref, h_ref, tab_ref, w_ref, z_ref = refs[U_SLOTS:]
    sub = lax.broadcasted_iota(i32, (SUBLANES, LANES), 0)
    lo_half = sub < HALF

    def token(t, carry):
        r8 = pl.multiple_of(t * SUBLANES, SUBLANES)
        hrow = h_ref[pl.ds(r8, SUBLANES), :]
        hswap = pltpu.roll(hrow, HALF, 0)
        hl = jnp.where(lo_half, hrow, hswap)
        hh = jnp.where(lo_half, hswap, hrow)
        for g in range(U_GROUPS):
            for fold in range(2):
                ms = []
                for k in range(4):
                    s0 = fold * SUBLANES + 2 * k
                    words = _merged(tab_ref, c_refs[s0][t * U_GROUPS + g], c_refs[s0 + 1][t * U_GROUPS + g], lo_half)
                    ms.append(_lo(words) * hl + _hi(words) * hh)
                row0 = t * PEER_SEL + (2 * g + fold) * SUBLANES
                z_ref[pl.ds(pl.multiple_of(row0, SUBLANES), SUBLANES), :] = _fold4(ms, sub)
        return carry

    lax.fori_loop(0, tb, token, 0)

    def finish(tg, carry):
        for i in range(U_FIN_UNROLL):
            t = tg * U_FIN_UNROLL + i
            zt = z_ref[pl.ds(pl.multiple_of(t * PEER_SEL, PEER_SEL), PEER_SEL), :]
            w_ref[pl.ds(t, 1), :] = jnp.sum(zt.T, axis=0, keepdims=True)
        return carry

    lax.fori_loop(0, tb // U_FIN_UNROLL, finish, 0)
    w_ref[...] = gate_ref[...] * _gelu_tanh(w_ref[...])


def peer_u_gather(codes, gate_p, h8, tab, *, tb=128):
    n = gate_p.shape[0]
    smem = [pl.BlockSpec((tb * U_GROUPS,), lambda i: (i,), memory_space=pltpu.SMEM) for _ in range(U_SLOTS)]
    return pl.pallas_call(
        functools.partial(_peer_u_body, tb=tb),
        grid=(n // tb,),
        in_specs=smem + [pl.BlockSpec((tb, PEER_SEL), lambda i: (i, 0)),
                         pl.BlockSpec((tb * SUBLANES, LANES), lambda i: (i, 0)),
                         pl.BlockSpec(tab.shape, lambda i: (0, 0), pipeline_mode=pl.Buffered(1))],
        out_specs=pl.BlockSpec((tb, PEER_SEL), lambda i: (i, 0)),
        out_shape=jax.ShapeDtypeStruct((n, PEER_SEL), f32),
        scratch_shapes=[pltpu.VMEM((tb * PEER_SEL, LANES), f32)],
        compiler_params=_cp("arbitrary"),
        name="peer_u",
    )(*[codes[s] for s in range(U_SLOTS)], gate_p, h8, tab)


V_SLOTS = 4
V_GROUPS = PEER_SEL // V_SLOTS
V_UNROLL = 8


def _peer_v_body(*refs, tb):
    c_refs = refs[:V_SLOTS]
    w_refs = refs[V_SLOTS:2 * V_SLOTS]
    x_ref, g_ref, tab_ref, o_ref = refs[2 * V_SLOTS:]
    sub = lax.broadcasted_iota(i32, (SUBLANES, LANES), 0)
    lo_half = sub < HALF
    g8 = g_ref[...]

    def token(t, carry):
        r8 = pl.multiple_of(t * SUBLANES, SUBLANES)

        def group(g, accs):
            lo, hi = accs
            i = t * V_GROUPS + g
            for k in range(V_SLOTS // 2):
                words = _merged(tab_ref, c_refs[2 * k][i], c_refs[2 * k + 1][i], lo_half)
                wm = jnp.where(lo_half, w_refs[2 * k][i], w_refs[2 * k + 1][i])
                lo = lo + wm * _lo(words)
                hi = hi + wm * _hi(words)
            return lo, hi

        z = jnp.zeros((SUBLANES, LANES), f32)
        lo, hi = lax.fori_loop(0, V_GROUPS, group, (z, z), unroll=V_UNROLL)
        lo = lo + pltpu.roll(lo, HALF, 0)
        hi = hi + pltpu.roll(hi, HALF, 0)
        o_ref[pl.ds(r8, SUBLANES), :] = x_ref[pl.ds(r8, SUBLANES), :] + g8 * jnp.where(lo_half, lo, hi)
        return carry

    lax.fori_loop(0, tb, token, 0)


def peer_v_gather(codes, ws, x8, g8_all, tab, *, prompt_rows, seq_len, tb=128):
    n = x8.shape[0] // SUBLANES
    pt, tps = prompt_rows // tb, seq_len // tb
    smem = [pl.BlockSpec((tb * V_GROUPS,), lambda i: (i,), memory_space=pltpu.SMEM) for _ in range(2 * V_SLOTS)]
    return pl.pallas_call(
        functools.partial(_peer_v_body, tb=tb),
        grid=(n // tb,),
        in_specs=smem + [pl.BlockSpec((tb * SUBLANES, LANES), lambda i: (i, 0)),
                         pl.BlockSpec((None, SUBLANES, LANES), lambda i: (_mod_row(i, pt, tps), 0, 0)),
                         pl.BlockSpec(tab.shape, lambda i: (0, 0), pipeline_mode=pl.Buffered(1))],
        out_specs=pl.BlockSpec((tb * SUBLANES, LANES), lambda i: (i, 0)),
        out_shape=jax.ShapeDtypeStruct((n * SUBLANES, LANES), f32),
        compiler_params=_cp("arbitrary"),
        name="peer_v",
    )(*[codes[s] for s in range(V_SLOTS)], *[ws[s] for s in range(V_SLOTS)], x8, g8_all, tab)


def _rope_tables(seq):
    rows = seq // GRID_W
    row = jnp.repeat(jnp.arange(rows, dtype=f32), GRID_W)
    col = jnp.tile(jnp.arange(GRID_W, dtype=f32), rows)
    nfreq = MLA_ROPE // 4
    inv = jnp.power(ROPE_BASE, -jnp.arange(nfreq, dtype=f32) / nfreq)
    ar = row[:, None] * inv
    ac = col[:, None] * inv
    pad1 = jnp.ones((seq, LANES - MLA_ROPE), f32)
    pad0 = jnp.zeros((seq, LANES - MLA_ROPE), f32)
    cos = jnp.concatenate([jnp.cos(ar), jnp.cos(ar), jnp.cos(ac), jnp.cos(ac), pad1], -1)
    sin = jnp.concatenate([-jnp.sin(ar), jnp.sin(ar), -jnp.sin(ac), jnp.sin(ac), pad0], -1)
    return cos, sin


def _pad_heads(w, head_w, lo, hi):
    k = w.shape[0]
    w = w.reshape(k, -1, head_w)[:, :, lo:hi]
    return jnp.pad(w, ((0, 0), (0, 0), (0, LANES - (hi - lo)))).reshape(k, -1)


def kernel(x_prompt, x_sample, state_gdn, cache_mla_ckv, cache_mla_krope, c, c_ctx, w_mod, b_mod, norm_mix, norm_ffn, even_w_in, even_conv_w, gdn_a_log, gdn_dt_bias, gdn_o_norm, even_w_out, odd_w_in, mla_q_norm, mla_kv_norm, mla_w_uq, mla_w_ukv, mla_q_headnorm, mla_k_headnorm, odd_w_out, peer_w_q, peer_sub_keys, peer_u, peer_v):
    B, L, _ = x_prompt.shape
    BS, LS, _ = x_sample.shape
    depth = w_mod.shape[0]
    NP, NS = B * L, BS * LS
    past = cache_mla_ckv.shape[2]
    geo = dict(prompt_rows=NP, seq_len=LS)

    x = jnp.concatenate([x_prompt.reshape(NP, D), x_sample.reshape(NS, D)], 0)
    nrow = 1 + BS
    rpad = -nrow % SUBLANES
    cv = jnp.concatenate([c_ctx[None, :], c, jnp.zeros((rpad, D), f32)], 0)
    mods = ada_mods_all(cv, w_mod, b_mod).reshape(depth, nrow + rpad, 6, D)

    cd_c, cd_s = _dft_mats(LANES)
    cd = jnp.concatenate([cd_c, cd_s], 1).astype(bf16)
    fmats = {}
    for n in (L, LS):
        cl, sl = _dft_mats(n)
        fmats[n] = jnp.concatenate([cl, -sl], 1).astype(bf16)
    cos_t, sin_t = _rope_tables(LS)

    new_gdn, new_ckv, new_kr = [], [], []
    for l in range(depth):
        j = l // 2
        m_l = mods[l]
        if l % 2 == 0:
            wi = even_w_in[j]
            o0 = 3 * A_QK + A_QK
            ba = wi[:, o0:o0 + 4 * GDN_H].reshape(D, 4, GDN_H).transpose(0, 2, 1)
            ba = jnp.pad(ba, ((0, 0), (0, 0), (0, LANES - 4))).reshape(D, GDN_H * LANES)
            w_in = jnp.concatenate([wi[:, :o0], wi[:, o0 + 4 * GDN_H:], ba], 1).astype(bf16)
            p = normmm(x, m_l, 0, norm_mix[l], w_in, **geo)
            nega = -jnp.exp(gdn_a_log[j])
            kw = dict(ba_col0=(o0 + FN_W) // LANES)
            mix_p, st = gdn(p, nega, gdn_dt_bias[j], even_conv_w[j], gdn_o_norm[j], None,
                            row_off=0, batch=B, seq=L, **kw)
            mix_s, _ = gdn(p, nega, gdn_dt_bias[j], even_conv_w[j], gdn_o_norm[j], state_gdn[:, j],
                           row_off=NP, batch=BS, seq=LS, **kw)
            new_gdn.append(st)
            fb_p = fnet(p, cd, fmats[L], row_off=0, batch=B, seq=L, col_block=o0 // FN_W, tr=L)
            fb_s = fnet(p, cd, fmats[LS], row_off=NP, batch=BS, seq=LS, col_block=o0 // FN_W, tr=256)
            wo = even_w_out[j].astype(bf16)
            x = outproj(x, m_l, 2, [jnp.concatenate([mix_p, mix_s], 0), jnp.concatenate([fb_p, fb_s], 0)],
                        [wo[:A_QK], wo[A_QK:]], **geo)
        else:
            wi = odd_w_in[j]
            w_in = jnp.pad(wi, ((0, 0), (0, LANES - MLA_ROPE))).astype(bf16)
            p = normmm(x, m_l, 0, norm_mix[l], w_in, **geo)
            norm_ws = (mla_q_norm[j].reshape(1, -1), mla_kv_norm[j].reshape(1, -1))
            qh, kh = mla_q_headnorm[j], mla_k_headnorm[j]

            def split_hw(hw):
                return (hw[:MLA_NOPE].reshape(1, LANES),
                        jnp.pad(hw[MLA_NOPE:], (0, LANES - MLA_ROPE)).reshape(1, LANES))

            q_ws = (_pad_heads(mla_w_uq[j], MLA_QK, 0, MLA_NOPE).astype(bf16),
                    _pad_heads(mla_w_uq[j], MLA_QK, MLA_NOPE, MLA_QK).astype(bf16)) + split_hw(qh)
            kv_ws = (_pad_heads(mla_w_ukv[j], 2 * LANES, 0, LANES).astype(bf16),
                     _pad_heads(mla_w_ukv[j], 2 * LANES, LANES, 2 * LANES).astype(bf16)) + split_hw(kh)
            qf_p, kf_p, v_p, ckv_p, krp_p = mla_prep((p,), norm_ws, q_ws, kv_ws, None, row_off=0, rows=NP, seq=L)
            qf_s, kf_s, v_s, _, _ = mla_prep((p,), norm_ws, q_ws, kv_ws, (cos_t, sin_t), row_off=NP, rows=NS, seq=LS)
            ckv_c = cache_mla_ckv[:, j].reshape(BS * past, MLA_KVL)
            krp_c = jnp.pad(cache_mla_krope[:, j].reshape(BS * past, MLA_ROPE), ((0, 0), (0, LANES - MLA_ROPE)))
            kf_c, v_c = mla_prep((ckv_c, krp_c), None, None, kv_ws, None, row_off=0, rows=BS * past, seq=past)
            o_p = attend(qf_p, [kf_p], [v_p], batch=B, lq=L, lks=[L])
            o_s = attend(qf_s, [kf_s, kf_c], [v_s, v_c], batch=BS, lq=LS, lks=[LS, past])
            new_ckv.append(ckv_p.reshape(B, L, MLA_KVL))
            new_kr.append(krp_p[:, :MLA_ROPE].reshape(B, L, MLA_ROPE))
            x = outproj(x, m_l, 2, [jnp.concatenate([o_p, o_s], 0)], [odd_w_out[j].astype(bf16)], **geo)
        q, h = normmm(x, m_l, 3, norm_ffn[l], peer_w_q[l], precise=True, with_h=True, **geo)
        eidx, gate = peer_topk(q, peer_sub_keys[l])
        perm = jnp.array(u_phase_perm(), i32)
        w_p = peer_u_gather(slot_codes(eidx, U_SLOTS), gate[:, perm], h.reshape(-1, LANES), pack_expert_table(peer_u[l]))
        codes_v = slot_codes(eidx[:, perm], V_SLOTS)
        g8 = m_l[:, 5].reshape(-1, SUBLANES, LANES)
        x = peer_v_gather(codes_v, slot_major(w_p, V_SLOTS), x.reshape(-1, LANES), g8, pack_expert_table(peer_v[l]),
                          **geo).reshape(-1, D)

    y_prompt = x[:NP].reshape(B, L, D)
    y_sample = x[NP:].reshape(BS, LS, D)
    return (y_prompt, y_sample, jnp.stack(new_gdn, 1), jnp.stack(new_ckv, 1), jnp.stack(new_kr, 1))
```

```python
import functools
import math

import jax
import jax.numpy as jnp
from jax import lax
from jax.experimental import pallas as pl
from jax.experimental.pallas import tpu as pltpu

f32 = jnp.float32
bf16 = jnp.bfloat16
i32 = jnp.int32

LANES = 128
SUBLANES = 8
VMEM_LIMIT = 56 * 1024 * 1024

EPS = 1e-6
D = 1024
GRID_W = 64
GDN_H = 4
GDN_DK = 128
GDN_CHUNK = 64
A_QK = GDN_H * GDN_DK
FN_W = 512
MLA_H = 8
MLA_QL = 512
MLA_KVL = 256
MLA_NOPE = 128
MLA_ROPE = 64
MLA_QK = MLA_NOPE + MLA_ROPE
ROPE_BASE = 10000.0
PEER_H = 8
PEER_NK = 128
PEER_K = 16
PEER_SEL = PEER_H * PEER_K

HI = lax.Precision.HIGHEST
NT = (((1,), (1,)), ((), ()))
TN = (((0,), (0,)), ((), ()))
NN = (((1,), (0,)), ((), ()))


def _dot(a, b, dims=NN, precision=None):
    return lax.dot_general(a, b, dims, preferred_element_type=f32, precision=precision)


def _cp(*sem):
    return pltpu.CompilerParams(dimension_semantics=sem, vmem_limit_bytes=VMEM_LIMIT)


def _mod_row(i, prompt_tiles, tiles_per_seq):
    return jnp.where(i < prompt_tiles, 0, 1 + (i - prompt_tiles) // tiles_per_seq)


def _mods_body(c_ref, w_ref, b_ref, o_ref):
    c = c_ref[...]
    o_ref[...] = _dot(c * jax.nn.sigmoid(c), w_ref[...], precision=HI) + b_ref[...]


def ada_mods_all(cv, w_mod, b_mod):
    depth, _, n6 = w_mod.shape
    r = cv.shape[0]
    tn = 1536
    return pl.pallas_call(
        _mods_body,
        grid=(depth, n6 // tn),
        in_specs=[pl.BlockSpec((r, D), lambda l, j: (0, 0)),
                  pl.BlockSpec((None, D, tn), lambda l, j: (l, 0, j)),
                  pl.BlockSpec((None, 1, tn), lambda l, j: (l, 0, j))],
        out_specs=pl.BlockSpec((None, r, tn), lambda l, j: (l, 0, j)),
        out_shape=jax.ShapeDtypeStruct((depth, r, n6), f32),
        compiler_params=_cp("arbitrary", "arbitrary"),
        name="ada_mods",
    )(cv, w_mod, b_mod.reshape(depth, 1, n6))


def _normmm_body(x_ref, m_ref, nw_ref, w_ref, *rest, shift_idx, split):
    if split:
        wlo_ref, rest = rest[0], rest[1:]
    o_ref, h_refs = rest[0], rest[1:]
    x = x_ref[...]
    y = x * lax.rsqrt(jnp.mean(x * x, -1, keepdims=True) + EPS) * nw_ref[...]
    h = y * (1.0 + m_ref[shift_idx + 1:shift_idx + 2, :]) + m_ref[shift_idx:shift_idx + 1, :]
    hb = h.astype(bf16)
    acc = _dot(hb, w_ref[...])
    if split:
        acc = acc + _dot((h - hb.astype(f32)).astype(bf16), w_ref[...]) + _dot(hb, wlo_ref[...])
    o_ref[...] = acc
    if h_refs:
        h_refs[0][...] = h


def normmm(x, mods_l, shift_idx, norm_w, w, *, prompt_rows, seq_len, tm=256, with_h=False):
    n = x.shape[0]
    nout = w.shape[1]
    split = w.dtype == f32
    ws = [w]
    if split:
        w_hi = w.astype(bf16)
        ws = [w_hi, (w - w_hi.astype(f32)).astype(bf16)]
    pt, tps = prompt_rows // tm, seq_len // tm
    out_shape = [jax.ShapeDtypeStruct((n, nout), f32)]
    out_specs = [pl.BlockSpec((tm, nout), lambda i: (i, 0))]
    if with_h:
        out_shape.append(jax.ShapeDtypeStruct((n, D), f32))
        out_specs.append(pl.BlockSpec((tm, D), lambda i: (i, 0)))
    res = pl.pallas_call(
        functools.partial(_normmm_body, shift_idx=shift_idx, split=split),
        grid=(n // tm,),
        in_specs=[pl.BlockSpec((tm, D), lambda i: (i, 0)),
                  pl.BlockSpec((None, 6, D), lambda i: (_mod_row(i, pt, tps), 0, 0)),
                  pl.BlockSpec((1, D), lambda i: (0, 0))] + [pl.BlockSpec((D, nout), lambda i: (0, 0)) for _ in ws],
        out_specs=out_specs,
        out_shape=out_shape,
        compiler_params=_cp("arbitrary"),
        name="normmm",
    )(x, mods_l, norm_w.reshape(1, D), *ws)
    return res if with_h else res[0]


def _outproj_body(*refs, n_in, gate_idx):
    x_ref, m_ref = refs[0], refs[1]
    a_refs = refs[2:2 + n_in]
    w_refs = refs[2 + n_in:2 + 2 * n_in]
    o_ref = refs[2 + 2 * n_in]
    y = _dot(a_refs[0][...].astype(bf16), w_refs[0][...])
    for a_ref, w_ref in zip(a_refs[1:], w_refs[1:]):
        y = y + _dot(a_ref[...].astype(bf16), w_ref[...])
    o_ref[...] = x_ref[...] + m_ref[gate_idx:gate_idx + 1, :] * y


def outproj(x, mods_l, gate_idx, acts, ws, *, prompt_rows, seq_len, tm=256):
    n = x.shape[0]
    pt, tps = prompt_rows // tm, seq_len // tm
    in_specs = [pl.BlockSpec((tm, D), lambda i: (i, 0)),
                pl.BlockSpec((None, 6, D), lambda i: (_mod_row(i, pt, tps), 0, 0))]
    in_specs += [pl.BlockSpec((tm, a.shape[1]), lambda i: (i, 0)) for a in acts]
    in_specs += [pl.BlockSpec(w.shape, lambda i: (0, 0)) for w in ws]
    return pl.pallas_call(
        functools.partial(_outproj_body, n_in=len(acts), gate_idx=gate_idx),
        grid=(n // tm,),
        in_specs=in_specs,
        out_specs=pl.BlockSpec((tm, D), lambda i: (i, 0)),
        out_shape=jax.ShapeDtypeStruct((n, D), f32),
        compiler_params=_cp("arbitrary"),
        name="outproj",
    )(x, mods_l, *acts, *ws)


def _softplus(x):
    return jnp.maximum(x, 0.0) + jnp.log1p(jnp.exp(-jnp.abs(x)))


def _gdn_body(nega_ref, dt_ref, q_ref, k_ref, v_ref, gate_ref, ba_ref, cwq_ref, cwk_ref, cwv_ref, onw_ref,
              *rest, seq, has_s0):
    if has_s0:
        s0_ref, rest = rest[0], rest[1:]
    out_ref, sout_ref, qs, ks, vs, us, ws, qes, kes, qks, egs = rest
    h = pl.program_id(1)
    C = GDN_CHUNK
    nch = seq // C

    row = lax.broadcasted_iota(i32, (seq, LANES), 0)
    first = row == 0
    last = row == seq - 1

    def conv_silu(x_ref, w_ref):
        x = x_ref[...]
        w = w_ref[...]
        prev = jnp.where(first, 0.0, pltpu.roll(x, 1, 0))
        nxt = jnp.where(last, 0.0, pltpu.roll(x, seq - 1, 0))
        y = prev * w[0:1, :] + x * w[1:2, :] + nxt * w[2:3, :]
        return y * jax.nn.sigmoid(y)

    def l2n(x):
        return x * lax.rsqrt(jnp.sum(x * x, -1, keepdims=True) + EPS)

    qs[...] = l2n(conv_silu(q_ref, cwq_ref)) * (GDN_DK ** -0.5)
    ks[...] = l2n(conv_silu(k_ref, cwk_ref))
    vs[...] = conv_silu(v_ref, cwv_ref)

    ii = lax.broadcasted_iota(i32, (C, C), 0)
    jj = lax.broadcasted_iota(i32, (C, C), 1)
    eye = (ii == jj).astype(f32)

    ones_cc = jnp.ones((C, C), f32)
    zeros_cc = jnp.zeros((C, C), f32)
    NB = 4
    same_block = (lax.broadcasted_iota(i32, (NB * C, NB * C), 0) // C
                  == lax.broadcasted_iota(i32, (NB * C, NB * C), 1) // C)

    def block_diag(m_cat):
        return jnp.where(same_block, jnp.concatenate([m_cat] * NB, axis=0), 0.0)

    def prep_pair(cp, carry):
        a_blocks, rhs_blocks, dest = [], [], []
        for cc in range(2):
            r0 = pl.multiple_of((2 * cp + cc) * C, C)
            qc = qs[pl.ds(r0, C), :]
            kc = ks[pl.ds(r0, C), :]
            vc = vs[pl.ds(r0, C), :]
            ba = ba_ref[pl.ds(r0, C), :]
            betas = [jax.nn.sigmoid(ba[:, d:d + 1]) for d in range(2)]
            kbs = [kc * b for b in betas]
            aq = _dot(jnp.concatenate(kbs + [qc], axis=0).astype(bf16), kc.astype(bf16), NT)
            for d in range(2):
                rel = (ii - jj) if d == 0 else (jj - ii)
                m_incl = rel >= 0
                tri = m_incl.astype(f32)
                tri_t = (rel <= 0).astype(f32)
                g = nega_ref[d, h] * _softplus(ba[:, 2 + d:3 + d] + dt_ref[d, h])
                g1 = jnp.broadcast_to(g, (C, C))
                lhs = jnp.concatenate([tri, -ones_cc], axis=1)
                rhs = jnp.concatenate([jnp.concatenate([g1, g1, g1], axis=1),
                                       jnp.concatenate([zeros_cc, zeros_cc, g1 * tri_t], axis=1)], axis=0)
                R = _dot(lhs, rhs, precision=HI)
                gcB = R[:, :LANES]
                gc = gcB[:, 0:1]
                decay = jnp.exp(jnp.where(m_incl, R[:, LANES:LANES + C], -jnp.inf))
                a_blocks.append(jnp.where(rel > 0, aq[d * C:(d + 1) * C] * decay, 0.0))
                qks[d, pl.ds(r0, C), 0:C] = jnp.where(m_incl, aq[2 * C:] * decay, 0.0)
                rhs_blocks.append(jnp.concatenate([vc * betas[d], kbs[d] * jnp.exp(gc)], axis=1))
                dest.append((d, r0))
                qes[d, pl.ds(r0, C), :] = qc * jnp.exp(gc)
                g_last = gcB[C - 1:C, :] if d == 0 else gcB[0:1, :]
                kes[d, pl.ds(r0, C), :] = kc * jnp.exp(g_last - gcB)
                egs[d, pl.ds(pl.multiple_of((2 * cp + cc) * SUBLANES, SUBLANES), SUBLANES), :] = jnp.broadcast_to(
                    jnp.exp(g_last), (SUBLANES, LANES))
        A = jnp.concatenate(a_blocks, axis=1)
        X = jnp.concatenate([eye] * NB, axis=1) - A
        P = _dot(A, block_diag(A), precision=HI)
        for _ in range(4):
            xp = _dot(jnp.concatenate([X, P], axis=0), block_diag(P), precision=HI)
            X = X + xp[:C]
            P = xp[C:]
        X = X + _dot(X, block_diag(P), precision=HI)
        zero_rhs = jnp.zeros((C, 2 * GDN_DK), f32)
        rhs_rows = [jnp.concatenate([zero_rhs] * k + [rhs_blocks[k]] + [zero_rhs] * (NB - 1 - k), axis=1)
                    for k in range(NB)]
        sol = _dot(X, jnp.concatenate(rhs_rows, axis=0), precision=HI)
        for k, (d, r0) in enumerate(dest):
            us[d, pl.ds(r0, C), :] = sol[:, 2 * k * GDN_DK:(2 * k + 1) * GDN_DK]
            ws[d, pl.ds(r0, C), :] = sol[:, (2 * k + 1) * GDN_DK:(2 * k + 2) * GDN_DK]
        return carry

    lax.fori_loop(0, nch // 2, prep_pair, 0)

    def advance(d, c, S):
        r0 = pl.multiple_of(c * C, C)
        wq = _dot(jnp.concatenate([ws[d, pl.ds(r0, C), :], qes[d, pl.ds(r0, C), :]], axis=0).astype(bf16),
                  S.astype(bf16))
        v_new = us[d, pl.ds(r0, C), :] - wq[:C]
        vb = v_new.astype(bf16)
        us[d, pl.ds(r0, C), :] = wq[C:] + _dot(qks[d, pl.ds(r0, C), 0:C].astype(bf16), vb)
        eg = egs[d, pl.ds(pl.multiple_of(c * SUBLANES, SUBLANES), 1), :]
        return S * eg + _dot(kes[d, pl.ds(r0, C), :].astype(bf16), vb, TN)

    def step(i, carry):
        return advance(0, i, carry[0]), advance(1, nch - 1 - i, carry[1])

    if has_s0:
        S0 = (s0_ref[0], s0_ref[1])
    else:
        S0 = (jnp.zeros((GDN_DK, GDN_DK), f32),) * 2
    Sf, Sb = lax.fori_loop(0, nch, step, S0)
    sout_ref[0] = Sf
    sout_ref[1] = Sb

    o = us[0] + us[1]
    y = o * lax.rsqrt(jnp.mean(o * o, -1, keepdims=True) + EPS) * onw_ref[...]
    gt = gate_ref[...]
    out_ref[...] = y * (gt * jax.nn.sigmoid(gt))


def gdn(p, nega, dtb, conv_w, onw, s0, *, row_off, batch, seq, ba_col0):
    rb = row_off // seq
    H = GDN_H
    nch = seq // GDN_CHUNK

    def col(cb):
        return pl.BlockSpec((seq, LANES), lambda b, h: (rb + b, cb + h), pipeline_mode=pl.Buffered(1))

    state_spec = pl.BlockSpec((None, 2, None, GDN_DK, GDN_DK), lambda b, h: (b, 0, h, 0, 0))
    in_specs = [pl.BlockSpec(memory_space=pltpu.SMEM), pl.BlockSpec(memory_space=pltpu.SMEM),
                col(0), col(H), col(2 * H), col(3 * H), col(ba_col0),
                pl.BlockSpec((3, LANES), lambda b, h: (0, h)),
                pl.BlockSpec((3, LANES), lambda b, h: (0, H + h)),
                pl.BlockSpec((3, LANES), lambda b, h: (0, 2 * H + h)),
                pl.BlockSpec((1, LANES), lambda b, h: (0, 0))]
    args = [nega, dtb, p, p, p, p, p, conv_w, conv_w, conv_w, onw.reshape(1, LANES)]
    if s0 is not None:
        in_specs.append(state_spec)
        args.append(s0)
    seq_buf = pltpu.VMEM((seq, LANES), f32)
    dir_buf = pltpu.VMEM((2, seq, LANES), f32)
    return pl.pallas_call(
        functools.partial(_gdn_body, seq=seq, has_s0=s0 is not None),
        grid=(batch, H),
        in_specs=in_specs,
        out_specs=[pl.BlockSpec((seq, LANES), lambda b, h: (b, h)), state_spec],
        out_shape=[jax.ShapeDtypeStruct((batch * seq, H * LANES), f32),
                   jax.ShapeDtypeStruct((batch, 2, H, GDN_DK, GDN_DK), f32)],
        scratch_shapes=[seq_buf, seq_buf, seq_buf, dir_buf, dir_buf, dir_buf, dir_buf, dir_buf,
                        pltpu.VMEM((2, nch * SUBLANES, LANES), f32)],
        compiler_params=_cp("arbitrary", "arbitrary"),
        name="gdn",
    )(*args)


def _fnet_body(x_ref, cd_ref, f_ref, o_ref, z_scr, *, seq):
    r = pl.program_id(1)

    @pl.when(r == 0)
    def _stage1():
        for g in range(FN_W // LANES):
            xg = x_ref[:, g * LANES:(g + 1) * LANES].astype(bf16)
            y = _dot(xg, cd_ref[...])
            z_scr[0:seq, g * LANES:(g + 1) * LANES] = y[:, :LANES].astype(bf16)
            z_scr[seq:2 * seq, g * LANES:(g + 1) * LANES] = y[:, LANES:].astype(bf16)

    scale = 1.0 / math.sqrt(seq * LANES)
    o_ref[...] = _dot(f_ref[...], z_scr[...]) * scale


def fnet(p, cd, fmat, *, row_off, batch, seq, col_block, tr):
    rb = row_off // seq
    return pl.pallas_call(
        functools.partial(_fnet_body, seq=seq),
        grid=(batch, seq // tr),
        in_specs=[pl.BlockSpec((seq, FN_W), lambda b, r: (rb + b, col_block)),
                  pl.BlockSpec((LANES, 2 * LANES), lambda b, r: (0, 0)),
                  pl.BlockSpec((tr, 2 * seq), lambda b, r: (r, 0))],
        out_specs=pl.BlockSpec((tr, FN_W), lambda b, r: (b * (seq // tr) + r, 0)),
        out_shape=jax.ShapeDtypeStruct((batch * seq, FN_W), f32),
        scratch_shapes=[pltpu.VMEM((2 * seq, FN_W), bf16)],
        compiler_params=_cp("arbitrary", "arbitrary"),
        name="fnet",
    )(p, cd, fmat)


def _dft_mats(n):
    k = jnp.arange(n, dtype=i32)
    ang = ((k[:, None] * k[None, :]) % n).astype(f32) * (2.0 * math.pi / n)
    return jnp.cos(ang), jnp.sin(ang)


def _rope(x, cos, sin_signed, first_half):
    partner = jnp.where(first_half, pltpu.roll(x, LANES - MLA_ROPE // 4, 1), pltpu.roll(x, MLA_ROPE // 4, 1))
    return x * cos + partner * sin_signed


def _mla_body(*refs, has_q, normalize, use_rope):
    it = iter(refs)
    if normalize:
        p_ref = next(it)
        qnw_ref = next(it)
        kvnw_ref = next(it)
    else:
        ckv_ref = next(it)
        krp_ref = next(it)
    if has_q:
        wqn_ref, wqr_ref, qhn_ref, qhr_ref = next(it), next(it), next(it), next(it)
    wkn_ref, wv_ref, khn_ref, khr_ref = next(it), next(it), next(it), next(it)
    if use_rope:
        cos_ref, sin_ref = next(it), next(it)
    if has_q:
        qf_ref = next(it)
    kf_ref, v_ref = next(it), next(it)
    if normalize:
        ckv_out, krp_out = next(it), next(it)

    def rms(x):
        return x * lax.rsqrt(jnp.mean(x * x, -1, keepdims=True) + EPS)

    if normalize:
        p = p_ref[...]
        cq = rms(p[:, :MLA_QL]) * qnw_ref[...]
        ckv = rms(p[:, MLA_QL:MLA_QL + MLA_KVL]) * kvnw_ref[...]
        krp = p[:, MLA_QL + MLA_KVL:]
        ckv_out[...] = ckv
        krp_out[...] = krp
    else:
        ckv = ckv_ref[...]
        krp = krp_ref[...]
    if use_rope:
        cos = cos_ref[...]
        sin = sin_ref[...]
        lane = lax.broadcasted_iota(i32, cos.shape, 1)
        first_half = (lane % (MLA_ROPE // 2)) < (MLA_ROPE // 4)

    def head_norm(a, b, wn, wr, scale):
        ss = jnp.sum(a * a, -1, keepdims=True) + jnp.sum(b * b, -1, keepdims=True)
        r = lax.rsqrt(ss * (1.0 / MLA_QK) + EPS)
        a = a * r * wn
        b = b * r * wr
        if use_rope:
            b = _rope(b, cos, sin, first_half)
        return (a * scale).astype(bf16), (b * scale).astype(bf16)

    if has_q:
        cqb = cq.astype(bf16)
        qn = _dot(cqb, wqn_ref[...])
        qr = _dot(cqb, wqr_ref[...])
        for h in range(MLA_H):
            a, b = head_norm(qn[:, h * LANES:(h + 1) * LANES], qr[:, h * LANES:(h + 1) * LANES],
                             qhn_ref[...], qhr_ref[...], MLA_QK ** -0.5)
            qf_ref[:, 2 * h * LANES:(2 * h + 1) * LANES] = a
            qf_ref[:, (2 * h + 1) * LANES:(2 * h + 2) * LANES] = b
    ckvb = ckv.astype(bf16)
    kn = _dot(ckvb, wkn_ref[...])
    v_ref[...] = _dot(ckvb, wv_ref[...]).astype(bf16)
    for h in range(MLA_H):
        a, b = head_norm(kn[:, h * LANES:(h + 1) * LANES], krp, khn_ref[...], khr_ref[...], 1.0)
        kf_ref[:, 2 * h * LANES:(2 * h + 1) * LANES] = a
        kf_ref[:, (2 * h + 1) * LANES:(2 * h + 2) * LANES] = b


def mla_prep(srcs, norm_ws, q_ws, kv_ws, rope_tabs, *, row_off, rows, seq, tm=256):
    normalize = norm_ws is not None
    has_q = q_ws is not None
    use_rope = rope_tabs is not None
    ro = row_off // tm
    nt = rows // tm
    hd = MLA_H * 2 * LANES

    def full(a):
        return pl.BlockSpec(a.shape, lambda i: (0,) * a.ndim)

    args, in_specs = [], []
    for s in srcs:
        args.append(s)
        in_specs.append(pl.BlockSpec((tm, s.shape[1]), lambda i: (ro + i, 0)))
    for group in (norm_ws, q_ws, kv_ws):
        if group is not None:
            for a in group:
                args.append(a)
                in_specs.append(full(a))
    if use_rope:
        tps = seq // tm
        for a in rope_tabs:
            args.append(a)
            in_specs.append(pl.BlockSpec((tm, LANES), lambda i: (i % tps, 0)))
    out_shape, out_specs = [], []

    def out(cols, dt):
        out_shape.append(jax.ShapeDtypeStruct((rows, cols), dt))
        out_specs.append(pl.BlockSpec((tm, cols), lambda i: (i, 0)))

    if has_q:
        out(hd, bf16)
    out(hd, bf16)
    out(MLA_H * LANES, bf16)
    if normalize:
        out(MLA_KVL, f32)
        out(LANES, f32)
    return pl.pallas_call(
        functools.partial(_mla_body, has_q=has_q, normalize=normalize, use_rope=use_rope),
        grid=(nt,),
        in_specs=in_specs,
        out_specs=out_specs,
        out_shape=out_shape,
        compiler_params=_cp("arbitrary"),
        name="mla_prep",
    )(*args)


def _attn_body(*refs, n_kv):
    q_ref = refs[0]
    k_refs = refs[1:1 + n_kv]
    v_refs = refs[1 + n_kv:1 + 2 * n_kv]
    o_ref = refs[1 + 2 * n_kv]
    q = q_ref[...]
    ss = [_dot(q, k_ref[...], NT) for k_ref in k_refs]
    m = jnp.max(ss[0], -1, keepdims=True)
    for s in ss[1:]:
        m = jnp.maximum(m, jnp.max(s, -1, keepdims=True))
    l = None
    acc = None
    for s, v_ref in zip(ss, v_refs):
        e = jnp.exp(s - m)
        ls = jnp.sum(e, -1, keepdims=True)
        l = ls if l is None else l + ls
        pv = _dot(e.astype(bf16), v_ref[...])
        acc = pv if acc is None else acc + pv
    o_ref[...] = acc / l


def attend(qf, kfs, vs, *, batch, lq, lks, tq=256):
    n_kv = len(kfs)
    nq = lq // tq
    in_specs = [pl.BlockSpec((tq, 2 * LANES), lambda b, h, i: (b * nq + i, h))]
    in_specs += [pl.BlockSpec((lk, 2 * LANES), lambda b, h, i: (b, h)) for lk in lks]
    in_specs += [pl.BlockSpec((lk, LANES), lambda b, h, i: (b, h)) for lk in lks]
    return pl.pallas_call(
        functools.partial(_attn_body, n_kv=n_kv),
        grid=(batch, MLA_H, nq),
        in_specs=in_specs,
        out_specs=pl.BlockSpec((tq, LANES), lambda b, h, i: (b * nq + i, h)),
        out_shape=jax.ShapeDtypeStruct((batch * lq, MLA_H * LANES), f32),
        compiler_params=_cp("arbitrary", "arbitrary", "arbitrary"),
        name="attend",
    )(qf, *kfs, *vs)


def _topk_rows(s, k, val_scr, idx_scr, payload=None):
    rows = s.shape[0]
    iota = lax.broadcasted_iota(i32, s.shape, 0)
    for r in range(k):
        m = jnp.max(s, axis=0, keepdims=True)
        idx = jnp.min(jnp.where(s == m, iota, rows), axis=0, keepdims=True)
        hit = iota == idx
        val_scr[r:r + 1, :] = m
        if payload is None:
            idx_scr[r:r + 1, :] = idx
        else:
            idx_scr[r:r + 1, :] = jnp.max(jnp.where(hit, payload, -1), axis=0, keepdims=True)
        s = jnp.where(hit, -jnp.inf, s)


def _pair_candidates(sv0, sv1, si0, si1):
    K = PEER_K
    sub = lax.broadcasted_iota(i32, (SUBLANES, sv0.shape[1]), 0)
    vals, idxs = [], []
    for a in range(SUBLANES):
        nb = K // (a + 1)
        for b0 in range(0, nb, SUBLANES):
            v = sv0[a:a + 1, :] + sv1[b0:b0 + SUBLANES, :]
            e = si0[a:a + 1, :] * PEER_NK + si1[b0:b0 + SUBLANES, :]
            if nb - b0 < SUBLANES:
                v = jnp.where(sub < nb - b0, v, -jnp.inf)
            vals.append(v)
            idxs.append(e)
    vals.append(sv0[SUBLANES:K, :] + sv1[0:1, :])
    idxs.append(si0[SUBLANES:K, :] * PEER_NK + si1[0:1, :])
    return jnp.concatenate(vals, axis=0), jnp.concatenate(idxs, axis=0)


def _peer_topk_body(q_ref, keys_ref, eidx_ref, gate_ref, sv, si, cv, ci, e_all, g_all):
    K = PEER_K
    for h in range(PEER_H):
        for p in range(2):
            qhp = q_ref[:, (2 * h + p) * LANES:(2 * h + p + 1) * LANES]
            s = _dot(keys_ref[h, p], qhp, NT, precision=HI)
            _topk_rows(s, K, sv.at[p], si.at[p])
        cand, cidx = _pair_candidates(sv[0], sv[1], si[0], si[1])
        _topk_rows(cand, K, cv, ci, payload=cidx)
        cs = cv[...]
        e = jnp.exp(cs - cs[0:1, :])
        g_all[h * K:(h + 1) * K, :] = e / jnp.sum(e, axis=0, keepdims=True)
        e_all[h * K:(h + 1) * K, :] = ci[...]
    eidx_ref[...] = e_all[...].T
    gate_ref[...] = g_all[...].T


def peer_topk(q, sub_keys, *, tt=256):
    n = q.shape[0]
    K = PEER_K
    return pl.pallas_call(
        _peer_topk_body,
        grid=(n // tt,),
        in_specs=[pl.BlockSpec((tt, q.shape[1]), lambda i: (i, 0)),
                  pl.BlockSpec(sub_keys.shape, lambda i: (0, 0, 0, 0))],
        out_specs=[pl.BlockSpec((tt, PEER_SEL), lambda i: (i, 0)),
                   pl.BlockSpec((tt, PEER_SEL), lambda i: (i, 0))],
        out_shape=[jax.ShapeDtypeStruct((n, PEER_SEL), i32), jax.ShapeDtypeStruct((n, PEER_SEL), f32)],
        scratch_shapes=[pltpu.VMEM((2, K, tt), f32), pltpu.VMEM((2, K, tt), i32),
                        pltpu.VMEM((K, tt), f32), pltpu.VMEM((K, tt), i32),
                        pltpu.VMEM((PEER_SEL, tt), i32), pltpu.VMEM((PEER_SEL, tt), f32)],
        compiler_params=_cp("arbitrary"),
        name="peer_topk",
    )(q, sub_keys)


HALF = SUBLANES // 2
HI_MASK = -65536
PACK_TE = 256


def _pack_body(x_ref, o_ref):
    i = pl.program_id(0)
    guard = jnp.logical_or(i == 0, i == pl.num_programs(0) - 1)

    @pl.when(guard)
    def _zero():
        o_ref[...] = jnp.zeros_like(o_ref)

    @pl.when(jnp.logical_not(guard))
    def _pack():
        half = x_ref.shape[1] // 2
        lo = lax.bitcast_convert_type(x_ref[:, :half].astype(bf16).astype(f32), i32)
        hi = lax.bitcast_convert_type(x_ref[:, half:].astype(bf16).astype(f32), i32)
        words = (hi & jnp.int32(HI_MASK)) | lax.shift_right_logical(lo, jnp.int32(16))
        for s in range(HALF):
            o_ref[pl.ds(s, PACK_TE, stride=HALF), :] = words[:, s * LANES:(s + 1) * LANES]


def pack_expert_table(tab):
    e, d = tab.shape
    nb = e // PACK_TE
    return pl.pallas_call(
        _pack_body,
        grid=(nb + 2,),
        in_specs=[pl.BlockSpec((PACK_TE, d), lambda i: (jnp.clip(i - 1, 0, nb - 1), 0))],
        out_specs=pl.BlockSpec((PACK_TE * HALF, LANES), lambda i: (i, 0)),
        out_shape=jax.ShapeDtypeStruct(((nb + 2) * PACK_TE * HALF, LANES), i32),
        compiler_params=_cp("arbitrary"),
        name="pack_table",
    )(tab)


def slot_major(a, nslot):
    n = a.shape[0]
    return a.reshape(n, PEER_SEL // nslot, nslot).transpose(2, 0, 1).reshape(nslot, -1)


def slot_codes(eidx, nslot):
    c = (eidx + PACK_TE) * HALF - jnp.where(jnp.arange(PEER_SEL) % 2 == 1, HALF, 0)
    return slot_major(c, nslot)


def _merged(tab_ref, ca, cb, lo_half):
    a = tab_ref[pl.ds(pl.multiple_of(ca, HALF), SUBLANES), :]
    b = tab_ref[pl.ds(pl.multiple_of(cb, HALF), SUBLANES), :]
    return jnp.where(lo_half, a, b)


def _lo(words):
    return lax.bitcast_convert_type(words << 16, f32)


def _hi(words):
    return lax.bitcast_convert_type(words & jnp.int32(HI_MASK), f32)


def _gelu_tanh(x):
    return 0.5 * x * (1.0 + jnp.tanh(math.sqrt(2.0 / math.pi) * (x + 0.044715 * (x * x * x))))


U_SLOTS = 16
U_GROUPS = PEER_SEL // U_SLOTS
U_FIN_UNROLL = 4


def _fold4(ms, sub):
    lo2 = (sub % 4) < 2
    b = [jnp.where(lo2, ms[k] + pltpu.roll(ms[k], SUBLANES - 2, 0), ms[k + 2] + pltpu.roll(ms[k + 2], 2, 0))
         for k in range(2)]
    lo1 = (sub % 2) < 1
    return jnp.where(lo1, b[0] + pltpu.roll(b[0], SUBLANES - 1, 0), b[1] + pltpu.roll(b[1], 1, 0))


def u_phase_perm():
    perm = []
    for g in range(U_GROUPS):
        for fold in range(2):
            for r in range(SUBLANES):
                perm.append(g * U_SLOTS + fold * SUBLANES + 2 * (r % 4) + r // 4)
    return perm


def _peer_u_body(*refs, tb):
    c_refs = refs[:U_SLOTS]
    gate_ref, h_ref, tab_ref, w_ref, z_ref, h8_ref = refs[U_SLOTS:]
    sub = lax.broadcasted_iota(i32, (SUBLANES, LANES), 0)
    lo_half = sub < HALF
    for s in range(SUBLANES):
        h8_ref[pl.ds(s, tb, stride=SUBLANES), :] = h_ref[:, s * LANES:(s + 1) * LANES]

    def token(t, carry):
        r8 = pl.multiple_of(t * SUBLANES, SUBLANES)
        hrow = h8_ref[pl.ds(r8, SUBLANES), :]
        hswap = pltpu.roll(hrow, HALF, 0)
        hl = jnp.where(lo_half, hrow, hswap)
        hh = jnp.where(lo_half, hswap, hrow)
        for g in range(U_GROUPS):
            for fold in range(2):
                ms = []
                for k in range(4):
                    s0 = fold * SUBLANES + 2 * k
                    words = _merged(tab_ref, c_refs[s0][t * U_GROUPS + g], c_refs[s0 + 1][t * U_GROUPS + g], lo_half)
                    ms.append(_lo(words) * hl + _hi(words) * hh)
                row0 = t * PEER_SEL + (2 * g + fold) * SUBLANES
                z_ref[pl.ds(pl.multiple_of(row0, SUBLANES), SUBLANES), :] = _fold4(ms, sub)
        return carry

    lax.fori_loop(0, tb, token, 0)

    def finish(tg, carry):
        for i in range(U_FIN_UNROLL):
            t = tg * U_FIN_UNROLL + i
            zt = z_ref[pl.ds(pl.multiple_of(t * PEER_SEL, PEER_SEL), PEER_SEL), :]
            w_ref[pl.ds(t, 1), :] = jnp.sum(zt.T, axis=0, keepdims=True)
        return carry

    lax.fori_loop(0, tb // U_FIN_UNROLL, finish, 0)
    w_ref[...] = gate_ref[...] * _gelu_tanh(w_ref[...])


def peer_u_gather(codes, gate_p, h, tab, *, tb=128):
    n = gate_p.shape[0]
    smem = [pl.BlockSpec((tb * U_GROUPS,), lambda i: (i,), memory_space=pltpu.SMEM) for _ in range(U_SLOTS)]
    return pl.pallas_call(
        functools.partial(_peer_u_body, tb=tb),
        grid=(n // tb,),
        in_specs=smem + [pl.BlockSpec((tb, PEER_SEL), lambda i: (i, 0)),
                         pl.BlockSpec((tb, D), lambda i: (i, 0)),
                         pl.BlockSpec(tab.shape, lambda i: (0, 0), pipeline_mode=pl.Buffered(1))],
        out_specs=pl.BlockSpec((tb, PEER_SEL), lambda i: (i, 0)),
        out_shape=jax.ShapeDtypeStruct((n, PEER_SEL), f32),
        scratch_shapes=[pltpu.VMEM((tb * PEER_SEL, LANES), f32), pltpu.VMEM((tb * SUBLANES, LANES), f32)],
        compiler_params=_cp("arbitrary"),
        name="peer_u",
    )(*[codes[s] for s in range(U_SLOTS)], gate_p, h, tab)


V_SLOTS = 4
V_GROUPS = PEER_SEL // V_SLOTS
V_UNROLL = 8


def _peer_v_body(*refs, tb):
    c_refs = refs[:V_SLOTS]
    w_refs = refs[V_SLOTS:2 * V_SLOTS]
    x_ref, m_ref, tab_ref, o_ref, y8_ref = refs[2 * V_SLOTS:]
    sub = lax.broadcasted_iota(i32, (SUBLANES, LANES), 0)
    lo_half = sub < HALF

    def token(t, carry):
        def group(g, accs):
            lo, hi = accs
            i = t * V_GROUPS + g
            for k in range(V_SLOTS // 2):
                words = _merged(tab_ref, c_refs[2 * k][i], c_refs[2 * k + 1][i], lo_half)
                wm = jnp.where(lo_half, w_refs[2 * k][i], w_refs[2 * k + 1][i])
                lo = lo + wm * _lo(words)
                hi = hi + wm * _hi(words)
            return lo, hi

        z = jnp.zeros((SUBLANES, LANES), f32)
        lo, hi = lax.fori_loop(0, V_GROUPS, group, (z, z), unroll=V_UNROLL)
        lo = lo + pltpu.roll(lo, HALF, 0)
        hi = hi + pltpu.roll(hi, HALF, 0)
        y8_ref[pl.ds(pl.multiple_of(t * SUBLANES, SUBLANES), SUBLANES), :] = jnp.where(lo_half, lo, hi)
        return carry

    lax.fori_loop(0, tb, token, 0)
    for s in range(SUBLANES):
        cols = slice(s * LANES, (s + 1) * LANES)
        o_ref[:, cols] = x_ref[:, cols] + m_ref[5:6, cols] * y8_ref[pl.ds(s, tb, stride=SUBLANES), :]


def peer_v_gather(codes, ws, x, mods_l, tab, *, prompt_rows, seq_len, tb=128):
    n = x.shape[0]
    pt, tps = prompt_rows // tb, seq_len // tb
    smem = [pl.BlockSpec((tb * V_GROUPS,), lambda i: (i,), memory_space=pltpu.SMEM) for _ in range(2 * V_SLOTS)]
    return pl.pallas_call(
        functools.partial(_peer_v_body, tb=tb),
        grid=(n // tb,),
        in_specs=smem + [pl.BlockSpec((tb, D), lambda i: (i, 0)),
                         pl.BlockSpec((None, 6, D), lambda i: (_mod_row(i, pt, tps), 0, 0)),
                         pl.BlockSpec(tab.shape, lambda i: (0, 0), pipeline_mode=pl.Buffered(1))],
        out_specs=pl.BlockSpec((tb, D), lambda i: (i, 0)),
        out_shape=jax.ShapeDtypeStruct((n, D), f32),
        scratch_shapes=[pltpu.VMEM((tb * SUBLANES, LANES), f32)],
        compiler_params=_cp("arbitrary"),
        name="peer_v",
    )(*[codes[s] for s in range(V_SLOTS)], *[ws[s] for s in range(V_SLOTS)], x, mods_l, tab)


def _rope_tables(seq):
    rows = seq // GRID_W
    row = jnp.repeat(jnp.arange(rows, dtype=f32), GRID_W)
    col = jnp.tile(jnp.arange(GRID_W, dtype=f32), rows)
    nfreq = MLA_ROPE // 4
    inv = jnp.power(ROPE_BASE, -jnp.arange(nfreq, dtype=f32) / nfreq)
    ar = row[:, None] * inv
    ac = col[:, None] * inv
    pad1 = jnp.ones((seq, LANES - MLA_ROPE), f32)
    pad0 = jnp.zeros((seq, LANES - MLA_ROPE), f32)
    cos = jnp.concatenate([jnp.cos(ar), jnp.cos(ar), jnp.cos(ac), jnp.cos(ac), pad1], -1)
    sin = jnp.concatenate([-jnp.sin(ar), jnp.sin(ar), -jnp.sin(ac), jnp.sin(ac), pad0], -1)
    return cos, sin


def _pad_heads(w, head_w, lo, hi):
    k = w.shape[0]
    w = w.reshape(k, -1, head_w)[:, :, lo:hi]
    return jnp.pad(w, ((0, 0), (0, 0), (0, LANES - (hi - lo)))).reshape(k, -1)


def kernel(x_prompt, x_sample, state_gdn, cache_mla_ckv, cache_mla_krope, c, c_ctx, w_mod, b_mod, norm_mix, norm_ffn, even_w_in, even_conv_w, gdn_a_log, gdn_dt_bias, gdn_o_norm, even_w_out, odd_w_in, mla_q_norm, mla_kv_norm, mla_w_uq, mla_w_ukv, mla_q_headnorm, mla_k_headnorm, odd_w_out, peer_w_q, peer_sub_keys, peer_u, peer_v):
    B, L, _ = x_prompt.shape
    BS, LS, _ = x_sample.shape
    depth = w_mod.shape[0]
    NP, NS = B * L, BS * LS
    past = cache_mla_ckv.shape[2]
    geo = dict(prompt_rows=NP, seq_len=LS)

    x = jnp.concatenate([x_prompt.reshape(NP, D), x_sample.reshape(NS, D)], 0)
    nrow = 1 + BS
    rpad = -nrow % SUBLANES
    cv = jnp.concatenate([c_ctx[None, :], c, jnp.zeros((rpad, D), f32)], 0)
    mods = ada_mods_all(cv, w_mod, b_mod).reshape(depth, nrow + rpad, 6, D)

    cd_c, cd_s = _dft_mats(LANES)
    cd = jnp.concatenate([cd_c, cd_s], 1).astype(bf16)
    fmats = {}
    for n in (L, LS):
        cl, sl = _dft_mats(n)
        fmats[n] = jnp.concatenate([cl, -sl], 1).astype(bf16)
    cos_t, sin_t = _rope_tables(LS)

    new_gdn, new_ckv, new_kr = [], [], []
    for l in range(depth):
        j = l // 2
        m_l = mods[l]
        if l % 2 == 0:
            wi = even_w_in[j]
            o0 = 3 * A_QK + A_QK
            ba = wi[:, o0:o0 + 4 * GDN_H].reshape(D, 4, GDN_H).transpose(0, 2, 1)
            ba = jnp.pad(ba, ((0, 0), (0, 0), (0, LANES - 4))).reshape(D, GDN_H * LANES)
            w_in = jnp.concatenate([wi[:, :o0], wi[:, o0 + 4 * GDN_H:], ba], 1).astype(bf16)
            p = normmm(x, m_l, 0, norm_mix[l], w_in, **geo)
            nega = -jnp.exp(gdn_a_log[j])
            kw = dict(ba_col0=(o0 + FN_W) // LANES)
            mix_p, st = gdn(p, nega, gdn_dt_bias[j], even_conv_w[j], gdn_o_norm[j], None,
                            row_off=0, batch=B, seq=L, **kw)
            mix_s, _ = gdn(p, nega, gdn_dt_bias[j], even_conv_w[j], gdn_o_norm[j], state_gdn[:, j],
                           row_off=NP, batch=BS, seq=LS, **kw)
            new_gdn.append(st)
            fb_p = fnet(p, cd, fmats[L], row_off=0, batch=B, seq=L, col_block=o0 // FN_W, tr=L)
            fb_s = fnet(p, cd, fmats[LS], row_off=NP, batch=BS, seq=LS, col_block=o0 // FN_W, tr=256)
            wo = even_w_out[j].astype(bf16)
            x = outproj(x, m_l, 2, [jnp.concatenate([mix_p, mix_s], 0), jnp.concatenate([fb_p, fb_s], 0)],
                        [wo[:A_QK], wo[A_QK:]], **geo)
        else:
            wi = odd_w_in[j]
            w_in = jnp.pad(wi, ((0, 0), (0, LANES - MLA_ROPE))).astype(bf16)
            p = normmm(x, m_l, 0, norm_mix[l], w_in, **geo)
            norm_ws = (mla_q_norm[j].reshape(1, -1), mla_kv_norm[j].reshape(1, -1))
            qh, kh = mla_q_headnorm[j], mla_k_headnorm[j]

            def split_hw(hw):
                return (hw[:MLA_NOPE].reshape(1, LANES),
                        jnp.pad(hw[MLA_NOPE:], (0, LANES - MLA_ROPE)).reshape(1, LANES))

            q_ws = (_pad_heads(mla_w_uq[j], MLA_QK, 0, MLA_NOPE).astype(bf16),
                    _pad_heads(mla_w_uq[j], MLA_QK, MLA_NOPE, MLA_QK).astype(bf16)) + split_hw(qh)
            kv_ws = (_pad_heads(mla_w_ukv[j], 2 * LANES, 0, LANES).astype(bf16),
                     _pad_heads(mla_w_ukv[j], 2 * LANES, LANES, 2 * LANES).astype(bf16)) + split_hw(kh)
            qf_p, kf_p, v_p, ckv_p, krp_p = mla_prep((p,), norm_ws, q_ws, kv_ws, None, row_off=0, rows=NP, seq=L)
            qf_s, kf_s, v_s, _, _ = mla_prep((p,), norm_ws, q_ws, kv_ws, (cos_t, sin_t), row_off=NP, rows=NS, seq=LS)
            ckv_c = cache_mla_ckv[:, j].reshape(BS * past, MLA_KVL)
            krp_c = jnp.pad(cache_mla_krope[:, j].reshape(BS * past, MLA_ROPE), ((0, 0), (0, LANES - MLA_ROPE)))
            kf_c, v_c = mla_prep((ckv_c, krp_c), None, None, kv_ws, None, row_off=0, rows=BS * past, seq=past)
            o_p = attend(qf_p, [kf_p], [v_p], batch=B, lq=L, lks=[L])
            o_s = attend(qf_s, [kf_s, kf_c], [v_s, v_c], batch=BS, lq=LS, lks=[LS, past])
            new_ckv.append(ckv_p.reshape(B, L, MLA_KVL))
            new_kr.append(krp_p[:, :MLA_ROPE].reshape(B, L, MLA_ROPE))
            x = outproj(x, m_l, 2, [jnp.concatenate([o_p, o_s], 0)], [odd_w_out[j].astype(bf16)], **geo)
        q, h = normmm(x, m_l, 3, norm_ffn[l], peer_w_q[l], with_h=True, **geo)
        eidx, gate = peer_topk(q, peer_sub_keys[l])
        perm = jnp.array(u_phase_perm(), i32)
        w_p = peer_u_gather(slot_codes(eidx, U_SLOTS), gate[:, perm], h, pack_expert_table(peer_u[l]))
        codes_v = slot_codes(eidx[:, perm], V_SLOTS)
        x = peer_v_gather(codes_v, slot_major(w_p, V_SLOTS), x, m_l, pack_expert_table(peer_v[l]), **geo)

    y_prompt = x[:NP].reshape(B, L, D)
    y_sample = x[NP:].reshape(BS, LS, D)
    return (y_prompt, y_sample, jnp.stack(new_gdn, 1), jnp.stack(new_ckv, 1), jnp.stack(new_kr, 1))
```

```python
import functools
import math

import jax
import jax.numpy as jnp
from jax import lax
from jax.experimental import pallas as pl
from jax.experimental.pallas import tpu as pltpu

f32 = jnp.float32
bf16 = jnp.bfloat16
i32 = jnp.int32

LANES = 128
SUBLANES = 8
VMEM_LIMIT = 56 * 1024 * 1024

EPS = 1e-6
D = 1024
GRID_W = 64
GDN_H = 4
GDN_DK = 128
GDN_CHUNK = 64
A_QK = GDN_H * GDN_DK
FN_W = 512
MLA_H = 8
MLA_QL = 512
MLA_KVL = 256
MLA_NOPE = 128
MLA_ROPE = 64
MLA_QK = MLA_NOPE + MLA_ROPE
ROPE_BASE = 10000.0
PEER_H = 8
PEER_NK = 128
PEER_K = 16
PEER_SEL = PEER_H * PEER_K
HALF = SUBLANES // 2
HI_MASK = -65536
PACK_TE = 256
GATHER_TB = 128
U_SLOTS = 16
U_GROUPS = PEER_SEL // U_SLOTS
V_SLOTS = 4
V_GROUPS = PEER_SEL // V_SLOTS

HI = lax.Precision.HIGHEST
NT = (((1,), (1,)), ((), ()))
TN = (((0,), (0,)), ((), ()))
NN = (((1,), (0,)), ((), ()))


def _dot(a, b, dims=NN, precision=None):
    return lax.dot_general(a, b, dims, preferred_element_type=f32, precision=precision)


def _cp(*sem):
    return pltpu.CompilerParams(dimension_semantics=sem, vmem_limit_bytes=VMEM_LIMIT)


def _mod_row(i, prompt_tiles, tiles_per_seq):
    return jnp.where(i < prompt_tiles, 0, 1 + (i - prompt_tiles) // tiles_per_seq)


def _mods_body(c_ref, w_ref, b_ref, o_ref):
    c = c_ref[...]
    o_ref[...] = _dot(c * jax.nn.sigmoid(c), w_ref[...], precision=HI) + b_ref[...]


def ada_mods_all(cv, w_mod, b_mod):
    depth, _, n6 = w_mod.shape
    r = cv.shape[0]
    tn = 1536
    return pl.pallas_call(
        _mods_body,
        grid=(depth, n6 // tn),
        in_specs=[pl.BlockSpec((r, D), lambda l, j: (0, 0)),
                  pl.BlockSpec((None, D, tn), lambda l, j: (l, 0, j)),
                  pl.BlockSpec((None, 1, tn), lambda l, j: (l, 0, j))],
        out_specs=pl.BlockSpec((None, r, tn), lambda l, j: (l, 0, j)),
        out_shape=jax.ShapeDtypeStruct((depth, r, n6), f32),
        compiler_params=_cp("arbitrary", "arbitrary"),
        name="ada_mods",
    )(cv, w_mod, b_mod.reshape(depth, 1, n6))


def _normmm_body(x_ref, m_ref, nw_ref, w_ref, *rest, shift_idx, split):
    if split:
        wlo_ref, rest = rest[0], rest[1:]
    o_ref, h_refs = rest[0], rest[1:]
    x = x_ref[...]
    y = x * lax.rsqrt(jnp.mean(x * x, -1, keepdims=True) + EPS) * nw_ref[...]
    h = y * (1.0 + m_ref[shift_idx + 1:shift_idx + 2, :]) + m_ref[shift_idx:shift_idx + 1, :]
    hb = h.astype(bf16)
    acc = _dot(hb, w_ref[...])
    if split:
        acc = acc + _dot((h - hb.astype(f32)).astype(bf16), w_ref[...]) + _dot(hb, wlo_ref[...])
    o_ref[...] = acc
    if h_refs:
        h_refs[0][...] = h


def normmm(x, mods_l, shift_idx, norm_w, w, *, prompt_rows, seq_len, tm=256, with_h=False):
    n = x.shape[0]
    nout = w.shape[1]
    split = w.dtype == f32
    ws = [w]
    if split:
        w_hi = w.astype(bf16)
        ws = [w_hi, (w - w_hi.astype(f32)).astype(bf16)]
    pt, tps = prompt_rows // tm, seq_len // tm
    out_shape = [jax.ShapeDtypeStruct((n, nout), f32)]
    out_specs = [pl.BlockSpec((tm, nout), lambda i: (i, 0))]
    if with_h:
        out_shape.append(jax.ShapeDtypeStruct((n, D), f32))
        out_specs.append(pl.BlockSpec((tm, D), lambda i: (i, 0)))
    res = pl.pallas_call(
        functools.partial(_normmm_body, shift_idx=shift_idx, split=split),
        grid=(n // tm,),
        in_specs=[pl.BlockSpec((tm, D), lambda i: (i, 0)),
                  pl.BlockSpec((None, 6, D), lambda i: (_mod_row(i, pt, tps), 0, 0)),
                  pl.BlockSpec((1, D), lambda i: (0, 0))] + [pl.BlockSpec((D, nout), lambda i: (0, 0)) for _ in ws],
        out_specs=out_specs,
        out_shape=out_shape,
        compiler_params=_cp("arbitrary"),
        name="normmm",
    )(x, mods_l, norm_w.reshape(1, D), *ws)
    return res if with_h else res[0]


def _outproj_body(*refs, n_in, gate_idx):
    x_ref, m_ref = refs[0], refs[1]
    a_refs = refs[2:2 + n_in]
    w_refs = refs[2 + n_in:2 + 2 * n_in]
    o_ref = refs[2 + 2 * n_in]
    y = _dot(a_refs[0][...].astype(bf16), w_refs[0][...])
    for a_ref, w_ref in zip(a_refs[1:], w_refs[1:]):
        y = y + _dot(a_ref[...].astype(bf16), w_ref[...])
    o_ref[...] = x_ref[...] + m_ref[gate_idx:gate_idx + 1, :] * y


def outproj(x, mods_l, gate_idx, acts, ws, *, prompt_rows, seq_len, tm=256):
    n = x.shape[0]
    pt, tps = prompt_rows // tm, seq_len // tm
    in_specs = [pl.BlockSpec((tm, D), lambda i: (i, 0)),
                pl.BlockSpec((None, 6, D), lambda i: (_mod_row(i, pt, tps), 0, 0))]
    in_specs += [pl.BlockSpec((tm, a.shape[1]), lambda i: (i, 0)) for a in acts]
    in_specs += [pl.BlockSpec(w.shape, lambda i: (0, 0)) for w in ws]
    return pl.pallas_call(
        functools.partial(_outproj_body, n_in=len(acts), gate_idx=gate_idx),
        grid=(n // tm,),
        in_specs=in_specs,
        out_specs=pl.BlockSpec((tm, D), lambda i: (i, 0)),
        out_shape=jax.ShapeDtypeStruct((n, D), f32),
        compiler_params=_cp("arbitrary"),
        name="outproj",
    )(x, mods_l, *acts, *ws)


def _softplus(x):
    return jnp.maximum(x, 0.0) + jnp.log1p(jnp.exp(-jnp.abs(x)))


def _gdn_body(nega_ref, dt_ref, q_ref, k_ref, v_ref, gate_ref, ba_ref, cwq_ref, cwk_ref, cwv_ref, onw_ref,
              *rest, seq, has_s0):
    if has_s0:
        s0_ref, rest = rest[0], rest[1:]
    out_ref, sout_ref, qs, ks, vs, us, ws, qes, kes, qks, egs = rest
    h = pl.program_id(1)
    C = GDN_CHUNK
    nch = seq // C

    row = lax.broadcasted_iota(i32, (seq, LANES), 0)
    first = row == 0
    last = row == seq - 1

    def conv_silu(x_ref, w_ref):
        x = x_ref[...]
        w = w_ref[...]
        prev = jnp.where(first, 0.0, pltpu.roll(x, 1, 0))
        nxt = jnp.where(last, 0.0, pltpu.roll(x, seq - 1, 0))
        y = prev * w[0:1, :] + x * w[1:2, :] + nxt * w[2:3, :]
        return y * jax.nn.sigmoid(y)

    def l2n(x):
        return x * lax.rsqrt(jnp.sum(x * x, -1, keepdims=True) + EPS)

    qs[...] = l2n(conv_silu(q_ref, cwq_ref)) * (GDN_DK ** -0.5)
    ks[...] = l2n(conv_silu(k_ref, cwk_ref))
    vs[...] = conv_silu(v_ref, cwv_ref)

    ii = lax.broadcasted_iota(i32, (C, C), 0)
    jj = lax.broadcasted_iota(i32, (C, C), 1)
    eye = (ii == jj).astype(f32)

    ones_cc = jnp.ones((C, C), f32)
    zeros_cc = jnp.zeros((C, C), f32)
    NB = 4
    same_block = (lax.broadcasted_iota(i32, (NB * C, NB * C), 0) // C
                  == lax.broadcasted_iota(i32, (NB * C, NB * C), 1) // C)

    def block_diag(m_cat):
        return jnp.where(same_block, jnp.concatenate([m_cat] * NB, axis=0), 0.0)

    def prep_pair(cp, carry):
        a_blocks, rhs_blocks, dest = [], [], []
        for cc in range(2):
            r0 = pl.multiple_of((2 * cp + cc) * C, C)
            qc = qs[pl.ds(r0, C), :]
            kc = ks[pl.ds(r0, C), :]
            vc = vs[pl.ds(r0, C), :]
            ba = ba_ref[pl.ds(r0, C), :]
            betas = [jax.nn.sigmoid(ba[:, d:d + 1]) for d in range(2)]
            kbs = [kc * b for b in betas]
            aq = _dot(jnp.concatenate(kbs + [qc], axis=0).astype(bf16), kc.astype(bf16), NT)
            for d in range(2):
                rel = (ii - jj) if d == 0 else (jj - ii)
                m_incl = rel >= 0
                tri = m_incl.astype(f32)
                tri_t = (rel <= 0).astype(f32)
                g = nega_ref[d, h] * _softplus(ba[:, 2 + d:3 + d] + dt_ref[d, h])
                g1 = jnp.broadcast_to(g, (C, C))
                lhs = jnp.concatenate([tri, -ones_cc], axis=1)
                rhs = jnp.concatenate([jnp.concatenate([g1, g1, g1], axis=1),
                                       jnp.concatenate([zeros_cc, zeros_cc, g1 * tri_t], axis=1)], axis=0)
                R = _dot(lhs, rhs, precision=HI)
                gcB = R[:, :LANES]
                gc = gcB[:, 0:1]
                decay = jnp.exp(jnp.where(m_incl, R[:, LANES:LANES + C], -jnp.inf))
                a_blocks.append(jnp.where(rel > 0, aq[d * C:(d + 1) * C] * decay, 0.0))
                qks[d, pl.ds(r0, C), 0:C] = jnp.where(m_incl, aq[2 * C:] * decay, 0.0)
                rhs_blocks.append(jnp.concatenate([vc * betas[d], kbs[d] * jnp.exp(gc)], axis=1))
                dest.append((d, r0))
                qes[d, pl.ds(r0, C), :] = qc * jnp.exp(gc)
                g_last = gcB[C - 1:C, :] if d == 0 else gcB[0:1, :]
                kes[d, pl.ds(r0, C), :] = kc * jnp.exp(g_last - gcB)
                egs[d, pl.ds(pl.multiple_of((2 * cp + cc) * SUBLANES, SUBLANES), SUBLANES), :] = jnp.broadcast_to(
                    jnp.exp(g_last), (SUBLANES, LANES))
        A = jnp.concatenate(a_blocks, axis=1)
        X = jnp.concatenate([eye] * NB, axis=1) - A
        P = _dot(A, block_diag(A), precision=HI)
        for _ in range(4):
            xp = _dot(jnp.concatenate([X, P], axis=0), block_diag(P), precision=HI)
            X = X + xp[:C]
            P = xp[C:]
        X = X + _dot(X, block_diag(P), precision=HI)
        zero_rhs = jnp.zeros((C, 2 * GDN_DK), f32)
        rhs_rows = [jnp.concatenate([zero_rhs] * k + [rhs_blocks[k]] + [zero_rhs] * (NB - 1 - k), axis=1)
                    for k in range(NB)]
        sol = _dot(X, jnp.concatenate(rhs_rows, axis=0), precision=HI)
        for k, (d, r0) in enumerate(dest):
            us[d, pl.ds(r0, C), :] = sol[:, 2 * k * GDN_DK:(2 * k + 1) * GDN_DK]
            ws[d, pl.ds(r0, C), :] = sol[:, (2 * k + 1) * GDN_DK:(2 * k + 2) * GDN_DK]
        return carry

    lax.fori_loop(0, nch // 2, prep_pair, 0)

    def advance(d, c, S):
        r0 = pl.multiple_of(c * C, C)
        wq = _dot(jnp.concatenate([ws[d, pl.ds(r0, C), :], qes[d, pl.ds(r0, C), :]], axis=0).astype(bf16),
                  S.astype(bf16))
        v_new = us[d, pl.ds(r0, C), :] - wq[:C]
        vb = v_new.astype(bf16)
        us[d, pl.ds(r0, C), :] = wq[C:] + _dot(qks[d, pl.ds(r0, C), 0:C].astype(bf16), vb)
        eg = egs[d, pl.ds(pl.multiple_of(c * SUBLANES, SUBLANES), 1), :]
        return S * eg + _dot(kes[d, pl.ds(r0, C), :].astype(bf16), vb, TN)

    def step(i, carry):
        return advance(0, i, carry[0]), advance(1, nch - 1 - i, carry[1])

    if has_s0:
        S0 = (s0_ref[0], s0_ref[1])
    else:
        S0 = (jnp.zeros((GDN_DK, GDN_DK), f32),) * 2
    Sf, Sb = lax.fori_loop(0, nch, step, S0)
    sout_ref[0] = Sf
    sout_ref[1] = Sb

    o = us[0] + us[1]
    y = o * lax.rsqrt(jnp.mean(o * o, -1, keepdims=True) + EPS) * onw_ref[...]
    gt = gate_ref[...]
    out_ref[...] = y * (gt * jax.nn.sigmoid(gt))


def gdn(p, nega, dtb, conv_w, onw, s0, *, row_off, batch, seq, ba_col0):
    rb = row_off // seq
    H = GDN_H
    nch = seq // GDN_CHUNK

    def col(cb):
        return pl.BlockSpec((seq, LANES), lambda b, h: (rb + b, cb + h), pipeline_mode=pl.Buffered(1))

    state_spec = pl.BlockSpec((None, 2, None, GDN_DK, GDN_DK), lambda b, h: (b, 0, h, 0, 0))
    in_specs = [pl.BlockSpec(memory_space=pltpu.SMEM), pl.BlockSpec(memory_space=pltpu.SMEM),
                col(0), col(H), col(2 * H), col(3 * H), col(ba_col0),
                pl.BlockSpec((3, LANES), lambda b, h: (0, h)),
                pl.BlockSpec((3, LANES), lambda b, h: (0, H + h)),
                pl.BlockSpec((3, LANES), lambda b, h: (0, 2 * H + h)),
                pl.BlockSpec((1, LANES), lambda b, h: (0, 0))]
    args = [nega, dtb, p, p, p, p, p, conv_w, conv_w, conv_w, onw.reshape(1, LANES)]
    if s0 is not None:
        in_specs.append(state_spec)
        args.append(s0)
    seq_buf = pltpu.VMEM((seq, LANES), f32)
    dir_buf = pltpu.VMEM((2, seq, LANES), f32)
    return pl.pallas_call(
        functools.partial(_gdn_body, seq=seq, has_s0=s0 is not None),
        grid=(batch, H),
        in_specs=in_specs,
        out_specs=[pl.BlockSpec((seq, LANES), lambda b, h: (b, h)), state_spec],
        out_shape=[jax.ShapeDtypeStruct((batch * seq, H * LANES), f32),
                   jax.ShapeDtypeStruct((batch, 2, H, GDN_DK, GDN_DK), f32)],
        scratch_shapes=[seq_buf, seq_buf, seq_buf, dir_buf, dir_buf, dir_buf, dir_buf, dir_buf,
                        pltpu.VMEM((2, nch * SUBLANES, LANES), f32)],
        compiler_params=_cp("arbitrary", "arbitrary"),
        name="gdn",
    )(*args)


def _fnet_body(x_ref, cd_ref, f_ref, o_ref, z_scr, *, seq):
    r = pl.program_id(1)

    @pl.when(r == 0)
    def _stage1():
        for g in range(FN_W // LANES):
            xg = x_ref[:, g * LANES:(g + 1) * LANES].astype(bf16)
            y = _dot(xg, cd_ref[...])
            z_scr[0:seq, g * LANES:(g + 1) * LANES] = y[:, :LANES].astype(bf16)
            z_scr[seq:2 * seq, g * LANES:(g + 1) * LANES] = y[:, LANES:].astype(bf16)

    scale = 1.0 / math.sqrt(seq * LANES)
    o_ref[...] = _dot(f_ref[...], z_scr[...]) * scale


def fnet(p, cd, fmat, *, row_off, batch, seq, col_block, tr):
    rb = row_off // seq
    return pl.pallas_call(
        functools.partial(_fnet_body, seq=seq),
        grid=(batch, seq // tr),
        in_specs=[pl.BlockSpec((seq, FN_W), lambda b, r: (rb + b, col_block)),
                  pl.BlockSpec((LANES, 2 * LANES), lambda b, r: (0, 0)),
                  pl.BlockSpec((tr, 2 * seq), lambda b, r: (r, 0))],
        out_specs=pl.BlockSpec((tr, FN_W), lambda b, r: (b * (seq // tr) + r, 0)),
        out_shape=jax.ShapeDtypeStruct((batch * seq, FN_W), f32),
        scratch_shapes=[pltpu.VMEM((2 * seq, FN_W), bf16)],
        compiler_params=_cp("arbitrary", "arbitrary"),
        name="fnet",
    )(p, cd, fmat)


def _dft_mats(n):
    k = jnp.arange(n, dtype=i32)
    ang = ((k[:, None] * k[None, :]) % n).astype(f32) * (2.0 * math.pi / n)
    return jnp.cos(ang), jnp.sin(ang)


def _rope(x, cos, sin_signed, first_half):
    partner = jnp.where(first_half, pltpu.roll(x, LANES - MLA_ROPE // 4, 1), pltpu.roll(x, MLA_ROPE // 4, 1))
    return x * cos + partner * sin_signed


def _mla_body(*refs, has_q, normalize, use_rope):
    it = iter(refs)
    if normalize:
        p_ref = next(it)
        qnw_ref = next(it)
        kvnw_ref = next(it)
    else:
        ckv_ref = next(it)
        krp_ref = next(it)
    if has_q:
        wqn_ref, wqr_ref, qhn_ref, qhr_ref = next(it), next(it), next(it), next(it)
    wkn_ref, wv_ref, khn_ref, khr_ref = next(it), next(it), next(it), next(it)
    if use_rope:
        cos_ref, sin_ref = next(it), next(it)
    if has_q:
        qf_ref = next(it)
    kf_ref, v_ref = next(it), next(it)
    if normalize:
        ckv_out, krp_out = next(it), next(it)

    def rms(x):
        return x * lax.rsqrt(jnp.mean(x * x, -1, keepdims=True) + EPS)

    if normalize:
        p = p_ref[...]
        cq = rms(p[:, :MLA_QL]) * qnw_ref[...]
        ckv = rms(p[:, MLA_QL:MLA_QL + MLA_KVL]) * kvnw_ref[...]
        krp = p[:, MLA_QL + MLA_KVL:]
        ckv_out[...] = ckv
        krp_out[...] = krp
    else:
        ckv = ckv_ref[...]
        krp = krp_ref[...]
    if use_rope:
        cos = cos_ref[...]
        sin = sin_ref[...]
        lane = lax.broadcasted_iota(i32, cos.shape, 1)
        first_half = (lane % (MLA_ROPE // 2)) < (MLA_ROPE // 4)

    def head_norm(a, b, wn, wr, scale):
        ss = jnp.sum(a * a, -1, keepdims=True) + jnp.sum(b * b, -1, keepdims=True)
        r = lax.rsqrt(ss * (1.0 / MLA_QK) + EPS)
        a = a * r * wn
        b = b * r * wr
        if use_rope:
            b = _rope(b, cos, sin, first_half)
        return (a * scale).astype(bf16), (b * scale).astype(bf16)

    if has_q:
        cqb = cq.astype(bf16)
        qn = _dot(cqb, wqn_ref[...])
        qr = _dot(cqb, wqr_ref[...])
        for h in range(MLA_H):
            a, b = head_norm(qn[:, h * LANES:(h + 1) * LANES], qr[:, h * LANES:(h + 1) * LANES],
                             qhn_ref[...], qhr_ref[...], MLA_QK ** -0.5)
            qf_ref[:, 2 * h * LANES:(2 * h + 1) * LANES] = a
            qf_ref[:, (2 * h + 1) * LANES:(2 * h + 2) * LANES] = b
    ckvb = ckv.astype(bf16)
    kn = _dot(ckvb, wkn_ref[...])
    v_ref[...] = _dot(ckvb, wv_ref[...]).astype(bf16)
    for h in range(MLA_H):
        a, b = head_norm(kn[:, h * LANES:(h + 1) * LANES], krp, khn_ref[...], khr_ref[...], 1.0)
        kf_ref[:, 2 * h * LANES:(2 * h + 1) * LANES] = a
        kf_ref[:, (2 * h + 1) * LANES:(2 * h + 2) * LANES] = b


def mla_prep(srcs, norm_ws, q_ws, kv_ws, rope_tabs, *, row_off, rows, seq, tm=256):
    normalize = norm_ws is not None
    has_q = q_ws is not None
    use_rope = rope_tabs is not None
    ro = row_off // tm
    nt = rows // tm
    hd = MLA_H * 2 * LANES

    def full(a):
        return pl.BlockSpec(a.shape, lambda i: (0,) * a.ndim)

    args, in_specs = [], []
    for s in srcs:
        args.append(s)
        in_specs.append(pl.BlockSpec((tm, s.shape[1]), lambda i: (ro + i, 0)))
    for group in (norm_ws, q_ws, kv_ws):
        if group is not None:
            for a in group:
                args.append(a)
                in_specs.append(full(a))
    if use_rope:
        tps = seq // tm
        for a in rope_tabs:
            args.append(a)
            in_specs.append(pl.BlockSpec((tm, LANES), lambda i: (i % tps, 0)))
    out_shape, out_specs = [], []

    def out(cols, dt):
        out_shape.append(jax.ShapeDtypeStruct((rows, cols), dt))
        out_specs.append(pl.BlockSpec((tm, cols), lambda i: (i, 0)))

    if has_q:
        out(hd, bf16)
    out(hd, bf16)
    out(MLA_H * LANES, bf16)
    if normalize:
        out(MLA_KVL, f32)
        out(LANES, f32)
    return pl.pallas_call(
        functools.partial(_mla_body, has_q=has_q, normalize=normalize, use_rope=use_rope),
        grid=(nt,),
        in_specs=in_specs,
        out_specs=out_specs,
        out_shape=out_shape,
        compiler_params=_cp("arbitrary"),
        name="mla_prep",
    )(*args)


def _attn_body(*refs, n_kv):
    q_ref = refs[0]
    k_refs = refs[1:1 + n_kv]
    v_refs = refs[1 + n_kv:1 + 2 * n_kv]
    o_ref = refs[1 + 2 * n_kv]
    q = q_ref[...]
    ss = [_dot(q, k_ref[...], NT) for k_ref in k_refs]
    m = jnp.max(ss[0], -1, keepdims=True)
    for s in ss[1:]:
        m = jnp.maximum(m, jnp.max(s, -1, keepdims=True))
    l = None
    acc = None
    for s, v_ref in zip(ss, v_refs):
        e = jnp.exp(s - m)
        ls = jnp.sum(e, -1, keepdims=True)
        l = ls if l is None else l + ls
        pv = _dot(e.astype(bf16), v_ref[...])
        acc = pv if acc is None else acc + pv
    o_ref[...] = acc / l


def attend(qf, kfs, vs, *, batch, lq, lks, tq=256):
    n_kv = len(kfs)
    nq = lq // tq
    in_specs = [pl.BlockSpec((tq, 2 * LANES), lambda b, h, i: (b * nq + i, h))]
    in_specs += [pl.BlockSpec((lk, 2 * LANES), lambda b, h, i: (b, h)) for lk in lks]
    in_specs += [pl.BlockSpec((lk, LANES), lambda b, h, i: (b, h)) for lk in lks]
    return pl.pallas_call(
        functools.partial(_attn_body, n_kv=n_kv),
        grid=(batch, MLA_H, nq),
        in_specs=in_specs,
        out_specs=pl.BlockSpec((tq, LANES), lambda b, h, i: (b * nq + i, h)),
        out_shape=jax.ShapeDtypeStruct((batch * lq, MLA_H * LANES), f32),
        compiler_params=_cp("arbitrary", "arbitrary", "arbitrary"),
        name="attend",
    )(qf, *kfs, *vs)


def _topk_rows(s, k, val_scr, idx_scr, payload=None):
    rows = s.shape[0]
    iota = lax.broadcasted_iota(i32, s.shape, 0)
    for r in range(k):
        m = jnp.max(s, axis=0, keepdims=True)
        idx = jnp.min(jnp.where(s == m, iota, rows), axis=0, keepdims=True)
        hit = iota == idx
        val_scr[r:r + 1, :] = m
        if payload is None:
            idx_scr[r:r + 1, :] = idx
        else:
            idx_scr[r:r + 1, :] = jnp.max(jnp.where(hit, payload, -1), axis=0, keepdims=True)
        s = jnp.where(hit, -jnp.inf, s)


def _pair_candidates(sv0, sv1, si0, si1):
    K = PEER_K
    sub = lax.broadcasted_iota(i32, (SUBLANES, sv0.shape[1]), 0)
    vals, idxs = [], []
    for a in range(SUBLANES):
        nb = K // (a + 1)
        for b0 in range(0, nb, SUBLANES):
            v = sv0[a:a + 1, :] + sv1[b0:b0 + SUBLANES, :]
            e = si0[a:a + 1, :] * PEER_NK + si1[b0:b0 + SUBLANES, :]
            if nb - b0 < SUBLANES:
                v = jnp.where(sub < nb - b0, v, -jnp.inf)
            vals.append(v)
            idxs.append(e)
    vals.append(sv0[SUBLANES:K, :] + sv1[0:1, :])
    idxs.append(si0[SUBLANES:K, :] * PEER_NK + si1[0:1, :])
    return jnp.concatenate(vals, axis=0), jnp.concatenate(idxs, axis=0)


def _u_slot_rows(s):
    return SUBLANES * (s // SUBLANES) + HALF * (s % 2) + (s % SUBLANES) // 2


def _peer_topk_body(q_ref, keys_ref, cu_ref, cv_ref, gate_ref, sv, si, cv, ci, e_all, g_all):
    K = PEER_K
    tt = q_ref.shape[0]
    for h in range(PEER_H):
        for p in range(2):
            qhp = q_ref[:, (2 * h + p) * LANES:(2 * h + p + 1) * LANES]
            s = _dot(keys_ref[h, p], qhp, NT, precision=HI)
            _topk_rows(s, K, sv.at[p], si.at[p])
        cand, cidx = _pair_candidates(sv[0], sv[1], si[0], si[1])
        _topk_rows(cand, K, cv, ci, payload=cidx)
        cs = cv[...]
        e = jnp.exp(cs - cs[0:1, :])
        g_all[h * K:(h + 1) * K, :] = e / jnp.sum(e, axis=0, keepdims=True)
        code = (ci[...] + PACK_TE) * HALF
        for b in range(tt // GATHER_TB):
            e_all[b, h * K:(h + 1) * K, :] = code[:, b * GATHER_TB:(b + 1) * GATHER_TB]
    gate_ref[...] = g_all[...].T
    for b in range(tt // GATHER_TB):
        for s in range(U_SLOTS):
            cu_ref[s, b] = e_all[b, pl.ds(_u_slot_rows(s), U_GROUPS, stride=U_SLOTS), :] - HALF * (s % 2)
        for s in range(V_SLOTS):
            cv_ref[s, b] = e_all[b, pl.ds(s, V_GROUPS, stride=V_SLOTS), :] - HALF * (s % 2)


def peer_topk(q, sub_keys, *, tt=256):
    n = q.shape[0]
    K = PEER_K
    nb = tt // GATHER_TB
    return pl.pallas_call(
        _peer_topk_body,
        grid=(n // tt,),
        in_specs=[pl.BlockSpec((tt, q.shape[1]), lambda i: (i, 0)),
                  pl.BlockSpec(sub_keys.shape, lambda i: (0, 0, 0, 0))],
        out_specs=[pl.BlockSpec((U_SLOTS, nb, U_GROUPS, GATHER_TB), lambda i: (0, i, 0, 0)),
                   pl.BlockSpec((V_SLOTS, nb, V_GROUPS, GATHER_TB), lambda i: (0, i, 0, 0)),
                   pl.BlockSpec((tt, PEER_SEL), lambda i: (i, 0))],
        out_shape=[jax.ShapeDtypeStruct((U_SLOTS, n // GATHER_TB, U_GROUPS, GATHER_TB), i32),
                   jax.ShapeDtypeStruct((V_SLOTS, n // GATHER_TB, V_GROUPS, GATHER_TB), i32),
                   jax.ShapeDtypeStruct((n, PEER_SEL), f32)],
        scratch_shapes=[pltpu.VMEM((2, K, tt), f32), pltpu.VMEM((2, K, tt), i32),
                        pltpu.VMEM((K, tt), f32), pltpu.VMEM((K, tt), i32),
                        pltpu.VMEM((nb, PEER_SEL, GATHER_TB), i32), pltpu.VMEM((PEER_SEL, tt), f32)],
        compiler_params=_cp("arbitrary"),
        name="peer_topk",
    )(q, sub_keys)


def _pack_body(x_ref, o_ref):
    i = pl.program_id(0)
    guard = jnp.logical_or(i == 0, i == pl.num_programs(0) - 1)

    @pl.when(guard)
    def _zero():
        o_ref[...] = jnp.zeros_like(o_ref)

    @pl.when(jnp.logical_not(guard))
    def _pack():
        half = x_ref.shape[1] // 2
        lo = lax.bitcast_convert_type(x_ref[:, :half].astype(bf16).astype(f32), i32)
        hi = lax.bitcast_convert_type(x_ref[:, half:].astype(bf16).astype(f32), i32)
        words = (hi & jnp.int32(HI_MASK)) | lax.shift_right_logical(lo, jnp.int32(16))
        for s in range(HALF):
            o_ref[pl.ds(s, PACK_TE, stride=HALF), :] = words[:, s * LANES:(s + 1) * LANES]


def pack_expert_table(tab):
    e, d = tab.shape
    nb = e // PACK_TE
    return pl.pallas_call(
        _pack_body,
        grid=(nb + 2,),
        in_specs=[pl.BlockSpec((PACK_TE, d), lambda i: (jnp.clip(i - 1, 0, nb - 1), 0))],
        out_specs=pl.BlockSpec((PACK_TE * HALF, LANES), lambda i: (i, 0)),
        out_shape=jax.ShapeDtypeStruct(((nb + 2) * PACK_TE * HALF, LANES), i32),
        compiler_params=_cp("arbitrary"),
        name="pack_table",
    )(tab)


def _merged(tab_ref, ca, cb, lo_half):
    a = tab_ref[pl.ds(pl.multiple_of(ca, HALF), SUBLANES), :]
    b = tab_ref[pl.ds(pl.multiple_of(cb, HALF), SUBLANES), :]
    return jnp.where(lo_half, a, b)


def _lo(words):
    return lax.bitcast_convert_type(words << 16, f32)


def _hi(words):
    return lax.bitcast_convert_type(words & jnp.int32(HI_MASK), f32)


def _gelu_tanh(x):
    return 0.5 * x * (1.0 + jnp.tanh(math.sqrt(2.0 / math.pi) * (x + 0.044715 * (x * x * x))))


U_FIN_UNROLL = 4


def _fold4(ms, sub):
    lo2 = (sub % 4) < 2
    b = [jnp.where(lo2, ms[k] + pltpu.roll(ms[k], SUBLANES - 2, 0), ms[k + 2] + pltpu.roll(ms[k + 2], 2, 0))
         for k in range(2)]
    lo1 = (sub % 2) < 1
    return jnp.where(lo1, b[0] + pltpu.roll(b[0], SUBLANES - 1, 0), b[1] + pltpu.roll(b[1], 1, 0))


def _peer_u_body(*refs, tb):
    c_refs = refs[:U_SLOTS]
    gate_ref, h_ref, tab_ref, wv_ref, z_ref, h8_ref, w_ref, wt_ref = refs[U_SLOTS:]
    sub = lax.broadcasted_iota(i32, (SUBLANES, LANES), 0)
    lo_half = sub < HALF
    for s in range(SUBLANES):
        h8_ref[pl.ds(s, tb, stride=SUBLANES), :] = h_ref[:, s * LANES:(s + 1) * LANES]

    def token(t, carry):
        r8 = pl.multiple_of(t * SUBLANES, SUBLANES)
        hrow = h8_ref[pl.ds(r8, SUBLANES), :]
        hswap = pltpu.roll(hrow, HALF, 0)
        hl = jnp.where(lo_half, hrow, hswap)
        hh = jnp.where(lo_half, hswap, hrow)
        for g in range(U_GROUPS):
            for fold in range(2):
                ms = []
                for k in range(4):
                    s0 = fold * SUBLANES + 2 * k
                    words = _merged(tab_ref, c_refs[s0][g * tb + t], c_refs[s0 + 1][g * tb + t], lo_half)
                    ms.append(_lo(words) * hl + _hi(words) * hh)
                row0 = t * PEER_SEL + (2 * g + fold) * SUBLANES
                z_ref[pl.ds(pl.multiple_of(row0, SUBLANES), SUBLANES), :] = _fold4(ms, sub)
        return carry

    lax.fori_loop(0, tb, token, 0)

    def finish(tg, carry):
        for i in range(U_FIN_UNROLL):
            t = tg * U_FIN_UNROLL + i
            zt = z_ref[pl.ds(pl.multiple_of(t * PEER_SEL, PEER_SEL), PEER_SEL), :]
            w_ref[pl.ds(t, 1), :] = jnp.sum(zt.T, axis=0, keepdims=True)
        return carry

    lax.fori_loop(0, tb // U_FIN_UNROLL, finish, 0)
    wt_ref[...] = (gate_ref[...] * _gelu_tanh(w_ref[...])).T
    for s in range(V_SLOTS):
        wv_ref[s] = wt_ref[pl.ds(s, V_GROUPS, stride=V_SLOTS), :]


def peer_u_gather(codes, gate, h, tab):
    tb = GATHER_TB
    n = gate.shape[0]
    codes = codes.reshape(U_SLOTS, n * U_GROUPS)
    smem = [pl.BlockSpec((tb * U_GROUPS,), lambda i: (i,), memory_space=pltpu.SMEM) for _ in range(U_SLOTS)]
    return pl.pallas_call(
        functools.partial(_peer_u_body, tb=tb),
        grid=(n // tb,),
        in_specs=smem + [pl.BlockSpec((tb, PEER_SEL), lambda i: (i, 0)),
                         pl.BlockSpec((tb, D), lambda i: (i, 0)),
                         pl.BlockSpec(tab.shape, lambda i: (0, 0), pipeline_mode=pl.Buffered(1))],
        out_specs=pl.BlockSpec((V_SLOTS, None, V_GROUPS, tb), lambda i: (0, i, 0, 0)),
        out_shape=jax.ShapeDtypeStruct((V_SLOTS, n // tb, V_GROUPS, tb), f32),
        scratch_shapes=[pltpu.VMEM((tb * PEER_SEL, LANES), f32), pltpu.VMEM((tb * SUBLANES, LANES), f32),
                        pltpu.VMEM((tb, PEER_SEL), f32), pltpu.VMEM((PEER_SEL, tb), f32)],
        compiler_params=_cp("arbitrary"),
        name="peer_u",
    )(*[codes[s] for s in range(U_SLOTS)], gate, h, tab)


V_UNROLL = 8


def _peer_v_body(*refs, tb):
    c_refs = refs[:V_SLOTS]
    w_refs = refs[V_SLOTS:2 * V_SLOTS]
    x_ref, m_ref, tab_ref, o_ref, y8_ref = refs[2 * V_SLOTS:]
    sub = lax.broadcasted_iota(i32, (SUBLANES, LANES), 0)
    lo_half = sub < HALF

    def token(t, carry):
        def group(g, accs):
            lo, hi = accs
            i = g * tb + t
            for k in range(V_SLOTS // 2):
                words = _merged(tab_ref, c_refs[2 * k][i], c_refs[2 * k + 1][i], lo_half)
                wm = jnp.where(lo_half, w_refs[2 * k][i], w_refs[2 * k + 1][i])
                lo = lo + wm * _lo(words)
                hi = hi + wm * _hi(words)
            return lo, hi

        z = jnp.zeros((SUBLANES, LANES), f32)
        lo, hi = lax.fori_loop(0, V_GROUPS, group, (z, z), unroll=V_UNROLL)
        lo = lo + pltpu.roll(lo, HALF, 0)
        hi = hi + pltpu.roll(hi, HALF, 0)
        y8_ref[pl.ds(pl.multiple_of(t * SUBLANES, SUBLANES), SUBLANES), :] = jnp.where(lo_half, lo, hi)
        return carry

    lax.fori_loop(0, tb, token, 0)
    for s in range(SUBLANES):
        cols = slice(s * LANES, (s + 1) * LANES)
        o_ref[:, cols] = x_ref[:, cols] + m_ref[5:6, cols] * y8_ref[pl.ds(s, tb, stride=SUBLANES), :]


def peer_v_gather(codes, ws, x, mods_l, tab, *, prompt_rows, seq_len):
    tb = GATHER_TB
    n = x.shape[0]
    pt, tps = prompt_rows // tb, seq_len // tb
    codes = codes.reshape(V_SLOTS, n * V_GROUPS)
    ws = ws.reshape(V_SLOTS, n * V_GROUPS)
    smem = [pl.BlockSpec((tb * V_GROUPS,), lambda i: (i,), memory_space=pltpu.SMEM) for _ in range(2 * V_SLOTS)]
    return pl.pallas_call(
        functools.partial(_peer_v_body, tb=tb),
        grid=(n // tb,),
        in_specs=smem + [pl.BlockSpec((tb, D), lambda i: (i, 0)),
                         pl.BlockSpec((None, 6, D), lambda i: (_mod_row(i, pt, tps), 0, 0)),
                         pl.BlockSpec(tab.shape, lambda i: (0, 0), pipeline_mode=pl.Buffered(1))],
        out_specs=pl.BlockSpec((tb, D), lambda i: (i, 0)),
        out_shape=jax.ShapeDtypeStruct((n, D), f32),
        scratch_shapes=[pltpu.VMEM((tb * SUBLANES, LANES), f32)],
        compiler_params=_cp("arbitrary"),
        name="peer_v",
    )(*[codes[s] for s in range(V_SLOTS)], *[ws[s] for s in range(V_SLOTS)], x, mods_l, tab)


def _rope_tables(seq):
    rows = seq // GRID_W
    row = jnp.repeat(jnp.arange(rows, dtype=f32), GRID_W)
    col = jnp.tile(jnp.arange(GRID_W, dtype=f32), rows)
    nfreq = MLA_ROPE // 4
    inv = jnp.power(ROPE_BASE, -jnp.arange(nfreq, dtype=f32) / nfreq)
    ar = row[:, None] * inv
    ac = col[:, None] * inv
    pad1 = jnp.ones((seq, LANES - MLA_ROPE), f32)
    pad0 = jnp.zeros((seq, LANES - MLA_ROPE), f32)
    cos = jnp.concatenate([jnp.cos(ar), jnp.cos(ar), jnp.cos(ac), jnp.cos(ac), pad1], -1)
    sin = jnp.concatenate([-jnp.sin(ar), jnp.sin(ar), -jnp.sin(ac), jnp.sin(ac), pad0], -1)
    return cos, sin


def _pad_heads(w, head_w, lo, hi):
    k = w.shape[0]
    w = w.reshape(k, -1, head_w)[:, :, lo:hi]
    return jnp.pad(w, ((0, 0), (0, 0), (0, LANES - (hi - lo)))).reshape(k, -1)


def kernel(x_prompt, x_sample, state_gdn, cache_mla_ckv, cache_mla_krope, c, c_ctx, w_mod, b_mod, norm_mix, norm_ffn, even_w_in, even_conv_w, gdn_a_log, gdn_dt_bias, gdn_o_norm, even_w_out, odd_w_in, mla_q_norm, mla_kv_norm, mla_w_uq, mla_w_ukv, mla_q_headnorm, mla_k_headnorm, odd_w_out, peer_w_q, peer_sub_keys, peer_u, peer_v):
    B, L, _ = x_prompt.shape
    BS, LS, _ = x_sample.shape
    depth = w_mod.shape[0]
    NP, NS = B * L, BS * LS
    past = cache_mla_ckv.shape[2]
    geo = dict(prompt_rows=NP, seq_len=LS)

    x = jnp.concatenate([x_prompt.reshape(NP, D), x_sample.reshape(NS, D)], 0)
    nrow = 1 + BS
    rpad = -nrow % SUBLANES
    cv = jnp.concatenate([c_ctx[None, :], c, jnp.zeros((rpad, D), f32)], 0)
    mods = ada_mods_all(cv, w_mod, b_mod).reshape(depth, nrow + rpad, 6, D)

    cd_c, cd_s = _dft_mats(LANES)
    cd = jnp.concatenate([cd_c, cd_s], 1).astype(bf16)
    fmats = {}
    for n in (L, LS):
        cl, sl = _dft_mats(n)
        fmats[n] = jnp.concatenate([cl, -sl], 1).astype(bf16)
    cos_t, sin_t = _rope_tables(LS)

    new_gdn, new_ckv, new_kr = [], [], []
    for l in range(depth):
        j = l // 2
        m_l = mods[l]
        if l % 2 == 0:
            wi = even_w_in[j]
            o0 = 3 * A_QK + A_QK
            ba = wi[:, o0:o0 + 4 * GDN_H].reshape(D, 4, GDN_H).transpose(0, 2, 1)
            ba = jnp.pad(ba, ((0, 0), (0, 0), (0, LANES - 4))).reshape(D, GDN_H * LANES)
            w_in = jnp.concatenate([wi[:, :o0], wi[:, o0 + 4 * GDN_H:], ba], 1).astype(bf16)
            p = normmm(x, m_l, 0, norm_mix[l], w_in, **geo)
            nega = -jnp.exp(gdn_a_log[j])
            kw = dict(ba_col0=(o0 + FN_W) // LANES)
            mix_p, st = gdn(p, nega, gdn_dt_bias[j], even_conv_w[j], gdn_o_norm[j], None,
                            row_off=0, batch=B, seq=L, **kw)
            mix_s, _ = gdn(p, nega, gdn_dt_bias[j], even_conv_w[j], gdn_o_norm[j], state_gdn[:, j],
                           row_off=NP, batch=BS, seq=LS, **kw)
            new_gdn.append(st)
            fb_p = fnet(p, cd, fmats[L], row_off=0, batch=B, seq=L, col_block=o0 // FN_W, tr=L)
            fb_s = fnet(p, cd, fmats[LS], row_off=NP, batch=BS, seq=LS, col_block=o0 // FN_W, tr=256)
            wo = even_w_out[j].astype(bf16)
            x = outproj(x, m_l, 2, [jnp.concatenate([mix_p, mix_s], 0), jnp.concatenate([fb_p, fb_s], 0)],
                        [wo[:A_QK], wo[A_QK:]], **geo)
        else:
            wi = odd_w_in[j]
            w_in = jnp.pad(wi, ((0, 0), (0, LANES - MLA_ROPE))).astype(bf16)
            p = normmm(x, m_l, 0, norm_mix[l], w_in, **geo)
            norm_ws = (mla_q_norm[j].reshape(1, -1), mla_kv_norm[j].reshape(1, -1))
            qh, kh = mla_q_headnorm[j], mla_k_headnorm[j]

            def split_hw(hw):
                return (hw[:MLA_NOPE].reshape(1, LANES),
                        jnp.pad(hw[MLA_NOPE:], (0, LANES - MLA_ROPE)).reshape(1, LANES))

            q_ws = (_pad_heads(mla_w_uq[j], MLA_QK, 0, MLA_NOPE).astype(bf16),
                    _pad_heads(mla_w_uq[j], MLA_QK, MLA_NOPE, MLA_QK).astype(bf16)) + split_hw(qh)
            kv_ws = (_pad_heads(mla_w_ukv[j], 2 * LANES, 0, LANES).astype(bf16),
                     _pad_heads(mla_w_ukv[j], 2 * LANES, LANES, 2 * LANES).astype(bf16)) + split_hw(kh)
            qf_p, kf_p, v_p, ckv_p, krp_p = mla_prep((p,), norm_ws, q_ws, kv_ws, None, row_off=0, rows=NP, seq=L)
            qf_s, kf_s, v_s, _, _ = mla_prep((p,), norm_ws, q_ws, kv_ws, (cos_t, sin_t), row_off=NP, rows=NS, seq=LS)
            ckv_c = cache_mla_ckv[:, j].reshape(BS * past, MLA_KVL)
            krp_c = jnp.pad(cache_mla_krope[:, j].reshape(BS * past, MLA_ROPE), ((0, 0), (0, LANES - MLA_ROPE)))
            kf_c, v_c = mla_prep((ckv_c, krp_c), None, None, kv_ws, None, row_off=0, rows=BS * past, seq=past)
            o_p = attend(qf_p, [kf_p], [v_p], batch=B, lq=L, lks=[L])
            o_s = attend(qf_s, [kf_s, kf_c], [v_s, v_c], batch=BS, lq=LS, lks=[LS, past])
            new_ckv.append(ckv_p.reshape(B, L, MLA_KVL))
            new_kr.append(krp_p[:, :MLA_ROPE].reshape(B, L, MLA_ROPE))
            x = outproj(x, m_l, 2, [jnp.concatenate([o_p, o_s], 0)], [odd_w_out[j].astype(bf16)], **geo)
        q, h = normmm(x, m_l, 3, norm_ffn[l], peer_w_q[l], with_h=True, **geo)
        codes_u, codes_v, gate = peer_topk(q, peer_sub_keys[l])
        w_v = peer_u_gather(codes_u, gate, h, pack_expert_table(peer_u[l]))
        x = peer_v_gather(codes_v, w_v, x, m_l, pack_expert_table(peer_v[l]), **geo)

    y_prompt = x[:NP].reshape(B, L, D)
    y_sample = x[NP:].reshape(BS, LS, D)
    return (y_prompt, y_sample, jnp.stack(new_gdn, 1), jnp.stack(new_ckv, 1), jnp.stack(new_kr, 1))
```

```python
import functools
import math

import jax
import jax.numpy as jnp
from jax import lax
from jax.experimental import pallas as pl
from jax.experimental.pallas import tpu as pltpu

f32 = jnp.float32
bf16 = jnp.bfloat16
i32 = jnp.int32

LANES = 128
SUBLANES = 8
VMEM_LIMIT = 56 * 1024 * 1024

EPS = 1e-6
D = 1024
GRID_W = 64
GDN_H = 4
GDN_DK = 128
GDN_CHUNK = 64
A_QK = GDN_H * GDN_DK
FN_W = 512
MLA_H = 8
MLA_QL = 512
MLA_KVL = 256
MLA_NOPE = 128
MLA_ROPE = 64
MLA_QK = MLA_NOPE + MLA_ROPE
ROPE_BASE = 10000.0
PEER_H = 8
PEER_NK = 128
PEER_K = 16
PEER_SEL = PEER_H * PEER_K
HALF = SUBLANES // 2
HI_MASK = -65536
PACK_TE = 256
GATHER_TB = 128
U_SLOTS = 16
U_GROUPS = PEER_SEL // U_SLOTS

HI = lax.Precision.HIGHEST
NT = (((1,), (1,)), ((), ()))
TN = (((0,), (0,)), ((), ()))
NN = (((1,), (0,)), ((), ()))


def _dot(a, b, dims=NN, precision=None):
    return lax.dot_general(a, b, dims, preferred_element_type=f32, precision=precision)


def _cp(*sem):
    return pltpu.CompilerParams(dimension_semantics=sem, vmem_limit_bytes=VMEM_LIMIT)


def _mod_row(i, prompt_tiles, tiles_per_seq):
    return jnp.where(i < prompt_tiles, 0, 1 + (i - prompt_tiles) // tiles_per_seq)


def _mods_body(c_ref, w_ref, b_ref, o_ref):
    c = c_ref[...]
    o_ref[...] = _dot(c * jax.nn.sigmoid(c), w_ref[...], precision=HI) + b_ref[...]


def ada_mods_all(cv, w_mod, b_mod):
    depth, _, n6 = w_mod.shape
    r = cv.shape[0]
    tn = 1536
    return pl.pallas_call(
        _mods_body,
        grid=(depth, n6 // tn),
        in_specs=[pl.BlockSpec((r, D), lambda l, j: (0, 0)),
                  pl.BlockSpec((None, D, tn), lambda l, j: (l, 0, j)),
                  pl.BlockSpec((None, 1, tn), lambda l, j: (l, 0, j))],
        out_specs=pl.BlockSpec((None, r, tn), lambda l, j: (l, 0, j)),
        out_shape=jax.ShapeDtypeStruct((depth, r, n6), f32),
        compiler_params=_cp("arbitrary", "arbitrary"),
        name="ada_mods",
    )(cv, w_mod, b_mod.reshape(depth, 1, n6))


def _normmm_body(x_ref, m_ref, nw_ref, w_ref, *rest, shift_idx, split):
    if split:
        wlo_ref, rest = rest[0], rest[1:]
    o_ref, h_refs = rest[0], rest[1:]
    x = x_ref[...]
    y = x * lax.rsqrt(jnp.mean(x * x, -1, keepdims=True) + EPS) * nw_ref[...]
    h = y * (1.0 + m_ref[shift_idx + 1:shift_idx + 2, :]) + m_ref[shift_idx:shift_idx + 1, :]
    hb = h.astype(bf16)
    acc = _dot(hb, w_ref[...])
    if split:
        acc = acc + _dot((h - hb.astype(f32)).astype(bf16), w_ref[...]) + _dot(hb, wlo_ref[...])
    o_ref[...] = acc
    if h_refs:
        h_refs[0][...] = h


def normmm(x, mods_l, shift_idx, norm_w, w, *, prompt_rows, seq_len, tm=256, with_h=False):
    n = x.shape[0]
    nout = w.shape[1]
    split = w.dtype == f32
    ws = [w]
    if split:
        w_hi = w.astype(bf16)
        ws = [w_hi, (w - w_hi.astype(f32)).astype(bf16)]
    pt, tps = prompt_rows // tm, seq_len // tm
    out_shape = [jax.ShapeDtypeStruct((n, nout), f32)]
    out_specs = [pl.BlockSpec((tm, nout), lambda i: (i, 0))]
    if with_h:
        out_shape.append(jax.ShapeDtypeStruct((n, D), f32))
        out_specs.append(pl.BlockSpec((tm, D), lambda i: (i, 0)))
    res = pl.pallas_call(
        functools.partial(_normmm_body, shift_idx=shift_idx, split=split),
        grid=(n // tm,),
        in_specs=[pl.BlockSpec((tm, D), lambda i: (i, 0)),
                  pl.BlockSpec((None, 6, D), lambda i: (_mod_row(i, pt, tps), 0, 0)),
                  pl.BlockSpec((1, D), lambda i: (0, 0))] + [pl.BlockSpec((D, nout), lambda i: (0, 0)) for _ in ws],
        out_specs=out_specs,
        out_shape=out_shape,
        compiler_params=_cp("arbitrary"),
        name="normmm",
    )(x, mods_l, norm_w.reshape(1, D), *ws)
    return res if with_h else res[0]


def _outproj_body(*refs, n_in, gate_idx):
    x_ref, m_ref = refs[0], refs[1]
    a_refs = refs[2:2 + n_in]
    w_refs = refs[2 + n_in:2 + 2 * n_in]
    o_ref = refs[2 + 2 * n_in]
    y = _dot(a_refs[0][...].astype(bf16), w_refs[0][...])
    for a_ref, w_ref in zip(a_refs[1:], w_refs[1:]):
        y = y + _dot(a_ref[...].astype(bf16), w_ref[...])
    o_ref[...] = x_ref[...] + m_ref[gate_idx:gate_idx + 1, :] * y


def outproj(x, mods_l, gate_idx, acts, ws, *, prompt_rows, seq_len, tm=256):
    n = x.shape[0]
    pt, tps = prompt_rows // tm, seq_len // tm
    in_specs = [pl.BlockSpec((tm, D), lambda i: (i, 0)),
                pl.BlockSpec((None, 6, D), lambda i: (_mod_row(i, pt, tps), 0, 0))]
    in_specs += [pl.BlockSpec((tm, a.shape[1]), lambda i: (i, 0)) for a in acts]
    in_specs += [pl.BlockSpec(w.shape, lambda i: (0, 0)) for w in ws]
    return pl.pallas_call(
        functools.partial(_outproj_body, n_in=len(acts), gate_idx=gate_idx),
        grid=(n // tm,),
        in_specs=in_specs,
        out_specs=pl.BlockSpec((tm, D), lambda i: (i, 0)),
        out_shape=jax.ShapeDtypeStruct((n, D), f32),
        compiler_params=_cp("arbitrary"),
        name="outproj",
    )(x, mods_l, *acts, *ws)


def _softplus(x):
    return jnp.maximum(x, 0.0) + jnp.log1p(jnp.exp(-jnp.abs(x)))


def _gdn_body(nega_ref, dt_ref, q_ref, k_ref, v_ref, gate_ref, ba_ref, cwq_ref, cwk_ref, cwv_ref, onw_ref,
              *rest, seq, has_s0):
    if has_s0:
        s0_ref, rest = rest[0], rest[1:]
    out_ref, sout_ref, qs, ks, vs, us, ws, qes, kes, qks, egs = rest
    h = pl.program_id(1)
    C = GDN_CHUNK
    nch = seq // C

    row = lax.broadcasted_iota(i32, (seq, LANES), 0)
    first = row == 0
    last = row == seq - 1

    def conv_silu(x_ref, w_ref):
        x = x_ref[...]
        w = w_ref[...]
        prev = jnp.where(first, 0.0, pltpu.roll(x, 1, 0))
        nxt = jnp.where(last, 0.0, pltpu.roll(x, seq - 1, 0))
        y = prev * w[0:1, :] + x * w[1:2, :] + nxt * w[2:3, :]
        return y * jax.nn.sigmoid(y)

    def l2n(x):
        return x * lax.rsqrt(jnp.sum(x * x, -1, keepdims=True) + EPS)

    qs[...] = l2n(conv_silu(q_ref, cwq_ref)) * (GDN_DK ** -0.5)
    ks[...] = l2n(conv_silu(k_ref, cwk_ref))
    vs[...] = conv_silu(v_ref, cwv_ref)

    ii = lax.broadcasted_iota(i32, (C, C), 0)
    jj = lax.broadcasted_iota(i32, (C, C), 1)
    eye = (ii == jj).astype(f32)

    ones_cc = jnp.ones((C, C), f32)
    zeros_cc = jnp.zeros((C, C), f32)
    NB = 4
    same_block = (lax.broadcasted_iota(i32, (NB * C, NB * C), 0) // C
                  == lax.broadcasted_iota(i32, (NB * C, NB * C), 1) // C)

    def block_diag(m_cat):
        return jnp.where(same_block, jnp.concatenate([m_cat] * NB, axis=0), 0.0)

    def prep_pair(cp, carry):
        a_blocks, rhs_blocks, dest = [], [], []
        for cc in range(2):
            r0 = pl.multiple_of((2 * cp + cc) * C, C)
            qc = qs[pl.ds(r0, C), :]
            kc = ks[pl.ds(r0, C), :]
            vc = vs[pl.ds(r0, C), :]
            ba = ba_ref[pl.ds(r0, C), :]
            betas = [jax.nn.sigmoid(ba[:, d:d + 1]) for d in range(2)]
            kbs = [kc * b for b in betas]
            aq = _dot(jnp.concatenate(kbs + [qc], axis=0).astype(bf16), kc.astype(bf16), NT)
            for d in range(2):
                rel = (ii - jj) if d == 0 else (jj - ii)
                m_incl = rel >= 0
                tri = m_incl.astype(f32)
                tri_t = (rel <= 0).astype(f32)
                g = nega_ref[d, h] * _softplus(ba[:, 2 + d:3 + d] + dt_ref[d, h])
                g1 = jnp.broadcast_to(g, (C, C))
                lhs = jnp.concatenate([tri, -ones_cc], axis=1)
                rhs = jnp.concatenate([jnp.concatenate([g1, g1, g1], axis=1),
                                       jnp.concatenate([zeros_cc, zeros_cc, g1 * tri_t], axis=1)], axis=0)
                R = _dot(lhs, rhs, precision=HI)
                gcB = R[:, :LANES]
                gc = gcB[:, 0:1]
                decay = jnp.exp(jnp.where(m_incl, R[:, LANES:LANES + C], -jnp.inf))
                a_blocks.append(jnp.where(rel > 0, aq[d * C:(d + 1) * C] * decay, 0.0))
                qks[d, pl.ds(r0, C), 0:C] = jnp.where(m_incl, aq[2 * C:] * decay, 0.0)
                rhs_blocks.append(jnp.concatenate([vc * betas[d], kbs[d] * jnp.exp(gc)], axis=1))
                dest.append((d, r0))
                qes[d, pl.ds(r0, C), :] = qc * jnp.exp(gc)
                g_last = gcB[C - 1:C, :] if d == 0 else gcB[0:1, :]
                kes[d, pl.ds(r0, C), :] = kc * jnp.exp(g_last - gcB)
                egs[d, pl.ds(pl.multiple_of((2 * cp + cc) * SUBLANES, SUBLANES), SUBLANES), :] = jnp.broadcast_to(
                    jnp.exp(g_last), (SUBLANES, LANES))
        A = jnp.concatenate(a_blocks, axis=1)
        X = jnp.concatenate([eye] * NB, axis=1) - A
        P = _dot(A, block_diag(A), precision=HI)
        for _ in range(4):
            xp = _dot(jnp.concatenate([X, P], axis=0), block_diag(P), precision=HI)
            X = X + xp[:C]
            P = xp[C:]
        X = X + _dot(X, block_diag(P), precision=HI)
        zero_rhs = jnp.zeros((C, 2 * GDN_DK), f32)
        rhs_rows = [jnp.concatenate([zero_rhs] * k + [rhs_blocks[k]] + [zero_rhs] * (NB - 1 - k), axis=1)
                    for k in range(NB)]
        sol = _dot(X, jnp.concatenate(rhs_rows, axis=0), precision=HI)
        for k, (d, r0) in enumerate(dest):
            us[d, pl.ds(r0, C), :] = sol[:, 2 * k * GDN_DK:(2 * k + 1) * GDN_DK]
            ws[d, pl.ds(r0, C), :] = sol[:, (2 * k + 1) * GDN_DK:(2 * k + 2) * GDN_DK]
        return carry

    lax.fori_loop(0, nch // 2, prep_pair, 0)

    def advance(d, c, S):
        r0 = pl.multiple_of(c * C, C)
        wq = _dot(jnp.concatenate([ws[d, pl.ds(r0, C), :], qes[d, pl.ds(r0, C), :]], axis=0).astype(bf16),
                  S.astype(bf16))
        v_new = us[d, pl.ds(r0, C), :] - wq[:C]
        vb = v_new.astype(bf16)
        us[d, pl.ds(r0, C), :] = wq[C:] + _dot(qks[d, pl.ds(r0, C), 0:C].astype(bf16), vb)
        eg = egs[d, pl.ds(pl.multiple_of(c * SUBLANES, SUBLANES), 1), :]
        return S * eg + _dot(kes[d, pl.ds(r0, C), :].astype(bf16), vb, TN)

    def step(i, carry):
        return advance(0, i, carry[0]), advance(1, nch - 1 - i, carry[1])

    if has_s0:
        S0 = (s0_ref[0], s0_ref[1])
    else:
        S0 = (jnp.zeros((GDN_DK, GDN_DK), f32),) * 2
    Sf, Sb = lax.fori_loop(0, nch, step, S0)
    sout_ref[0] = Sf
    sout_ref[1] = Sb

    o = us[0] + us[1]
    y = o * lax.rsqrt(jnp.mean(o * o, -1, keepdims=True) + EPS) * onw_ref[...]
    gt = gate_ref[...]
    out_ref[...] = y * (gt * jax.nn.sigmoid(gt))


def gdn(p, nega, dtb, conv_w, onw, s0, *, row_off, batch, seq, ba_col0):
    rb = row_off // seq
    H = GDN_H
    nch = seq // GDN_CHUNK

    def col(cb):
        return pl.BlockSpec((seq, LANES), lambda b, h: (rb + b, cb + h), pipeline_mode=pl.Buffered(1))

    state_spec = pl.BlockSpec((None, 2, None, GDN_DK, GDN_DK), lambda b, h: (b, 0, h, 0, 0))
    in_specs = [pl.BlockSpec(memory_space=pltpu.SMEM), pl.BlockSpec(memory_space=pltpu.SMEM),
                col(0), col(H), col(2 * H), col(3 * H), col(ba_col0),
                pl.BlockSpec((3, LANES), lambda b, h: (0, h)),
                pl.BlockSpec((3, LANES), lambda b, h: (0, H + h)),
                pl.BlockSpec((3, LANES), lambda b, h: (0, 2 * H + h)),
                pl.BlockSpec((1, LANES), lambda b, h: (0, 0))]
    args = [nega, dtb, p, p, p, p, p, conv_w, conv_w, conv_w, onw.reshape(1, LANES)]
    if s0 is not None:
        in_specs.append(state_spec)
        args.append(s0)
    seq_buf = pltpu.VMEM((seq, LANES), f32)
    dir_buf = pltpu.VMEM((2, seq, LANES), f32)
    return pl.pallas_call(
        functools.partial(_gdn_body, seq=seq, has_s0=s0 is not None),
        grid=(batch, H),
        in_specs=in_specs,
        out_specs=[pl.BlockSpec((seq, LANES), lambda b, h: (b, h)), state_spec],
        out_shape=[jax.ShapeDtypeStruct((batch * seq, H * LANES), f32),
                   jax.ShapeDtypeStruct((batch, 2, H, GDN_DK, GDN_DK), f32)],
        scratch_shapes=[seq_buf, seq_buf, seq_buf, dir_buf, dir_buf, dir_buf, dir_buf, dir_buf,
                        pltpu.VMEM((2, nch * SUBLANES, LANES), f32)],
        compiler_params=_cp("arbitrary", "arbitrary"),
        name="gdn",
    )(*args)


def _fnet_body(x_ref, cd_ref, f_ref, o_ref, z_scr, *, seq):
    r = pl.program_id(1)

    @pl.when(r == 0)
    def _stage1():
        for g in range(FN_W // LANES):
            xg = x_ref[:, g * LANES:(g + 1) * LANES].astype(bf16)
            y = _dot(xg, cd_ref[...])
            z_scr[0:seq, g * LANES:(g + 1) * LANES] = y[:, :LANES].astype(bf16)
            z_scr[seq:2 * seq, g * LANES:(g + 1) * LANES] = y[:, LANES:].astype(bf16)

    scale = 1.0 / math.sqrt(seq * LANES)
    o_ref[...] = _dot(f_ref[...], z_scr[...]) * scale


def fnet(p, cd, fmat, *, row_off, batch, seq, col_block, tr):
    rb = row_off // seq
    return pl.pallas_call(
        functools.partial(_fnet_body, seq=seq),
        grid=(batch, seq // tr),
        in_specs=[pl.BlockSpec((seq, FN_W), lambda b, r: (rb + b, col_block)),
                  pl.BlockSpec((LANES, 2 * LANES), lambda b, r: (0, 0)),
                  pl.BlockSpec((tr, 2 * seq), lambda b, r: (r, 0))],
        out_specs=pl.BlockSpec((tr, FN_W), lambda b, r: (b * (seq // tr) + r, 0)),
        out_shape=jax.ShapeDtypeStruct((batch * seq, FN_W), f32),
        scratch_shapes=[pltpu.VMEM((2 * seq, FN_W), bf16)],
        compiler_params=_cp("arbitrary", "arbitrary"),
        name="fnet",
    )(p, cd, fmat)


def _dft_mats(n):
    k = jnp.arange(n, dtype=i32)
    ang = ((k[:, None] * k[None, :]) % n).astype(f32) * (2.0 * math.pi / n)
    return jnp.cos(ang), jnp.sin(ang)


def _rope(x, cos, sin_signed, first_half):
    partner = jnp.where(first_half, pltpu.roll(x, LANES - MLA_ROPE // 4, 1), pltpu.roll(x, MLA_ROPE // 4, 1))
    return x * cos + partner * sin_signed


def _mla_body(*refs, has_q, normalize, use_rope):
    it = iter(refs)
    if normalize:
        p_ref = next(it)
        qnw_ref = next(it)
        kvnw_ref = next(it)
    else:
        ckv_ref = next(it)
        krp_ref = next(it)
    if has_q:
        wqn_ref, wqr_ref, qhn_ref, qhr_ref = next(it), next(it), next(it), next(it)
    wkn_ref, wv_ref, khn_ref, khr_ref = next(it), next(it), next(it), next(it)
    if use_rope:
        cos_ref, sin_ref = next(it), next(it)
    if has_q:
        qf_ref = next(it)
    kf_ref, v_ref = next(it), next(it)
    if normalize:
        ckv_out, krp_out = next(it), next(it)

    def rms(x):
        return x * lax.rsqrt(jnp.mean(x * x, -1, keepdims=True) + EPS)

    if normalize:
        p = p_ref[...]
        cq = rms(p[:, :MLA_QL]) * qnw_ref[...]
        ckv = rms(p[:, MLA_QL:MLA_QL + MLA_KVL]) * kvnw_ref[...]
        krp = p[:, MLA_QL + MLA_KVL:]
        ckv_out[...] = ckv
        krp_out[...] = krp
    else:
        ckv = ckv_ref[...]
        krp = krp_ref[...]
    if use_rope:
        cos = cos_ref[...]
        sin = sin_ref[...]
        lane = lax.broadcasted_iota(i32, cos.shape, 1)
        first_half = (lane % (MLA_ROPE // 2)) < (MLA_ROPE // 4)

    def head_norm(a, b, wn, wr, scale):
        ss = jnp.sum(a * a, -1, keepdims=True) + jnp.sum(b * b, -1, keepdims=True)
        r = lax.rsqrt(ss * (1.0 / MLA_QK) + EPS)
        a = a * r * wn
        b = b * r * wr
        if use_rope:
            b = _rope(b, cos, sin, first_half)
        return (a * scale).astype(bf16), (b * scale).astype(bf16)

    if has_q:
        cqb = cq.astype(bf16)
        qn = _dot(cqb, wqn_ref[...])
        qr = _dot(cqb, wqr_ref[...])
        for h in range(MLA_H):
            a, b = head_norm(qn[:, h * LANES:(h + 1) * LANES], qr[:, h * LANES:(h + 1) * LANES],
                             qhn_ref[...], qhr_ref[...], MLA_QK ** -0.5)
            qf_ref[:, 2 * h * LANES:(2 * h + 1) * LANES] = a
            qf_ref[:, (2 * h + 1) * LANES:(2 * h + 2) * LANES] = b
    ckvb = ckv.astype(bf16)
    kn = _dot(ckvb, wkn_ref[...])
    v_ref[...] = _dot(ckvb, wv_ref[...]).astype(bf16)
    for h in range(MLA_H):
        a, b = head_norm(kn[:, h * LANES:(h + 1) * LANES], krp, khn_ref[...], khr_ref[...], 1.0)
        kf_ref[:, 2 * h * LANES:(2 * h + 1) * LANES] = a
        kf_ref[:, (2 * h + 1) * LANES:(2 * h + 2) * LANES] = b


def mla_prep(srcs, norm_ws, q_ws, kv_ws, rope_tabs, *, row_off, rows, seq, tm=256):
    normalize = norm_ws is not None
    has_q = q_ws is not None
    use_rope = rope_tabs is not None
    ro = row_off // tm
    nt = rows // tm
    hd = MLA_H * 2 * LANES

    def full(a):
        return pl.BlockSpec(a.shape, lambda i: (0,) * a.ndim)

    args, in_specs = [], []
    for s in srcs:
        args.append(s)
        in_specs.append(pl.BlockSpec((tm, s.shape[1]), lambda i: (ro + i, 0)))
    for group in (norm_ws, q_ws, kv_ws):
        if group is not None:
            for a in group:
                args.append(a)
                in_specs.append(full(a))
    if use_rope:
        tps = seq // tm
        for a in rope_tabs:
            args.append(a)
            in_specs.append(pl.BlockSpec((tm, LANES), lambda i: (i % tps, 0)))
    out_shape, out_specs = [], []

    def out(cols, dt):
        out_shape.append(jax.ShapeDtypeStruct((rows, cols), dt))
        out_specs.append(pl.BlockSpec((tm, cols), lambda i: (i, 0)))

    if has_q:
        out(hd, bf16)
    out(hd, bf16)
    out(MLA_H * LANES, bf16)
    if normalize:
        out(MLA_KVL, f32)
        out(LANES, f32)
    return pl.pallas_call(
        functools.partial(_mla_body, has_q=has_q, normalize=normalize, use_rope=use_rope),
        grid=(nt,),
        in_specs=in_specs,
        out_specs=out_specs,
        out_shape=out_shape,
        compiler_params=_cp("arbitrary"),
        name="mla_prep",
    )(*args)


def _attn_body(*refs, n_kv):
    q_ref = refs[0]
    k_refs = refs[1:1 + n_kv]
    v_refs = refs[1 + n_kv:1 + 2 * n_kv]
    o_ref = refs[1 + 2 * n_kv]
    q = q_ref[...]
    ss = [_dot(q, k_ref[...], NT) for k_ref in k_refs]
    m = jnp.max(ss[0], -1, keepdims=True)
    for s in ss[1:]:
        m = jnp.maximum(m, jnp.max(s, -1, keepdims=True))
    l = None
    acc = None
    for s, v_ref in zip(ss, v_refs):
        e = jnp.exp(s - m)
        ls = jnp.sum(e, -1, keepdims=True)
        l = ls if l is None else l + ls
        pv = _dot(e.astype(bf16), v_ref[...])
        acc = pv if acc is None else acc + pv
    o_ref[...] = acc / l


def attend(qf, kfs, vs, *, batch, lq, lks, tq=256):
    n_kv = len(kfs)
    nq = lq // tq
    in_specs = [pl.BlockSpec((tq, 2 * LANES), lambda b, h, i: (b * nq + i, h))]
    in_specs += [pl.BlockSpec((lk, 2 * LANES), lambda b, h, i: (b, h)) for lk in lks]
    in_specs += [pl.BlockSpec((lk, LANES), lambda b, h, i: (b, h)) for lk in lks]
    return pl.pallas_call(
        functools.partial(_attn_body, n_kv=n_kv),
        grid=(batch, MLA_H, nq),
        in_specs=in_specs,
        out_specs=pl.BlockSpec((tq, LANES), lambda b, h, i: (b * nq + i, h)),
        out_shape=jax.ShapeDtypeStruct((batch * lq, MLA_H * LANES), f32),
        compiler_params=_cp("arbitrary", "arbitrary", "arbitrary"),
        name="attend",
    )(qf, *kfs, *vs)


def _topk_rows(s, k, val_scr, idx_scr, payload=None):
    rows = s.shape[0]
    iota = lax.broadcasted_iota(i32, s.shape, 0)
    for r in range(k):
        m = jnp.max(s, axis=0, keepdims=True)
        idx = jnp.min(jnp.where(s == m, iota, rows), axis=0, keepdims=True)
        hit = iota == idx
        val_scr[r:r + 1, :] = m
        if payload is None:
            idx_scr[r:r + 1, :] = idx
        else:
            idx_scr[r:r + 1, :] = jnp.max(jnp.where(hit, payload, -1), axis=0, keepdims=True)
        s = jnp.where(hit, -jnp.inf, s)


def _pair_candidates(sv0, sv1, si0, si1):
    K = PEER_K
    sub = lax.broadcasted_iota(i32, (SUBLANES, sv0.shape[1]), 0)
    vals, idxs = [], []
    for a in range(SUBLANES):
        nb = K // (a + 1)
        for b0 in range(0, nb, SUBLANES):
            v = sv0[a:a + 1, :] + sv1[b0:b0 + SUBLANES, :]
            e = si0[a:a + 1, :] * PEER_NK + si1[b0:b0 + SUBLANES, :]
            if nb - b0 < SUBLANES:
                v = jnp.where(sub < nb - b0, v, -jnp.inf)
            vals.append(v)
            idxs.append(e)
    vals.append(sv0[SUBLANES:K, :] + sv1[0:1, :])
    idxs.append(si0[SUBLANES:K, :] * PEER_NK + si1[0:1, :])
    return jnp.concatenate(vals, axis=0), jnp.concatenate(idxs, axis=0)


def _u_slot_rows(s):
    return SUBLANES * (s // SUBLANES) + HALF * (s % 2) + (s % SUBLANES) // 2


def _peer_topk_body(q_ref, keys_ref, cu_ref, gate_ref, sv, si, cv, ci, e_all, g_all):
    K = PEER_K
    tt = q_ref.shape[0]
    for h in range(PEER_H):
        for p in range(2):
            qhp = q_ref[:, (2 * h + p) * LANES:(2 * h + p + 1) * LANES]
            s = _dot(keys_ref[h, p], qhp, NT, precision=HI)
            _topk_rows(s, K, sv.at[p], si.at[p])
        cand, cidx = _pair_candidates(sv[0], sv[1], si[0], si[1])
        _topk_rows(cand, K, cv, ci, payload=cidx)
        cs = cv[...]
        e = jnp.exp(cs - cs[0:1, :])
        g_all[h * K:(h + 1) * K, :] = e / jnp.sum(e, axis=0, keepdims=True)
        code = (ci[...] + PACK_TE) * HALF
        for b in range(tt // GATHER_TB):
            e_all[b, h * K:(h + 1) * K, :] = code[:, b * GATHER_TB:(b + 1) * GATHER_TB]
    gate_ref[...] = g_all[...].T
    for b in range(tt // GATHER_TB):
        for s in range(U_SLOTS):
            cu_ref[s, b] = e_all[b, pl.ds(_u_slot_rows(s), U_GROUPS, stride=U_SLOTS), :] - HALF * (s % 2)


def peer_topk(q, sub_keys, *, tt=256):
    n = q.shape[0]
    K = PEER_K
    nb = tt // GATHER_TB
    return pl.pallas_call(
        _peer_topk_body,
        grid=(n // tt,),
        in_specs=[pl.BlockSpec((tt, q.shape[1]), lambda i: (i, 0)),
                  pl.BlockSpec(sub_keys.shape, lambda i: (0, 0, 0, 0))],
        out_specs=[pl.BlockSpec((U_SLOTS, nb, U_GROUPS, GATHER_TB), lambda i: (0, i, 0, 0)),
                   pl.BlockSpec((tt, PEER_SEL), lambda i: (i, 0))],
        out_shape=[jax.ShapeDtypeStruct((U_SLOTS, n // GATHER_TB, U_GROUPS, GATHER_TB), i32),
                   jax.ShapeDtypeStruct((n, PEER_SEL), f32)],
        scratch_shapes=[pltpu.VMEM((2, K, tt), f32), pltpu.VMEM((2, K, tt), i32),
                        pltpu.VMEM((K, tt), f32), pltpu.VMEM((K, tt), i32),
                        pltpu.VMEM((nb, PEER_SEL, GATHER_TB), i32), pltpu.VMEM((PEER_SEL, tt), f32)],
        compiler_params=_cp("arbitrary"),
        name="peer_topk",
    )(q, sub_keys)


def _pack_body(x_ref, o_ref):
    i = pl.program_id(0)
    guard = jnp.logical_or(i == 0, i == pl.num_programs(0) - 1)

    @pl.when(guard)
    def _zero():
        o_ref[...] = jnp.zeros_like(o_ref)

    @pl.when(jnp.logical_not(guard))
    def _pack():
        half = x_ref.shape[1] // 2
        lo = lax.bitcast_convert_type(x_ref[:, :half].astype(bf16).astype(f32), i32)
        hi = lax.bitcast_convert_type(x_ref[:, half:].astype(bf16).astype(f32), i32)
        words = (hi & jnp.int32(HI_MASK)) | lax.shift_right_logical(lo, jnp.int32(16))
        for s in range(HALF):
            o_ref[pl.ds(s, PACK_TE, stride=HALF), :] = words[:, s * LANES:(s + 1) * LANES]


def pack_expert_table(tab):
    e, d = tab.shape
    nb = e // PACK_TE
    return pl.pallas_call(
        _pack_body,
        grid=(nb + 2,),
        in_specs=[pl.BlockSpec((PACK_TE, d), lambda i: (jnp.clip(i - 1, 0, nb - 1), 0))],
        out_specs=pl.BlockSpec((PACK_TE * HALF, LANES), lambda i: (i, 0)),
        out_shape=jax.ShapeDtypeStruct(((nb + 2) * PACK_TE * HALF, LANES), i32),
        compiler_params=_cp("arbitrary"),
        name="pack_table",
    )(tab)


def _merged(tab_ref, ca, cb, lo_half):
    a = tab_ref[pl.ds(pl.multiple_of(ca, HALF), SUBLANES), :]
    b = tab_ref[pl.ds(pl.multiple_of(cb, HALF), SUBLANES), :]
    return jnp.where(lo_half, a, b)


def _lo(words):
    return lax.bitcast_convert_type(words << 16, f32)


def _hi(words):
    return lax.bitcast_convert_type(words & jnp.int32(HI_MASK), f32)


def _gelu_tanh(x):
    return 0.5 * x * (1.0 + jnp.tanh(math.sqrt(2.0 / math.pi) * (x + 0.044715 * (x * x * x))))


U_FIN_UNROLL = 16


def _fold4(ms, sub):
    lo2 = (sub % 4) < 2
    b = [jnp.where(lo2, ms[k] + pltpu.roll(ms[k], SUBLANES - 2, 0), ms[k + 2] + pltpu.roll(ms[k + 2], 2, 0))
         for k in range(2)]
    lo1 = (sub % 2) < 1
    return jnp.where(lo1, b[0] + pltpu.roll(b[0], SUBLANES - 1, 0), b[1] + pltpu.roll(b[1], 1, 0))


def _peer_u_body(*refs, tb):
    c_refs = refs[:U_SLOTS]
    gate_ref, h_ref, tab_ref, wt_ref, z_ref, h8_ref, w_ref = refs[U_SLOTS:]
    sub = lax.broadcasted_iota(i32, (SUBLANES, LANES), 0)
    lo_half = sub < HALF
    for s in range(SUBLANES):
        h8_ref[pl.ds(s, tb, stride=SUBLANES), :] = h_ref[:, s * LANES:(s + 1) * LANES]

    def token(t, carry):
        r8 = pl.multiple_of(t * SUBLANES, SUBLANES)
        hrow = h8_ref[pl.ds(r8, SUBLANES), :]
        hswap = pltpu.roll(hrow, HALF, 0)
        hl = jnp.where(lo_half, hrow, hswap)
        hh = jnp.where(lo_half, hswap, hrow)
        for g in range(U_GROUPS):
            for fold in range(2):
                ms = []
                for k in range(4):
                    s0 = fold * SUBLANES + 2 * k
                    words = _merged(tab_ref, c_refs[s0][g * tb + t], c_refs[s0 + 1][g * tb + t], lo_half)
                    ms.append(_lo(words) * hl + _hi(words) * hh)
                row0 = t * PEER_SEL + (2 * g + fold) * SUBLANES
                z_ref[pl.ds(pl.multiple_of(row0, SUBLANES), SUBLANES), :] = _fold4(ms, sub)
        return carry

    lax.fori_loop(0, tb, token, 0)

    def finish(tg, carry):
        for i in range(U_FIN_UNROLL):
            t = tg * U_FIN_UNROLL + i
            zt = z_ref[pl.ds(pl.multiple_of(t * PEER_SEL, PEER_SEL), PEER_SEL), :]
            w_ref[pl.ds(t, 1), :] = jnp.sum(zt.T, axis=0, keepdims=True)
        return carry

    lax.fori_loop(0, tb // U_FIN_UNROLL, finish, 0)
    wt_ref[...] = (gate_ref[...] * _gelu_tanh(w_ref[...])).T


def peer_u_gather(codes, gate, h, tab):
    tb = GATHER_TB
    n = gate.shape[0]
    codes = codes.reshape(U_SLOTS, n * U_GROUPS)
    smem = [pl.BlockSpec((tb * U_GROUPS,), lambda i: (i,), memory_space=pltpu.SMEM) for _ in range(U_SLOTS)]
    return pl.pallas_call(
        functools.partial(_peer_u_body, tb=tb),
        grid=(n // tb,),
        in_specs=smem + [pl.BlockSpec((tb, PEER_SEL), lambda i: (i, 0)),
                         pl.BlockSpec((tb, D), lambda i: (i, 0)),
                         pl.BlockSpec(tab.shape, lambda i: (0, 0), pipeline_mode=pl.Buffered(1))],
        out_specs=pl.BlockSpec((None, PEER_SEL, tb), lambda i: (i, 0, 0)),
        out_shape=jax.ShapeDtypeStruct((n // tb, PEER_SEL, tb), f32),
        scratch_shapes=[pltpu.VMEM((tb * PEER_SEL, LANES), f32), pltpu.VMEM((tb * SUBLANES, LANES), f32),
                        pltpu.VMEM((tb, PEER_SEL), f32)],
        compiler_params=_cp("arbitrary"),
        name="peer_u",
    )(*[codes[s] for s in range(U_SLOTS)], gate, h, tab)


def _peer_v_body(*refs, tb):
    c_refs = refs[:U_SLOTS]
    wt_ref, x_ref, m_ref, tab_ref, o_ref, y8_ref, wx_ref = refs[U_SLOTS:]
    sub = lax.broadcasted_iota(i32, (SUBLANES, LANES), 0)
    lo_half = sub < HALF

    def expanded(t):
        return jnp.take_along_axis(wt_ref[...], jnp.full((PEER_SEL, tb), t, i32), axis=1)

    wx_ref[...] = expanded(0)

    def token(t, carry):
        nxt = expanded(jnp.minimum(t + 1, tb - 1))
        accs = [jnp.zeros((SUBLANES, LANES), f32) for _ in range(4)]
        for g in range(U_GROUPS):
            for fold in range(2):
                for k in range(4):
                    s0 = fold * SUBLANES + 2 * k
                    ja = g * U_SLOTS + _u_slot_rows(s0)
                    jb = g * U_SLOTS + _u_slot_rows(s0 + 1)
                    words = _merged(tab_ref, c_refs[s0][g * tb + t], c_refs[s0 + 1][g * tb + t], lo_half)
                    wm = jnp.where(lo_half, jnp.broadcast_to(wx_ref[ja:ja + 1, :], (SUBLANES, LANES)),
                                   jnp.broadcast_to(wx_ref[jb:jb + 1, :], (SUBLANES, LANES)))
                    accs[2 * (k % 2)] = accs[2 * (k % 2)] + wm * _lo(words)
                    accs[2 * (k % 2) + 1] = accs[2 * (k % 2) + 1] + wm * _hi(words)
        lo = accs[0] + accs[2]
        hi = accs[1] + accs[3]
        lo = lo + pltpu.roll(lo, HALF, 0)
        hi = hi + pltpu.roll(hi, HALF, 0)
        y8_ref[pl.ds(pl.multiple_of(t * SUBLANES, SUBLANES), SUBLANES), :] = jnp.where(lo_half, lo, hi)
        wx_ref[...] = nxt
        return carry

    lax.fori_loop(0, tb, token, 0)
    for s in range(SUBLANES):
        cols = slice(s * LANES, (s + 1) * LANES)
        o_ref[:, cols] = x_ref[:, cols] + m_ref[5:6, cols] * y8_ref[pl.ds(s, tb, stride=SUBLANES), :]


def peer_v_gather(codes, wt, x, mods_l, tab, *, prompt_rows, seq_len):
    tb = GATHER_TB
    n = x.shape[0]
    pt, tps = prompt_rows // tb, seq_len // tb
    codes = codes.reshape(U_SLOTS, n * U_GROUPS)
    smem = [pl.BlockSpec((tb * U_GROUPS,), lambda i: (i,), memory_space=pltpu.SMEM) for _ in range(U_SLOTS)]
    return pl.pallas_call(
        functools.partial(_peer_v_body, tb=tb),
        grid=(n // tb,),
        in_specs=smem + [pl.BlockSpec((None, PEER_SEL, tb), lambda i: (i, 0, 0)),
                         pl.BlockSpec((tb, D), lambda i: (i, 0)),
                         pl.BlockSpec((None, 6, D), lambda i: (_mod_row(i, pt, tps), 0, 0)),
                         pl.BlockSpec(tab.shape, lambda i: (0, 0), pipeline_mode=pl.Buffered(1))],
        out_specs=pl.BlockSpec((tb, D), lambda i: (i, 0)),
        out_shape=jax.ShapeDtypeStruct((n, D), f32),
        scratch_shapes=[pltpu.VMEM((tb * SUBLANES, LANES), f32), pltpu.VMEM((PEER_SEL, LANES), f32)],
        compiler_params=_cp("arbitrary"),
        name="peer_v",
    )(*[codes[s] for s in range(U_SLOTS)], wt, x, mods_l, tab)


def _rope_tables(seq):
    rows = seq // GRID_W
    row = jnp.repeat(jnp.arange(rows, dtype=f32), GRID_W)
    col = jnp.tile(jnp.arange(GRID_W, dtype=f32), rows)
    nfreq = MLA_ROPE // 4
    inv = jnp.power(ROPE_BASE, -jnp.arange(nfreq, dtype=f32) / nfreq)
    ar = row[:, None] * inv
    ac = col[:, None] * inv
    pad1 = jnp.ones((seq, LANES - MLA_ROPE), f32)
    pad0 = jnp.zeros((seq, LANES - MLA_ROPE), f32)
    cos = jnp.concatenate([jnp.cos(ar), jnp.cos(ar), jnp.cos(ac), jnp.cos(ac), pad1], -1)
    sin = jnp.concatenate([-jnp.sin(ar), jnp.sin(ar), -jnp.sin(ac), jnp.sin(ac), pad0], -1)
    return cos, sin


def _pad_heads(w, head_w, lo, hi):
    k = w.shape[0]
    w = w.reshape(k, -1, head_w)[:, :, lo:hi]
    return jnp.pad(w, ((0, 0), (0, 0), (0, LANES - (hi - lo)))).reshape(k, -1)


def kernel(x_prompt, x_sample, state_gdn, cache_mla_ckv, cache_mla_krope, c, c_ctx, w_mod, b_mod, norm_mix, norm_ffn, even_w_in, even_conv_w, gdn_a_log, gdn_dt_bias, gdn_o_norm, even_w_out, odd_w_in, mla_q_norm, mla_kv_norm, mla_w_uq, mla_w_ukv, mla_q_headnorm, mla_k_headnorm, odd_w_out, peer_w_q, peer_sub_keys, peer_u, peer_v):
    B, L, _ = x_prompt.shape
    BS, LS, _ = x_sample.shape
    depth = w_mod.shape[0]
    NP, NS = B * L, BS * LS
    past = cache_mla_ckv.shape[2]
    geo = dict(prompt_rows=NP, seq_len=LS)

    x = jnp.concatenate([x_prompt.reshape(NP, D), x_sample.reshape(NS, D)], 0)
    nrow = 1 + BS
    rpad = -nrow % SUBLANES
    cv = jnp.concatenate([c_ctx[None, :], c, jnp.zeros((rpad, D), f32)], 0)
    mods = ada_mods_all(cv, w_mod, b_mod).reshape(depth, nrow + rpad, 6, D)

    cd_c, cd_s = _dft_mats(LANES)
    cd = jnp.concatenate([cd_c, cd_s], 1).astype(bf16)
    fmats = {}
    for n in (L, LS):
        cl, sl = _dft_mats(n)
        fmats[n] = jnp.concatenate([cl, -sl], 1).astype(bf16)
    cos_t, sin_t = _rope_tables(LS)

    new_gdn, new_ckv, new_kr = [], [], []
    for l in range(depth):
        j = l // 2
        m_l = mods[l]
        if l % 2 == 0:
            wi = even_w_in[j]
            o0 = 3 * A_QK + A_QK
            ba = wi[:, o0:o0 + 4 * GDN_H].reshape(D, 4, GDN_H).transpose(0, 2, 1)
            ba = jnp.pad(ba, ((0, 0), (0, 0), (0, LANES - 4))).reshape(D, GDN_H * LANES)
            w_in = jnp.concatenate([wi[:, :o0], wi[:, o0 + 4 * GDN_H:], ba], 1).astype(bf16)
            p = normmm(x, m_l, 0, norm_mix[l], w_in, **geo)
            nega = -jnp.exp(gdn_a_log[j])
            kw = dict(ba_col0=(o0 + FN_W) // LANES)
            mix_p, st = gdn(p, nega, gdn_dt_bias[j], even_conv_w[j], gdn_o_norm[j], None,
                            row_off=0, batch=B, seq=L, **kw)
            mix_s, _ = gdn(p, nega, gdn_dt_bias[j], even_conv_w[j], gdn_o_norm[j], state_gdn[:, j],
                           row_off=NP, batch=BS, seq=LS, **kw)
            new_gdn.append(st)
            fb_p = fnet(p, cd, fmats[L], row_off=0, batch=B, seq=L, col_block=o0 // FN_W, tr=L)
            fb_s = fnet(p, cd, fmats[LS], row_off=NP, batch=BS, seq=LS, col_block=o0 // FN_W, tr=256)
            wo = even_w_out[j].astype(bf16)
            x = outproj(x, m_l, 2, [jnp.concatenate([mix_p, mix_s], 0), jnp.concatenate([fb_p, fb_s], 0)],
                        [wo[:A_QK], wo[A_QK:]], **geo)
        else:
            wi = odd_w_in[j]
            w_in = jnp.pad(wi, ((0, 0), (0, LANES - MLA_ROPE))).astype(bf16)
            p = normmm(x, m_l, 0, norm_mix[l], w_in, **geo)
            norm_ws = (mla_q_norm[j].reshape(1, -1), mla_kv_norm[j].reshape(1, -1))
            qh, kh = mla_q_headnorm[j], mla_k_headnorm[j]

            def split_hw(hw):
                return (hw[:MLA_NOPE].reshape(1, LANES),
                        jnp.pad(hw[MLA_NOPE:], (0, LANES - MLA_ROPE)).reshape(1, LANES))

            q_ws = (_pad_heads(mla_w_uq[j], MLA_QK, 0, MLA_NOPE).astype(bf16),
                    _pad_heads(mla_w_uq[j], MLA_QK, MLA_NOPE, MLA_QK).astype(bf16)) + split_hw(qh)
            kv_ws = (_pad_heads(mla_w_ukv[j], 2 * LANES, 0, LANES).astype(bf16),
                     _pad_heads(mla_w_ukv[j], 2 * LANES, LANES, 2 * LANES).astype(bf16)) + split_hw(kh)
            qf_p, kf_p, v_p, ckv_p, krp_p = mla_prep((p,), norm_ws, q_ws, kv_ws, None, row_off=0, rows=NP, seq=L)
            qf_s, kf_s, v_s, _, _ = mla_prep((p,), norm_ws, q_ws, kv_ws, (cos_t, sin_t), row_off=NP, rows=NS, seq=LS)
            ckv_c = cache_mla_ckv[:, j].reshape(BS * past, MLA_KVL)
            krp_c = jnp.pad(cache_mla_krope[:, j].reshape(BS * past, MLA_ROPE), ((0, 0), (0, LANES - MLA_ROPE)))
            kf_c, v_c = mla_prep((ckv_c, krp_c), None, None, kv_ws, None, row_off=0, rows=BS * past, seq=past)
            o_p = attend(qf_p, [kf_p], [v_p], batch=B, lq=L, lks=[L])
            o_s = attend(qf_s, [kf_s, kf_c], [v_s, v_c], batch=BS, lq=LS, lks=[LS, past])
            new_ckv.append(ckv_p.reshape(B, L, MLA_KVL))
            new_kr.append(krp_p[:, :MLA_ROPE].reshape(B, L, MLA_ROPE))
            x = outproj(x, m_l, 2, [jnp.concatenate([o_p, o_s], 0)], [odd_w_out[j].astype(bf16)], **geo)
        q, h = normmm(x, m_l, 3, norm_ffn[l], peer_w_q[l], with_h=True, **geo)
        codes, gate = peer_topk(q, peer_sub_keys[l])
        wt = peer_u_gather(codes, gate, h, pack_expert_table(peer_u[l]))
        x = peer_v_gather(codes, wt, x, m_l, pack_expert_table(peer_v[l]), **geo)

    y_prompt = x[:NP].reshape(B, L, D)
    y_sample = x[NP:].reshape(BS, LS, D)
    return (y_prompt, y_sample, jnp.stack(new_gdn, 1), jnp.stack(new_ckv, 1), jnp.stack(new_kr, 1))
```

```python
import functools
import math

import jax
import jax.numpy as jnp
from jax import lax
from jax.experimental import pallas as pl
from jax.experimental.pallas import tpu as pltpu

f32 = jnp.float32
bf16 = jnp.bfloat16
i32 = jnp.int32

LANES = 128
SUBLANES = 8
VMEM_LIMIT = 56 * 1024 * 1024

EPS = 1e-6
D = 1024
GRID_W = 64
GDN_H = 4
GDN_DK = 128
GDN_CHUNK = 64
A_QK = GDN_H * GDN_DK
FN_W = 512
MLA_H = 8
MLA_QL = 512
MLA_KVL = 256
MLA_NOPE = 128
MLA_ROPE = 64
MLA_QK = MLA_NOPE + MLA_ROPE
ROPE_BASE = 10000.0
PEER_H = 8
PEER_NK = 128
PEER_K = 16
PEER_SEL = PEER_H * PEER_K
HALF = SUBLANES // 2
HI_MASK = -65536
PACK_TE = 256
GATHER_TB = 128
U_SLOTS = 16
U_GROUPS = PEER_SEL // U_SLOTS

HI = lax.Precision.HIGHEST
NT = (((1,), (1,)), ((), ()))
TN = (((0,), (0,)), ((), ()))
NN = (((1,), (0,)), ((), ()))


def _dot(a, b, dims=NN, precision=None):
    return lax.dot_general(a, b, dims, preferred_element_type=f32, precision=precision)


def _split(a):
    hi = a.astype(bf16)
    return hi, (a - hi.astype(f32)).astype(bf16)


def _split_mm(lhs_list, w):
    rows = lhs_list[0].shape[0]
    n = len(lhs_list)
    w_hi, w_lo = _split(w)
    parts = [_split(l) for l in lhs_list]
    his = [p[0] for p in parts]
    top = _dot(jnp.concatenate(his + [p[1] for p in parts], axis=0), w_hi)
    bot = _dot(jnp.concatenate(his, axis=0), w_lo) if n > 1 else _dot(his[0], w_lo)
    return [top[i * rows:(i + 1) * rows] + top[(n + i) * rows:(n + i + 1) * rows] + bot[i * rows:(i + 1) * rows]
            for i in range(n)]


def _cp(*sem):
    return pltpu.CompilerParams(dimension_semantics=sem, vmem_limit_bytes=VMEM_LIMIT)


def _mod_row(i, prompt_tiles, tiles_per_seq):
    return jnp.where(i < prompt_tiles, 0, 1 + (i - prompt_tiles) // tiles_per_seq)


def _mods_body(c_ref, w_ref, b_ref, o_ref):
    c = c_ref[...]
    o_ref[...] = _dot(c * jax.nn.sigmoid(c), w_ref[...], precision=HI) + b_ref[...]


def ada_mods_all(cv, w_mod, b_mod):
    depth, _, n6 = w_mod.shape
    r = cv.shape[0]
    tn = 1536
    return pl.pallas_call(
        _mods_body,
        grid=(depth, n6 // tn),
        in_specs=[pl.BlockSpec((r, D), lambda l, j: (0, 0)),
                  pl.BlockSpec((None, D, tn), lambda l, j: (l, 0, j)),
                  pl.BlockSpec((None, 1, tn), lambda l, j: (l, 0, j))],
        out_specs=pl.BlockSpec((None, r, tn), lambda l, j: (l, 0, j)),
        out_shape=jax.ShapeDtypeStruct((depth, r, n6), f32),
        compiler_params=_cp("arbitrary", "arbitrary"),
        name="ada_mods",
    )(cv, w_mod, b_mod.reshape(depth, 1, n6))


def _normmm_body(x_ref, m_ref, nw_ref, w_ref, *rest, shift_idx, split):
    if split:
        wlo_ref, rest = rest[0], rest[1:]
    o_ref, h_refs = rest[0], rest[1:]
    x = x_ref[...]
    y = x * lax.rsqrt(jnp.mean(x * x, -1, keepdims=True) + EPS) * nw_ref[...]
    h = y * (1.0 + m_ref[shift_idx + 1:shift_idx + 2, :]) + m_ref[shift_idx:shift_idx + 1, :]
    hb = h.astype(bf16)
    acc = _dot(hb, w_ref[...])
    if split:
        acc = acc + _dot((h - hb.astype(f32)).astype(bf16), w_ref[...]) + _dot(hb, wlo_ref[...])
    o_ref[...] = acc
    if h_refs:
        h_refs[0][...] = h


def normmm(x, mods_l, shift_idx, norm_w, w, *, prompt_rows, seq_len, tm=256, with_h=False):
    n = x.shape[0]
    nout = w.shape[1]
    split = w.dtype == f32
    ws = [w]
    if split:
        w_hi = w.astype(bf16)
        ws = [w_hi, (w - w_hi.astype(f32)).astype(bf16)]
    pt, tps = prompt_rows // tm, seq_len // tm
    out_shape = [jax.ShapeDtypeStruct((n, nout), f32)]
    out_specs = [pl.BlockSpec((tm, nout), lambda i: (i, 0))]
    if with_h:
        out_shape.append(jax.ShapeDtypeStruct((n, D), f32))
        out_specs.append(pl.BlockSpec((tm, D), lambda i: (i, 0)))
    res = pl.pallas_call(
        functools.partial(_normmm_body, shift_idx=shift_idx, split=split),
        grid=(n // tm,),
        in_specs=[pl.BlockSpec((tm, D), lambda i: (i, 0)),
                  pl.BlockSpec((None, 6, D), lambda i: (_mod_row(i, pt, tps), 0, 0)),
                  pl.BlockSpec((1, D), lambda i: (0, 0))] + [pl.BlockSpec((D, nout), lambda i: (0, 0)) for _ in ws],
        out_specs=out_specs,
        out_shape=out_shape,
        compiler_params=_cp("arbitrary"),
        name="normmm",
    )(x, mods_l, norm_w.reshape(1, D), *ws)
    return res if with_h else res[0]


def _outproj_body(*refs, n_in, gate_idx):
    x_ref, m_ref = refs[0], refs[1]
    a_refs = refs[2:2 + n_in]
    w_refs = refs[2 + n_in:2 + 2 * n_in]
    o_ref = refs[2 + 2 * n_in]
    y = _dot(a_refs[0][...].astype(bf16), w_refs[0][...])
    for a_ref, w_ref in zip(a_refs[1:], w_refs[1:]):
        y = y + _dot(a_ref[...].astype(bf16), w_ref[...])
    o_ref[...] = x_ref[...] + m_ref[gate_idx:gate_idx + 1, :] * y


def outproj(x, mods_l, gate_idx, acts, ws, *, prompt_rows, seq_len, tm=256):
    n = x.shape[0]
    pt, tps = prompt_rows // tm, seq_len // tm
    in_specs = [pl.BlockSpec((tm, D), lambda i: (i, 0)),
                pl.BlockSpec((None, 6, D), lambda i: (_mod_row(i, pt, tps), 0, 0))]
    in_specs += [pl.BlockSpec((tm, a.shape[1]), lambda i: (i, 0)) for a in acts]
    in_specs += [pl.BlockSpec(w.shape, lambda i: (0, 0)) for w in ws]
    return pl.pallas_call(
        functools.partial(_outproj_body, n_in=len(acts), gate_idx=gate_idx),
        grid=(n // tm,),
        in_specs=in_specs,
        out_specs=pl.BlockSpec((tm, D), lambda i: (i, 0)),
        out_shape=jax.ShapeDtypeStruct((n, D), f32),
        compiler_params=_cp("arbitrary"),
        name="outproj",
    )(x, mods_l, *acts, *ws)


def _softplus(x):
    return jnp.maximum(x, 0.0) + jnp.log1p(jnp.exp(-jnp.abs(x)))


def _gdn_body(nega_ref, dt_ref, q_ref, k_ref, v_ref, gate_ref, ba_ref, cwq_ref, cwk_ref, cwv_ref, onw_ref,
              *rest, seq, has_s0):
    if has_s0:
        s0_ref, rest = rest[0], rest[1:]
    out_ref, sout_ref, qs, ks, vs, us, ws, qes, kes, qks, egs = rest
    h = pl.program_id(1)
    C = GDN_CHUNK
    nch = seq // C

    row = lax.broadcasted_iota(i32, (seq, LANES), 0)
    first = row == 0
    last = row == seq - 1

    def conv_silu(x_ref, w_ref):
        x = x_ref[...]
        w = w_ref[...]
        prev = jnp.where(first, 0.0, pltpu.roll(x, 1, 0))
        nxt = jnp.where(last, 0.0, pltpu.roll(x, seq - 1, 0))
        y = prev * w[0:1, :] + x * w[1:2, :] + nxt * w[2:3, :]
        return y * jax.nn.sigmoid(y)

    def l2n(x):
        return x * lax.rsqrt(jnp.sum(x * x, -1, keepdims=True) + EPS)

    qs[...] = l2n(conv_silu(q_ref, cwq_ref)) * (GDN_DK ** -0.5)
    ks[...] = l2n(conv_silu(k_ref, cwk_ref))
    vs[...] = conv_silu(v_ref, cwv_ref)

    ii = lax.broadcasted_iota(i32, (C, C), 0)
    jj = lax.broadcasted_iota(i32, (C, C), 1)
    eye = (ii == jj).astype(f32)

    row_c = lax.broadcasted_iota(i32, (C, LANES), 0)
    NB = 4
    same_block = (lax.broadcasted_iota(i32, (NB * C, NB * C), 0) // C
                  == lax.broadcasted_iota(i32, (NB * C, NB * C), 1) // C)

    def block_diag(m_cat):
        return jnp.where(same_block, jnp.concatenate([m_cat] * NB, axis=0), 0.0)

    def prep_pair(cp, carry):
        a_blocks, rhs_blocks, dest = [], [], []
        for cc in range(2):
            r0 = pl.multiple_of((2 * cp + cc) * C, C)
            qc = qs[pl.ds(r0, C), :]
            kc = ks[pl.ds(r0, C), :]
            vc = vs[pl.ds(r0, C), :]
            ba = ba_ref[pl.ds(r0, C), :]
            betas = [jax.nn.sigmoid(ba[:, d:d + 1]) for d in range(2)]
            kbs = [kc * b for b in betas]
            aq = _dot(jnp.concatenate(kbs + [qc], axis=0).astype(bf16), kc.astype(bf16), NT)
            for d in range(2):
                rel = (ii - jj) if d == 0 else (jj - ii)
                m_incl = rel >= 0
                g = nega_ref[d, h] * _softplus(ba[:, 2 + d:3 + d] + dt_ref[d, h])
                gcB = jnp.broadcast_to(g, (C, LANES))
                for sh in (1, 2, 4, 8, 16, 32):
                    if d == 0:
                        gcB = gcB + jnp.where(row_c >= sh, pltpu.roll(gcB, sh, 0), 0.0)
                    else:
                        gcB = gcB + jnp.where(row_c < C - sh, pltpu.roll(gcB, C - sh, 0), 0.0)
                gc = gcB[:, 0:1]
                gc_row = gcB.T[:C, :]
                decay = jnp.exp(jnp.where(m_incl, gcB[:, :C] - gc_row, -jnp.inf))
                a_blocks.append(jnp.where(rel > 0, aq[d * C:(d + 1) * C] * decay, 0.0))
                qks[d, pl.ds(r0, C), 0:C] = jnp.where(m_incl, aq[2 * C:] * decay, 0.0)
                rhs_blocks.append(jnp.concatenate([vc * betas[d], kbs[d] * jnp.exp(gc)], axis=1))
                dest.append((d, r0))
                qes[d, pl.ds(r0, C), :] = qc * jnp.exp(gc)
                g_last = gcB[C - 1:C, :] if d == 0 else gcB[0:1, :]
                kes[d, pl.ds(r0, C), :] = kc * jnp.exp(g_last - gcB)
                egs[d, pl.ds(pl.multiple_of((2 * cp + cc) * SUBLANES, SUBLANES), SUBLANES), :] = jnp.broadcast_to(
                    jnp.exp(g_last), (SUBLANES, LANES))
        A = jnp.concatenate(a_blocks, axis=1)
        X = jnp.concatenate([eye] * NB, axis=1) - A
        (P,) = _split_mm([A], block_diag(A))
        for _ in range(4):
            xp, P = _split_mm([X, P], block_diag(P))
            X = X + xp
        X = X + _split_mm([X], block_diag(P))[0]
        for k, (d, r0) in enumerate(dest):
            (sol,) = _split_mm([X[:, k * C:(k + 1) * C]], rhs_blocks[k])
            us[d, pl.ds(r0, C), :] = sol[:, :GDN_DK]
            ws[d, pl.ds(r0, C), :] = sol[:, GDN_DK:]
        return carry

    lax.fori_loop(0, nch // 2, prep_pair, 0)

    def advance(d, c, S):
        r0 = pl.multiple_of(c * C, C)
        wq = _dot(jnp.concatenate([ws[d, pl.ds(r0, C), :], qes[d, pl.ds(r0, C), :]], axis=0).astype(bf16),
                  S.astype(bf16))
        v_new = us[d, pl.ds(r0, C), :] - wq[:C]
        vb = v_new.astype(bf16)
        us[d, pl.ds(r0, C), :] = wq[C:] + _dot(qks[d, pl.ds(r0, C), 0:C].astype(bf16), vb)
        eg = egs[d, pl.ds(pl.multiple_of(c * SUBLANES, SUBLANES), 1), :]
        return S * eg + _dot(kes[d, pl.ds(r0, C), :].astype(bf16), vb, TN)

    def step(i, carry):
        return advance(0, i, carry[0]), advance(1, nch - 1 - i, carry[1])

    if has_s0:
        S0 = (s0_ref[0], s0_ref[1])
    else:
        S0 = (jnp.zeros((GDN_DK, GDN_DK), f32),) * 2
    Sf, Sb = lax.fori_loop(0, nch, step, S0)
    sout_ref[0] = Sf
    sout_ref[1] = Sb

    o = us[0] + us[1]
    y = o * lax.rsqrt(jnp.mean(o * o, -1, keepdims=True) + EPS) * onw_ref[...]
    gt = gate_ref[...]
    out_ref[...] = y * (gt * jax.nn.sigmoid(gt))


def gdn(p, nega, dtb, conv_w, onw, s0, *, row_off, batch, seq, ba_col0):
    rb = row_off // seq
    H = GDN_H
    nch = seq // GDN_CHUNK

    def col(cb):
        return pl.BlockSpec((seq, LANES), lambda b, h: (rb + b, cb + h), pipeline_mode=pl.Buffered(1))

    state_spec = pl.BlockSpec((None, 2, None, GDN_DK, GDN_DK), lambda b, h: (b, 0, h, 0, 0))
    in_specs = [pl.BlockSpec(memory_space=pltpu.SMEM), pl.BlockSpec(memory_space=pltpu.SMEM),
                col(0), col(H), col(2 * H), col(3 * H), col(ba_col0),
                pl.BlockSpec((3, LANES), lambda b, h: (0, h)),
                pl.BlockSpec((3, LANES), lambda b, h: (0, H + h)),
                pl.BlockSpec((3, LANES), lambda b, h: (0, 2 * H + h)),
                pl.BlockSpec((1, LANES), lambda b, h: (0, 0))]
    args = [nega, dtb, p, p, p, p, p, conv_w, conv_w, conv_w, onw.reshape(1, LANES)]
    if s0 is not None:
        in_specs.append(state_spec)
        args.append(s0)
    seq_buf = pltpu.VMEM((seq, LANES), f32)
    dir_buf = pltpu.VMEM((2, seq, LANES), f32)
    return pl.pallas_call(
        functools.partial(_gdn_body, seq=seq, has_s0=s0 is not None),
        grid=(batch, H),
        in_specs=in_specs,
        out_specs=[pl.BlockSpec((seq, LANES), lambda b, h: (b, h)), state_spec],
        out_shape=[jax.ShapeDtypeStruct((batch * seq, H * LANES), f32),
                   jax.ShapeDtypeStruct((batch, 2, H, GDN_DK, GDN_DK), f32)],
        scratch_shapes=[seq_buf, seq_buf, seq_buf, dir_buf, dir_buf, dir_buf, dir_buf, dir_buf,
                        pltpu.VMEM((2, nch * SUBLANES, LANES), f32)],
        compiler_params=_cp("arbitrary", "arbitrary"),
        name="gdn",
    )(*args)


def _fnet_body(x_ref, cd_ref, f_ref, o_ref, z_scr, *, seq):
    r = pl.program_id(1)

    @pl.when(r == 0)
    def _stage1():
        for g in range(FN_W // LANES):
            xg = x_ref[:, g * LANES:(g + 1) * LANES].astype(bf16)
            y = _dot(xg, cd_ref[...])
            z_scr[0:seq, g * LANES:(g + 1) * LANES] = y[:, :LANES].astype(bf16)
            z_scr[seq:2 * seq, g * LANES:(g + 1) * LANES] = y[:, LANES:].astype(bf16)

    scale = 1.0 / math.sqrt(seq * LANES)
    o_ref[...] = _dot(f_ref[...], z_scr[...]) * scale


def fnet(p, cd, fmat, *, row_off, batch, seq, col_block, tr):
    rb = row_off // seq
    return pl.pallas_call(
        functools.partial(_fnet_body, seq=seq),
        grid=(batch, seq // tr),
        in_specs=[pl.BlockSpec((seq, FN_W), lambda b, r: (rb + b, col_block)),
                  pl.BlockSpec((LANES, 2 * LANES), lambda b, r: (0, 0)),
                  pl.BlockSpec((tr, 2 * seq), lambda b, r: (r, 0))],
        out_specs=pl.BlockSpec((tr, FN_W), lambda b, r: (b * (seq // tr) + r, 0)),
        out_shape=jax.ShapeDtypeStruct((batch * seq, FN_W), f32),
        scratch_shapes=[pltpu.VMEM((2 * seq, FN_W), bf16)],
        compiler_params=_cp("arbitrary", "arbitrary"),
        name="fnet",
    )(p, cd, fmat)


def _dft_mats(n):
    k = jnp.arange(n, dtype=i32)
    ang = ((k[:, None] * k[None, :]) % n).astype(f32) * (2.0 * math.pi / n)
    return jnp.cos(ang), jnp.sin(ang)


def _rope(x, cos, sin_signed, first_half):
    partner = jnp.where(first_half, pltpu.roll(x, LANES - MLA_ROPE // 4, 1), pltpu.roll(x, MLA_ROPE // 4, 1))
    return x * cos + partner * sin_signed


def _mla_body(*refs, has_q, normalize, use_rope):
    it = iter(refs)
    if normalize:
        p_ref = next(it)
        qnw_ref = next(it)
        kvnw_ref = next(it)
    else:
        ckv_ref = next(it)
        krp_ref = next(it)
    if has_q:
        wqn_ref, wqr_ref, qhn_ref, qhr_ref = next(it), next(it), next(it), next(it)
    wkn_ref, wv_ref, khn_ref, khr_ref = next(it), next(it), next(it), next(it)
    if use_rope:
        cos_ref, sin_ref = next(it), next(it)
    if has_q:
        qf_ref = next(it)
    kf_ref, v_ref = next(it), next(it)
    if normalize:
        ckv_out, krp_out = next(it), next(it)

    def rms(x):
        return x * lax.rsqrt(jnp.mean(x * x, -1, keepdims=True) + EPS)

    if normalize:
        p = p_ref[...]
        cq = rms(p[:, :MLA_QL]) * qnw_ref[...]
        ckv = rms(p[:, MLA_QL:MLA_QL + MLA_KVL]) * kvnw_ref[...]
        krp = p[:, MLA_QL + MLA_KVL:]
        ckv_out[...] = ckv
        krp_out[...] = krp
    else:
        ckv = ckv_ref[...]
        krp = krp_ref[...]
    if use_rope:
        cos = cos_ref[...]
        sin = sin_ref[...]
        lane = lax.broadcasted_iota(i32, cos.shape, 1)
        first_half = (lane % (MLA_ROPE // 2)) < (MLA_ROPE // 4)

    def head_norm(a, b, wn, wr, scale):
        ss = jnp.sum(a * a, -1, keepdims=True) + jnp.sum(b * b, -1, keepdims=True)
        r = lax.rsqrt(ss * (1.0 / MLA_QK) + EPS)
        a = a * r * wn
        b = b * r * wr
        if use_rope:
            b = _rope(b, cos, sin, first_half)
        return (a * scale).astype(bf16), (b * scale).astype(bf16)

    if has_q:
        cqb = cq.astype(bf16)
        qn = _dot(cqb, wqn_ref[...])
        qr = _dot(cqb, wqr_ref[...])
        for h in range(MLA_H):
            a, b = head_norm(qn[:, h * LANES:(h + 1) * LANES], qr[:, h * LANES:(h + 1) * LANES],
                             qhn_ref[...], qhr_ref[...], MLA_QK ** -0.5)
            qf_ref[:, 2 * h * LANES:(2 * h + 1) * LANES] = a
            qf_ref[:, (2 * h + 1) * LANES:(2 * h + 2) * LANES] = b
    ckvb = ckv.astype(bf16)
    kn = _dot(ckvb, wkn_ref[...])
    v_ref[...] = _dot(ckvb, wv_ref[...]).astype(bf16)
    for h in range(MLA_H):
        a, b = head_norm(kn[:, h * LANES:(h + 1) * LANES], krp, khn_ref[...], khr_ref[...], 1.0)
        kf_ref[:, 2 * h * LANES:(2 * h + 1) * LANES] = a
        kf_ref[:, (2 * h + 1) * LANES:(2 * h + 2) * LANES] = b


def mla_prep(srcs, norm_ws, q_ws, kv_ws, rope_tabs, *, row_off, rows, seq, tm=256):
    normalize = norm_ws is not None
    has_q = q_ws is not None
    use_rope = rope_tabs is not None
    ro = row_off // tm
    nt = rows // tm
    hd = MLA_H * 2 * LANES

    def full(a):
        return pl.BlockSpec(a.shape, lambda i: (0,) * a.ndim)

    args, in_specs = [], []
    for s in srcs:
        args.append(s)
        in_specs.append(pl.BlockSpec((tm, s.shape[1]), lambda i: (ro + i, 0)))
    for group in (norm_ws, q_ws, kv_ws):
        if group is not None:
            for a in group:
                args.append(a)
                in_specs.append(full(a))
    if use_rope:
        tps = seq // tm
        for a in rope_tabs:
            args.append(a)
            in_specs.append(pl.BlockSpec((tm, LANES), lambda i: (i % tps, 0)))
    out_shape, out_specs = [], []

    def out(cols, dt):
        out_shape.append(jax.ShapeDtypeStruct((rows, cols), dt))
        out_specs.append(pl.BlockSpec((tm, cols), lambda i: (i, 0)))

    if has_q:
        out(hd, bf16)
    out(hd, bf16)
    out(MLA_H * LANES, bf16)
    if normalize:
        out(MLA_KVL, f32)
        out(LANES, f32)
    return pl.pallas_call(
        functools.partial(_mla_body, has_q=has_q, normalize=normalize, use_rope=use_rope),
        grid=(nt,),
        in_specs=in_specs,
        out_specs=out_specs,
        out_shape=out_shape,
        compiler_params=_cp("arbitrary"),
        name="mla_prep",
    )(*args)


def _attn_body(*refs, n_kv):
    q_ref = refs[0]
    k_refs = refs[1:1 + n_kv]
    v_refs = refs[1 + n_kv:1 + 2 * n_kv]
    o_ref = refs[1 + 2 * n_kv]
    q = q_ref[...]
    ss = [_dot(q, k_ref[...], NT) for k_ref in k_refs]
    m = jnp.max(ss[0], -1, keepdims=True)
    for s in ss[1:]:
        m = jnp.maximum(m, jnp.max(s, -1, keepdims=True))
    l = None
    acc = None
    for s, v_ref in zip(ss, v_refs):
        e = jnp.exp(s - m)
        ls = jnp.sum(e, -1, keepdims=True)
        l = ls if l is None else l + ls
        pv = _dot(e.astype(bf16), v_ref[...])
        acc = pv if acc is None else acc + pv
    o_ref[...] = acc / l


def attend(qf, kfs, vs, *, batch, lq, lks, tq=256):
    n_kv = len(kfs)
    nq = lq // tq
    in_specs = [pl.BlockSpec((tq, 2 * LANES), lambda b, h, i: (b * nq + i, h))]
    in_specs += [pl.BlockSpec((lk, 2 * LANES), lambda b, h, i: (b, h)) for lk in lks]
    in_specs += [pl.BlockSpec((lk, LANES), lambda b, h, i: (b, h)) for lk in lks]
    return pl.pallas_call(
        functools.partial(_attn_body, n_kv=n_kv),
        grid=(batch, MLA_H, nq),
        in_specs=in_specs,
        out_specs=pl.BlockSpec((tq, LANES), lambda b, h, i: (b * nq + i, h)),
        out_shape=jax.ShapeDtypeStruct((batch * lq, MLA_H * LANES), f32),
        compiler_params=_cp("arbitrary", "arbitrary", "arbitrary"),
        name="attend",
    )(qf, *kfs, *vs)


def _topk_rows(s, k, val_scr, idx_scr, payload=None):
    rows = s.shape[0]
    iota = lax.broadcasted_iota(i32, s.shape, 0)
    for r in range(k):
        m = jnp.max(s, axis=0, keepdims=True)
        idx = jnp.min(jnp.where(s == m, iota, rows), axis=0, keepdims=True)
        hit = iota == idx
        val_scr[r:r + 1, :] = m
        if payload is None:
            idx_scr[r:r + 1, :] = idx
        else:
            idx_scr[r:r + 1, :] = jnp.max(jnp.where(hit, payload, -1), axis=0, keepdims=True)
        s = jnp.where(hit, -jnp.inf, s)


def _pair_candidates(sv0, sv1, si0, si1):
    K = PEER_K
    sub = lax.broadcasted_iota(i32, (SUBLANES, sv0.shape[1]), 0)
    vals, idxs = [], []
    for a in range(SUBLANES):
        nb = K // (a + 1)
        for b0 in range(0, nb, SUBLANES):
            v = sv0[a:a + 1, :] + sv1[b0:b0 + SUBLANES, :]
            e = si0[a:a + 1, :] * PEER_NK + si1[b0:b0 + SUBLANES, :]
            if nb - b0 < SUBLANES:
                v = jnp.where(sub < nb - b0, v, -jnp.inf)
            vals.append(v)
            idxs.append(e)
    vals.append(sv0[SUBLANES:K, :] + sv1[0:1, :])
    idxs.append(si0[SUBLANES:K, :] * PEER_NK + si1[0:1, :])
    return jnp.concatenate(vals, axis=0), jnp.concatenate(idxs, axis=0)


def _u_slot_rows(s):
    return SUBLANES * (s // SUBLANES) + HALF * (s % 2) + (s % SUBLANES) // 2


def _peer_topk_body(q_ref, keys_ref, cu_ref, gate_ref, sv, si, cv, ci, e_all, g_all):
    K = PEER_K
    tt = q_ref.shape[0]
    for h in range(PEER_H):
        for p in range(2):
            qhp = q_ref[:, (2 * h + p) * LANES:(2 * h + p + 1) * LANES]
            s = _dot(keys_ref[h, p], qhp, NT, precision=HI)
            _topk_rows(s, K, sv.at[p], si.at[p])
        cand, cidx = _pair_candidates(sv[0], sv[1], si[0], si[1])
        _topk_rows(cand, K, cv, ci, payload=cidx)
        cs = cv[...]
        e = jnp.exp(cs - cs[0:1, :])
        g_all[h * K:(h + 1) * K, :] = e / jnp.sum(e, axis=0, keepdims=True)
        code = (ci[...] + PACK_TE) * HALF
        for b in range(tt // GATHER_TB):
            e_all[b, h * K:(h + 1) * K, :] = code[:, b * GATHER_TB:(b + 1) * GATHER_TB]
    gate_ref[...] = g_all[...].T
    for b in range(tt // GATHER_TB):
        for s in range(U_SLOTS):
            cu_ref[s, b] = e_all[b, pl.ds(_u_slot_rows(s), U_GROUPS, stride=U_SLOTS), :] - HALF * (s % 2)


def peer_topk(q, sub_keys, *, tt=256):
    n = q.shape[0]
    K = PEER_K
    nb = tt // GATHER_TB
    return pl.pallas_call(
        _peer_topk_body,
        grid=(n // tt,),
        in_specs=[pl.BlockSpec((tt, q.shape[1]), lambda i: (i, 0)),
                  pl.BlockSpec(sub_keys.shape, lambda i: (0, 0, 0, 0))],
        out_specs=[pl.BlockSpec((U_SLOTS, nb, U_GROUPS, GATHER_TB), lambda i: (0, i, 0, 0)),
                   pl.BlockSpec((tt, PEER_SEL), lambda i: (i, 0))],
        out_shape=[jax.ShapeDtypeStruct((U_SLOTS, n // GATHER_TB, U_GROUPS, GATHER_TB), i32),
                   jax.ShapeDtypeStruct((n, PEER_SEL), f32)],
        scratch_shapes=[pltpu.VMEM((2, K, tt), f32), pltpu.VMEM((2, K, tt), i32),
                        pltpu.VMEM((K, tt), f32), pltpu.VMEM((K, tt), i32),
                        pltpu.VMEM((nb, PEER_SEL, GATHER_TB), i32), pltpu.VMEM((PEER_SEL, tt), f32)],
        compiler_params=_cp("arbitrary"),
        name="peer_topk",
    )(q, sub_keys)


def _pack_body(x_ref, o_ref):
    i = pl.program_id(0)
    guard = jnp.logical_or(i == 0, i == pl.num_programs(0) - 1)

    @pl.when(guard)
    def _zero():
        o_ref[...] = jnp.zeros_like(o_ref)

    @pl.when(jnp.logical_not(guard))
    def _pack():
        half = x_ref.shape[1] // 2
        lo = lax.bitcast_convert_type(x_ref[:, :half].astype(bf16).astype(f32), i32)
        hi = lax.bitcast_convert_type(x_ref[:, half:].astype(bf16).astype(f32), i32)
        words = (hi & jnp.int32(HI_MASK)) | lax.shift_right_logical(lo, jnp.int32(16))
        for s in range(HALF):
            o_ref[pl.ds(s, PACK_TE, stride=HALF), :] = words[:, s * LANES:(s + 1) * LANES]


def pack_expert_table(tab):
    e, d = tab.shape
    nb = e // PACK_TE
    return pl.pallas_call(
        _pack_body,
        grid=(nb + 2,),
        in_specs=[pl.BlockSpec((PACK_TE, d), lambda i: (jnp.clip(i - 1, 0, nb - 1), 0))],
        out_specs=pl.BlockSpec((PACK_TE * HALF, LANES), lambda i: (i, 0)),
        out_shape=jax.ShapeDtypeStruct(((nb + 2) * PACK_TE * HALF, LANES), i32),
        compiler_params=_cp("arbitrary"),
        name="pack_table",
    )(tab)


def _merged(tab_ref, ca, cb, lo_half):
    a = tab_ref[pl.ds(pl.multiple_of(ca, HALF), SUBLANES), :]
    b = tab_ref[pl.ds(pl.multiple_of(cb, HALF), SUBLANES), :]
    return jnp.where(lo_half, a, b)


def _lo(words):
    return lax.bitcast_convert_type(words << 16, f32)


def _hi(words):
    return lax.bitcast_convert_type(words & jnp.int32(HI_MASK), f32)


def _gelu_tanh(x):
    return 0.5 * x * (1.0 + jnp.tanh(math.sqrt(2.0 / math.pi) * (x + 0.044715 * (x * x * x))))


U_FIN_UNROLL = 16


def _fold4(ms, sub):
    lo2 = (sub % 4) < 2
    b = [jnp.where(lo2, ms[k] + pltpu.roll(ms[k], SUBLANES - 2, 0), ms[k + 2] + pltpu.roll(ms[k + 2], 2, 0))
         for k in range(2)]
    lo1 = (sub % 2) < 1
    return jnp.where(lo1, b[0] + pltpu.roll(b[0], SUBLANES - 1, 0), b[1] + pltpu.roll(b[1], 1, 0))


def _peer_u_body(*refs, tb):
    c_refs = refs[:U_SLOTS]
    gate_ref, h_ref, tab_ref, wt_ref, z_ref, h8_ref, w_ref = refs[U_SLOTS:]
    sub = lax.broadcasted_iota(i32, (SUBLANES, LANES), 0)
    lo_half = sub < HALF
    for s in range(SUBLANES):
        h8_ref[pl.ds(s, tb, stride=SUBLANES), :] = h_ref[:, s * LANES:(s + 1) * LANES]

    def token(t, carry):
        r8 = pl.multiple_of(t * SUBLANES, SUBLANES)
        hrow = h8_ref[pl.ds(r8, SUBLANES), :]
        hswap = pltpu.roll(hrow, HALF, 0)
        hl = jnp.where(lo_half, hrow, hswap)
        hh = jnp.where(lo_half, hswap, hrow)
        for g in range(U_GROUPS):
            for fold in range(2):
                ms = []
                for k in range(4):
                    s0 = fold * SUBLANES + 2 * k
                    words = _merged(tab_ref, c_refs[s0][g * tb + t], c_refs[s0 + 1][g * tb + t], lo_half)
                    ms.append(_lo(words) * hl + _hi(words) * hh)
                row0 = t * PEER_SEL + (2 * g + fold) * SUBLANES
                z_ref[pl.ds(pl.multiple_of(row0, SUBLANES), SUBLANES), :] = _fold4(ms, sub)
        return carry

    lax.fori_loop(0, tb, token, 0)

    def finish(tg, carry):
        for i in range(U_FIN_UNROLL):
            t = tg * U_FIN_UNROLL + i
            zt = z_ref[pl.ds(pl.multiple_of(t * PEER_SEL, PEER_SEL), PEER_SEL), :]
            w_ref[pl.ds(t, 1), :] = jnp.sum(zt.T, axis=0, keepdims=True)
        return carry

    lax.fori_loop(0, tb // U_FIN_UNROLL, finish, 0)
    wt_ref[...] = (gate_ref[...] * _gelu_tanh(w_ref[...])).T


def peer_u_gather(codes, gate, h, tab):
    tb = GATHER_TB
    n = gate.shape[0]
    codes = codes.reshape(U_SLOTS, n * U_GROUPS)
    smem = [pl.BlockSpec((tb * U_GROUPS,), lambda i: (i,), memory_space=pltpu.SMEM) for _ in range(U_SLOTS)]
    return pl.pallas_call(
        functools.partial(_peer_u_body, tb=tb),
        grid=(n // tb,),
        in_specs=smem + [pl.BlockSpec((tb, PEER_SEL), lambda i: (i, 0)),
                         pl.BlockSpec((tb, D), lambda i: (i, 0)),
                         pl.BlockSpec(tab.shape, lambda i: (0, 0), pipeline_mode=pl.Buffered(1))],
        out_specs=pl.BlockSpec((None, PEER_SEL, tb), lambda i: (i, 0, 0)),
        out_shape=jax.ShapeDtypeStruct((n // tb, PEER_SEL, tb), f32),
        scratch_shapes=[pltpu.VMEM((tb * PEER_SEL, LANES), f32), pltpu.VMEM((tb * SUBLANES, LANES), f32),
                        pltpu.VMEM((tb, PEER_SEL), f32)],
        compiler_params=_cp("arbitrary"),
        name="peer_u",
    )(*[codes[s] for s in range(U_SLOTS)], gate, h, tab)


def _peer_v_body(*refs, tb):
    c_refs = refs[:U_SLOTS]
    wt_ref, x_ref, m_ref, tab_ref, o_ref, y8_ref, wx_ref = refs[U_SLOTS:]
    sub = lax.broadcasted_iota(i32, (SUBLANES, LANES), 0)
    lo_half = sub < HALF

    def expanded(t):
        return jnp.take_along_axis(wt_ref[...], jnp.full((PEER_SEL, tb), t, i32), axis=1)

    wx_ref[...] = expanded(0)

    def token(t, carry):
        nxt = expanded(jnp.minimum(t + 1, tb - 1))
        accs = [jnp.zeros((SUBLANES, LANES), f32) for _ in range(4)]
        for g in range(U_GROUPS):
            for fold in range(2):
                for k in range(4):
                    s0 = fold * SUBLANES + 2 * k
                    ja = g * U_SLOTS + _u_slot_rows(s0)
                    jb = g * U_SLOTS + _u_slot_rows(s0 + 1)
                    words = _merged(tab_ref, c_refs[s0][g * tb + t], c_refs[s0 + 1][g * tb + t], lo_half)
                    wm = jnp.where(lo_half, jnp.broadcast_to(wx_ref[ja:ja + 1, :], (SUBLANES, LANES)),
                                   jnp.broadcast_to(wx_ref[jb:jb + 1, :], (SUBLANES, LANES)))
                    accs[2 * (k % 2)] = accs[2 * (k % 2)] + wm * _lo(words)
                    accs[2 * (k % 2) + 1] = accs[2 * (k % 2) + 1] + wm * _hi(words)
        lo = accs[0] + accs[2]
        hi = accs[1] + accs[3]
        lo = lo + pltpu.roll(lo, HALF, 0)
        hi = hi + pltpu.roll(hi, HALF, 0)
        y8_ref[pl.ds(pl.multiple_of(t * SUBLANES, SUBLANES), SUBLANES), :] = jnp.where(lo_half, lo, hi)
        wx_ref[...] = nxt
        return carry

    lax.fori_loop(0, tb, token, 0)
    for s in range(SUBLANES):
        cols = slice(s * LANES, (s + 1) * LANES)
        o_ref[:, cols] = x_ref[:, cols] + m_ref[5:6, cols] * y8_ref[pl.ds(s, tb, stride=SUBLANES), :]


def peer_v_gather(codes, wt, x, mods_l, tab, *, prompt_rows, seq_len):
    tb = GATHER_TB
    n = x.shape[0]
    pt, tps = prompt_rows // tb, seq_len // tb
    codes = codes.reshape(U_SLOTS, n * U_GROUPS)
    smem = [pl.BlockSpec((tb * U_GROUPS,), lambda i: (i,), memory_space=pltpu.SMEM) for _ in range(U_SLOTS)]
    return pl.pallas_call(
        functools.partial(_peer_v_body, tb=tb),
        grid=(n // tb,),
        in_specs=smem + [pl.BlockSpec((None, PEER_SEL, tb), lambda i: (i, 0, 0)),
                         pl.BlockSpec((tb, D), lambda i: (i, 0)),
                         pl.BlockSpec((None, 6, D), lambda i: (_mod_row(i, pt, tps), 0, 0)),
                         pl.BlockSpec(tab.shape, lambda i: (0, 0), pipeline_mode=pl.Buffered(1))],
        out_specs=pl.BlockSpec((tb, D), lambda i: (i, 0)),
        out_shape=jax.ShapeDtypeStruct((n, D), f32),
        scratch_shapes=[pltpu.VMEM((tb * SUBLANES, LANES), f32), pltpu.VMEM((PEER_SEL, LANES), f32)],
        compiler_params=_cp("arbitrary"),
        name="peer_v",
    )(*[codes[s] for s in range(U_SLOTS)], wt, x, mods_l, tab)


def _rope_tables(seq):
    rows = seq // GRID_W
    row = jnp.repeat(jnp.arange(rows, dtype=f32), GRID_W)
    col = jnp.tile(jnp.arange(GRID_W, dtype=f32), rows)
    nfreq = MLA_ROPE // 4
    inv = jnp.power(ROPE_BASE, -jnp.arange(nfreq, dtype=f32) / nfreq)
    ar = row[:, None] * inv
    ac = col[:, None] * inv
    pad1 = jnp.ones((seq, LANES - MLA_ROPE), f32)
    pad0 = jnp.zeros((seq, LANES - MLA_ROPE), f32)
    cos = jnp.concatenate([jnp.cos(ar), jnp.cos(ar), jnp.cos(ac), jnp.cos(ac), pad1], -1)
    sin = jnp.concatenate([-jnp.sin(ar), jnp.sin(ar), -jnp.sin(ac), jnp.sin(ac), pad0], -1)
    return cos, sin


def _pad_heads(w, head_w, lo, hi):
    k = w.shape[0]
    w = w.reshape(k, -1, head_w)[:, :, lo:hi]
    return jnp.pad(w, ((0, 0), (0, 0), (0, LANES - (hi - lo)))).reshape(k, -1)


def kernel(x_prompt, x_sample, state_gdn, cache_mla_ckv, cache_mla_krope, c, c_ctx, w_mod, b_mod, norm_mix, norm_ffn, even_w_in, even_conv_w, gdn_a_log, gdn_dt_bias, gdn_o_norm, even_w_out, odd_w_in, mla_q_norm, mla_kv_norm, mla_w_uq, mla_w_ukv, mla_q_headnorm, mla_k_headnorm, odd_w_out, peer_w_q, peer_sub_keys, peer_u, peer_v):
    B, L, _ = x_prompt.shape
    BS, LS, _ = x_sample.shape
    depth = w_mod.shape[0]
    NP, NS = B * L, BS * LS
    past = cache_mla_ckv.shape[2]
    geo = dict(prompt_rows=NP, seq_len=LS)

    x = jnp.concatenate([x_prompt.reshape(NP, D), x_sample.reshape(NS, D)], 0)
    nrow = 1 + BS
    rpad = -nrow % SUBLANES
    cv = jnp.concatenate([c_ctx[None, :], c, jnp.zeros((rpad, D), f32)], 0)
    mods = ada_mods_all(cv, w_mod, b_mod).reshape(depth, nrow + rpad, 6, D)

    cd_c, cd_s = _dft_mats(LANES)
    cd = jnp.concatenate([cd_c, cd_s], 1).astype(bf16)
    fmats = {}
    for n in (L, LS):
        cl, sl = _dft_mats(n)
        fmats[n] = jnp.concatenate([cl, -sl], 1).astype(bf16)
    cos_t, sin_t = _rope_tables(LS)

    new_gdn, new_ckv, new_kr = [], [], []
    for l in range(depth):
        j = l // 2
        m_l = mods[l]
        if l % 2 == 0:
            wi = even_w_in[j]
            o0 = 3 * A_QK + A_QK
            ba = wi[:, o0:o0 + 4 * GDN_H].reshape(D, 4, GDN_H).transpose(0, 2, 1)
            ba = jnp.pad(ba, ((0, 0), (0, 0), (0, LANES - 4))).reshape(D, GDN_H * LANES)
            w_in = jnp.concatenate([wi[:, :o0], wi[:, o0 + 4 * GDN_H:], ba], 1).astype(bf16)
            p = normmm(x, m_l, 0, norm_mix[l], w_in, **geo)
            nega = -jnp.exp(gdn_a_log[j])
            kw = dict(ba_col0=(o0 + FN_W) // LANES)
            mix_p, st = gdn(p, nega, gdn_dt_bias[j], even_conv_w[j], gdn_o_norm[j], None,
                            row_off=0, batch=B, seq=L, **kw)
            mix_s, _ = gdn(p, nega, gdn_dt_bias[j], even_conv_w[j], gdn_o_norm[j], state_gdn[:, j],
                           row_off=NP, batch=BS, seq=LS, **kw)
            new_gdn.append(st)
            fb_p = fnet(p, cd, fmats[L], row_off=0, batch=B, seq=L, col_block=o0 // FN_W, tr=L)
            fb_s = fnet(p, cd, fmats[LS], row_off=NP, batch=BS, seq=LS, col_block=o0 // FN_W, tr=256)
            wo = even_w_out[j].astype(bf16)
            x = outproj(x, m_l, 2, [jnp.concatenate([mix_p, mix_s], 0), jnp.concatenate([fb_p, fb_s], 0)],
                        [wo[:A_QK], wo[A_QK:]], **geo)
        else:
            wi = odd_w_in[j]
            w_in = jnp.pad(wi, ((0, 0), (0, LANES - MLA_ROPE))).astype(bf16)
            p = normmm(x, m_l, 0, norm_mix[l], w_in, **geo)
            norm_ws = (mla_q_norm[j].reshape(1, -1), mla_kv_norm[j].reshape(1, -1))
            qh, kh = mla_q_headnorm[j], mla_k_headnorm[j]

            def split_hw(hw):
                return (hw[:MLA_NOPE].reshape(1, LANES),
                        jnp.pad(hw[MLA_NOPE:], (0, LANES - MLA_ROPE)).reshape(1, LANES))

            q_ws = (_pad_heads(mla_w_uq[j], MLA_QK, 0, MLA_NOPE).astype(bf16),
                    _pad_heads(mla_w_uq[j], MLA_QK, MLA_NOPE, MLA_QK).astype(bf16)) + split_hw(qh)
            kv_ws = (_pad_heads(mla_w_ukv[j], 2 * LANES, 0, LANES).astype(bf16),
                     _pad_heads(mla_w_ukv[j], 2 * LANES, LANES, 2 * LANES).astype(bf16)) + split_hw(kh)
            qf_p, kf_p, v_p, ckv_p, krp_p = mla_prep((p,), norm_ws, q_ws, kv_ws, None, row_off=0, rows=NP, seq=L)
            qf_s, kf_s, v_s, _, _ = mla_prep((p,), norm_ws, q_ws, kv_ws, (cos_t, sin_t), row_off=NP, rows=NS, seq=LS)
            ckv_c = cache_mla_ckv[:, j].reshape(BS * past, MLA_KVL)
            krp_c = jnp.pad(cache_mla_krope[:, j].reshape(BS * past, MLA_ROPE), ((0, 0), (0, LANES - MLA_ROPE)))
            kf_c, v_c = mla_prep((ckv_c, krp_c), None, None, kv_ws, None, row_off=0, rows=BS * past, seq=past)
            o_p = attend(qf_p, [kf_p], [v_p], batch=B, lq=L, lks=[L])
            o_s = attend(qf_s, [kf_s, kf_c], [v_s, v_c], batch=BS, lq=LS, lks=[LS, past])
            new_ckv.append(ckv_p.reshape(B, L, MLA_KVL))
            new_kr.append(krp_p[:, :MLA_ROPE].reshape(B, L, MLA_ROPE))
            x = outproj(x, m_l, 2, [jnp.concatenate([o_p, o_s], 0)], [odd_w_out[j].astype(bf16)], **geo)
        q, h = normmm(x, m_l, 3, norm_ffn[l], peer_w_q[l], with_h=True, **geo)
        codes, gate = peer_topk(q, peer_sub_keys[l])
        wt = peer_u_gather(codes, gate, h, pack_expert_table(peer_u[l]))
        x = peer_v_gather(codes, wt, x, m_l, pack_expert_table(peer_v[l]), **geo)

    y_prompt = x[:NP].reshape(B, L, D)
    y_sample = x[NP:].reshape(BS, LS, D)
    return (y_prompt, y_sample, jnp.stack(new_gdn, 1), jnp.stack(new_ckv, 1), jnp.stack(new_kr, 1))
```

```python
import functools
import math

import jax
import jax.numpy as jnp
from jax import lax
from jax.experimental import pallas as pl
from jax.experimental.pallas import tpu as pltpu

f32 = jnp.float32
bf16 = jnp.bfloat16
i32 = jnp.int32

LANES = 128
SUBLANES = 8
VMEM_LIMIT = 56 * 1024 * 1024

EPS = 1e-6
D = 1024
GRID_W = 64
GDN_H = 4
GDN_DK = 128
GDN_CHUNK = 64
A_QK = GDN_H * GDN_DK
FN_W = 512
MLA_H = 8
MLA_QL = 512
MLA_KVL = 256
MLA_NOPE = 128
MLA_ROPE = 64
MLA_QK = MLA_NOPE + MLA_ROPE
ROPE_BASE = 10000.0
PEER_H = 8
PEER_NK = 128
PEER_K = 16
PEER_SEL = PEER_H * PEER_K
HALF = SUBLANES // 2
HI_MASK = -65536
PACK_TE = 256
GATHER_TB = 128
U_SLOTS = 16
U_GROUPS = PEER_SEL // U_SLOTS

HI = lax.Precision.HIGHEST
NT = (((1,), (1,)), ((), ()))
TN = (((0,), (0,)), ((), ()))
NN = (((1,), (0,)), ((), ()))


def _dot(a, b, dims=NN, precision=None):
    return lax.dot_general(a, b, dims, preferred_element_type=f32, precision=precision)


def _split(a):
    hi = a.astype(bf16)
    return hi, (a - hi.astype(f32)).astype(bf16)


def _split_mm(parts, w_hi, w_lo):
    rows = parts[0][0].shape[0]
    n = len(parts)
    his = [p[0] for p in parts]
    top = _dot(jnp.concatenate(his + [p[1] for p in parts], axis=0), w_hi)
    bot = _dot(jnp.concatenate(his, axis=0), w_lo) if n > 1 else _dot(his[0], w_lo)
    return [top[i * rows:(i + 1) * rows] + top[(n + i) * rows:(n + i + 1) * rows] + bot[i * rows:(i + 1) * rows]
            for i in range(n)]


def _cp(*sem):
    return pltpu.CompilerParams(dimension_semantics=sem, vmem_limit_bytes=VMEM_LIMIT)


def _mod_row(i, prompt_tiles, tiles_per_seq):
    return jnp.where(i < prompt_tiles, 0, 1 + (i - prompt_tiles) // tiles_per_seq)


def _mods_body(c_ref, w_ref, b_ref, o_ref):
    c = c_ref[...]
    o_ref[...] = _dot(c * jax.nn.sigmoid(c), w_ref[...], precision=HI) + b_ref[...]


def ada_mods_all(cv, w_mod, b_mod):
    depth, _, n6 = w_mod.shape
    r = cv.shape[0]
    tn = 1536
    return pl.pallas_call(
        _mods_body,
        grid=(depth, n6 // tn),
        in_specs=[pl.BlockSpec((r, D), lambda l, j: (0, 0)),
                  pl.BlockSpec((None, D, tn), lambda l, j: (l, 0, j)),
                  pl.BlockSpec((None, 1, tn), lambda l, j: (l, 0, j))],
        out_specs=pl.BlockSpec((None, r, tn), lambda l, j: (l, 0, j)),
        out_shape=jax.ShapeDtypeStruct((depth, r, n6), f32),
        compiler_params=_cp("arbitrary", "arbitrary"),
        name="ada_mods",
    )(cv, w_mod, b_mod.reshape(depth, 1, n6))


def _normmm_body(x_ref, m_ref, nw_ref, w_ref, *rest, shift_idx, split):
    if split:
        wlo_ref, rest = rest[0], rest[1:]
    o_ref, h_refs = rest[0], rest[1:]
    x = x_ref[...]
    y = x * lax.rsqrt(jnp.mean(x * x, -1, keepdims=True) + EPS) * nw_ref[...]
    h = y * (1.0 + m_ref[shift_idx + 1:shift_idx + 2, :]) + m_ref[shift_idx:shift_idx + 1, :]
    hb = h.astype(bf16)
    acc = _dot(hb, w_ref[...])
    if split:
        acc = acc + _dot((h - hb.astype(f32)).astype(bf16), w_ref[...]) + _dot(hb, wlo_ref[...])
    o_ref[...] = acc
    if h_refs:
        h_refs[0][...] = h


def normmm(x, mods_l, shift_idx, norm_w, w, *, prompt_rows, seq_len, tm=256, with_h=False):
    n = x.shape[0]
    nout = w.shape[1]
    split = w.dtype == f32
    ws = [w]
    if split:
        w_hi = w.astype(bf16)
        ws = [w_hi, (w - w_hi.astype(f32)).astype(bf16)]
    pt, tps = prompt_rows // tm, seq_len // tm
    out_shape = [jax.ShapeDtypeStruct((n, nout), f32)]
    out_specs = [pl.BlockSpec((tm, nout), lambda i: (i, 0))]
    if with_h:
        out_shape.append(jax.ShapeDtypeStruct((n, D), f32))
        out_specs.append(pl.BlockSpec((tm, D), lambda i: (i, 0)))
    res = pl.pallas_call(
        functools.partial(_normmm_body, shift_idx=shift_idx, split=split),
        grid=(n // tm,),
        in_specs=[pl.BlockSpec((tm, D), lambda i: (i, 0)),
                  pl.BlockSpec((None, 6, D), lambda i: (_mod_row(i, pt, tps), 0, 0)),
                  pl.BlockSpec((1, D), lambda i: (0, 0))] + [pl.BlockSpec((D, nout), lambda i: (0, 0)) for _ in ws],
        out_specs=out_specs,
        out_shape=out_shape,
        compiler_params=_cp("arbitrary"),
        name="normmm",
    )(x, mods_l, norm_w.reshape(1, D), *ws)
    return res if with_h else res[0]


def _outproj_body(*refs, n_in, gate_idx):
    x_ref, m_ref = refs[0], refs[1]
    a_refs = refs[2:2 + n_in]
    w_refs = refs[2 + n_in:2 + 2 * n_in]
    o_ref = refs[2 + 2 * n_in]
    y = _dot(a_refs[0][...].astype(bf16), w_refs[0][...])
    for a_ref, w_ref in zip(a_refs[1:], w_refs[1:]):
        y = y + _dot(a_ref[...].astype(bf16), w_ref[...])
    o_ref[...] = x_ref[...] + m_ref[gate_idx:gate_idx + 1, :] * y


def outproj(x, mods_l, gate_idx, acts, ws, *, prompt_rows, seq_len, tm=256):
    n = x.shape[0]
    pt, tps = prompt_rows // tm, seq_len // tm
    in_specs = [pl.BlockSpec((tm, D), lambda i: (i, 0)),
                pl.BlockSpec((None, 6, D), lambda i: (_mod_row(i, pt, tps), 0, 0))]
    in_specs += [pl.BlockSpec((tm, a.shape[1]), lambda i: (i, 0)) for a in acts]
    in_specs += [pl.BlockSpec(w.shape, lambda i: (0, 0)) for w in ws]
    return pl.pallas_call(
        functools.partial(_outproj_body, n_in=len(acts), gate_idx=gate_idx),
        grid=(n // tm,),
        in_specs=in_specs,
        out_specs=pl.BlockSpec((tm, D), lambda i: (i, 0)),
        out_shape=jax.ShapeDtypeStruct((n, D), f32),
        compiler_params=_cp("arbitrary"),
        name="outproj",
    )(x, mods_l, *acts, *ws)


def _softplus(x):
    return jnp.maximum(x, 0.0) + jnp.log1p(jnp.exp(-jnp.abs(x)))


def _gdn_body(nega_ref, dt_ref, q_ref, k_ref, v_ref, gate_ref, ba_ref, cwq_ref, cwk_ref, cwv_ref, onw_ref,
              *rest, seq, has_s0):
    if has_s0:
        s0_ref, rest = rest[0], rest[1:]
    out_ref, sout_ref, qs, ks, vs, us, ws, qes, kes, qks, egs = rest
    h = pl.program_id(1)
    C = GDN_CHUNK
    nch = seq // C

    row = lax.broadcasted_iota(i32, (seq, LANES), 0)
    first = row == 0
    last = row == seq - 1

    def conv_silu(x_ref, w_ref):
        x = x_ref[...]
        w = w_ref[...]
        prev = jnp.where(first, 0.0, pltpu.roll(x, 1, 0))
        nxt = jnp.where(last, 0.0, pltpu.roll(x, seq - 1, 0))
        y = prev * w[0:1, :] + x * w[1:2, :] + nxt * w[2:3, :]
        return y * jax.nn.sigmoid(y)

    def l2n(x):
        return x * lax.rsqrt(jnp.sum(x * x, -1, keepdims=True) + EPS)

    qs[...] = l2n(conv_silu(q_ref, cwq_ref)) * (GDN_DK ** -0.5)
    ks[...] = l2n(conv_silu(k_ref, cwk_ref))
    vs[...] = conv_silu(v_ref, cwv_ref)

    ii = lax.broadcasted_iota(i32, (C, C), 0)
    jj = lax.broadcasted_iota(i32, (C, C), 1)
    eye = (ii == jj).astype(f32)

    row_c = lax.broadcasted_iota(i32, (C, LANES), 0)
    NB = 4
    same_block = (lax.broadcasted_iota(i32, (NB * C, NB * C), 0) // C
                  == lax.broadcasted_iota(i32, (NB * C, NB * C), 1) // C)

    def block_diag(m_cat):
        return jnp.where(same_block, jnp.concatenate([m_cat] * NB, axis=0), jnp.zeros((), m_cat.dtype))

    def chunk_algebra(cp):
        a_blocks, rhs_blocks, dest = [], [], []
        for cc in range(2):
            r0 = pl.multiple_of((2 * cp + cc) * C, C)
            qc = qs[pl.ds(r0, C), :]
            kc = ks[pl.ds(r0, C), :]
            vc = vs[pl.ds(r0, C), :]
            ba = ba_ref[pl.ds(r0, C), :]
            betas = [jax.nn.sigmoid(ba[:, d:d + 1]) for d in range(2)]
            kbs = [kc * b for b in betas]
            aq = _dot(jnp.concatenate(kbs + [qc], axis=0).astype(bf16), kc.astype(bf16), NT)
            for d in range(2):
                rel = (ii - jj) if d == 0 else (jj - ii)
                m_incl = rel >= 0
                g = nega_ref[d, h] * _softplus(ba[:, 2 + d:3 + d] + dt_ref[d, h])
                gcB = jnp.broadcast_to(g, (C, LANES))
                for sh in (1, 2, 4, 8, 16, 32):
                    if d == 0:
                        gcB = gcB + jnp.where(row_c >= sh, pltpu.roll(gcB, sh, 0), 0.0)
                    else:
                        gcB = gcB + jnp.where(row_c < C - sh, pltpu.roll(gcB, C - sh, 0), 0.0)
                gc = gcB[:, 0:1]
                gc_row = gcB.T[:C, :]
                decay = jnp.exp(jnp.where(m_incl, gcB[:, :C] - gc_row, -jnp.inf))
                a_blocks.append(jnp.where(rel > 0, aq[d * C:(d + 1) * C] * decay, 0.0))
                qks[d, pl.ds(r0, C), 0:C] = jnp.where(m_incl, aq[2 * C:] * decay, 0.0)
                rhs_blocks.append(jnp.concatenate([vc * betas[d], kbs[d] * jnp.exp(gc)], axis=1))
                dest.append((d, r0))
                qes[d, pl.ds(r0, C), :] = qc * jnp.exp(gc)
                g_last = gcB[C - 1:C, :] if d == 0 else gcB[0:1, :]
                kes[d, pl.ds(r0, C), :] = kc * jnp.exp(g_last - gcB)
                egs[d, pl.ds(pl.multiple_of((2 * cp + cc) * SUBLANES, SUBLANES), SUBLANES), :] = jnp.broadcast_to(
                    jnp.exp(g_last), (SUBLANES, LANES))
        return jnp.concatenate(a_blocks, axis=1), rhs_blocks, dest

    GROUPS = min(4, nch // 2)

    def prep_step(it, carry):
        groups = [chunk_algebra(GROUPS * it + gi) for gi in range(GROUPS)]
        eyes = jnp.concatenate([eye] * NB, axis=1)
        Xs = [eyes - g[0] for g in groups]
        pps = [_split(g[0]) for g in groups]
        Ps = [_split_mm([pp], block_diag(pp[0]), block_diag(pp[1]))[0] for pp in pps]
        for _ in range(4):
            pps = [_split(P) for P in Ps]
            res = [_split_mm([_split(X), pp], block_diag(pp[0]), block_diag(pp[1])) for X, pp in zip(Xs, pps)]
            Xs = [X + r[0] for X, r in zip(Xs, res)]
            Ps = [r[1] for r in res]
        pps = [_split(P) for P in Ps]
        Xs = [X + _split_mm([_split(X)], block_diag(pp[0]), block_diag(pp[1]))[0] for X, pp in zip(Xs, pps)]
        for X, (_, rhs_blocks, dest) in zip(Xs, groups):
            x_hi, x_lo = _split(X)
            for k, (d, r0) in enumerate(dest):
                xs = slice(k * C, (k + 1) * C)
                (sol,) = _split_mm([(x_hi[:, xs], x_lo[:, xs])], *_split(rhs_blocks[k]))
                us[d, pl.ds(r0, C), :] = sol[:, :GDN_DK]
                ws[d, pl.ds(r0, C), :] = sol[:, GDN_DK:]
        return carry

    lax.fori_loop(0, nch // (2 * GROUPS), prep_step, 0)

    def step(i, carry):
        cs = (i, nch - 1 - i)
        r0s = [pl.multiple_of(c * C, C) for c in cs]
        wqs = [_dot(jnp.concatenate([ws[d, pl.ds(r0s[d], C), :], qes[d, pl.ds(r0s[d], C), :]], axis=0).astype(bf16),
                    carry[d].astype(bf16)) for d in range(2)]
        vbs = [(us[d, pl.ds(r0s[d], C), :] - wqs[d][:C]).astype(bf16) for d in range(2)]
        outs = [wqs[d][C:] + _dot(qks[d, pl.ds(r0s[d], C), 0:C].astype(bf16), vbs[d]) for d in range(2)]
        upd = [_dot(kes[d, pl.ds(r0s[d], C), :].astype(bf16), vbs[d], TN) for d in range(2)]
        new = []
        for d in range(2):
            us[d, pl.ds(r0s[d], C), :] = outs[d]
            eg = egs[d, pl.ds(pl.multiple_of(cs[d] * SUBLANES, SUBLANES), 1), :]
            new.append(carry[d] * eg + upd[d])
        return tuple(new)

    if has_s0:
        S0 = (s0_ref[0], s0_ref[1])
    else:
        S0 = (jnp.zeros((GDN_DK, GDN_DK), f32),) * 2
    Sf, Sb = lax.fori_loop(0, nch, step, S0)
    sout_ref[0] = Sf
    sout_ref[1] = Sb

    o = us[0] + us[1]
    y = o * lax.rsqrt(jnp.mean(o * o, -1, keepdims=True) + EPS) * onw_ref[...]
    gt = gate_ref[...]
    out_ref[...] = y * (gt * jax.nn.sigmoid(gt))


def gdn(p, nega, dtb, conv_w, onw, s0, *, row_off, batch, seq, ba_col0):
    rb = row_off // seq
    H = GDN_H
    nch = seq // GDN_CHUNK

    def col(cb):
        return pl.BlockSpec((seq, LANES), lambda b, h: (rb + b, cb + h), pipeline_mode=pl.Buffered(1))

    state_spec = pl.BlockSpec((None, 2, None, GDN_DK, GDN_DK), lambda b, h: (b, 0, h, 0, 0))
    in_specs = [pl.BlockSpec(memory_space=pltpu.SMEM), pl.BlockSpec(memory_space=pltpu.SMEM),
                col(0), col(H), col(2 * H), col(3 * H), col(ba_col0),
                pl.BlockSpec((3, LANES), lambda b, h: (0, h)),
                pl.BlockSpec((3, LANES), lambda b, h: (0, H + h)),
                pl.BlockSpec((3, LANES), lambda b, h: (0, 2 * H + h)),
                pl.BlockSpec((1, LANES), lambda b, h: (0, 0))]
    args = [nega, dtb, p, p, p, p, p, conv_w, conv_w, conv_w, onw.reshape(1, LANES)]
    if s0 is not None:
        in_specs.append(state_spec)
        args.append(s0)
    seq_buf = pltpu.VMEM((seq, LANES), f32)
    dir_buf = pltpu.VMEM((2, seq, LANES), f32)
    return pl.pallas_call(
        functools.partial(_gdn_body, seq=seq, has_s0=s0 is not None),
        grid=(batch, H),
        in_specs=in_specs,
        out_specs=[pl.BlockSpec((seq, LANES), lambda b, h: (b, h)), state_spec],
        out_shape=[jax.ShapeDtypeStruct((batch * seq, H * LANES), f32),
                   jax.ShapeDtypeStruct((batch, 2, H, GDN_DK, GDN_DK), f32)],
        scratch_shapes=[seq_buf, seq_buf, seq_buf, dir_buf, dir_buf, dir_buf, dir_buf, dir_buf,
                        pltpu.VMEM((2, nch * SUBLANES, LANES), f32)],
        compiler_params=_cp("arbitrary", "arbitrary"),
        name="gdn",
    )(*args)


def _fnet_body(x_ref, cd_ref, f_ref, o_ref, z_scr, *, seq):
    r = pl.program_id(1)

    @pl.when(r == 0)
    def _stage1():
        for g in range(FN_W // LANES):
            xg = x_ref[:, g * LANES:(g + 1) * LANES].astype(bf16)
            y = _dot(xg, cd_ref[...])
            z_scr[0:seq, g * LANES:(g + 1) * LANES] = y[:, :LANES].astype(bf16)
            z_scr[seq:2 * seq, g * LANES:(g + 1) * LANES] = y[:, LANES:].astype(bf16)

    scale = 1.0 / math.sqrt(seq * LANES)
    o_ref[...] = _dot(f_ref[...], z_scr[...]) * scale


def fnet(p, cd, fmat, *, row_off, batch, seq, col_block, tr):
    rb = row_off // seq
    return pl.pallas_call(
        functools.partial(_fnet_body, seq=seq),
        grid=(batch, seq // tr),
        in_specs=[pl.BlockSpec((seq, FN_W), lambda b, r: (rb + b, col_block)),
                  pl.BlockSpec((LANES, 2 * LANES), lambda b, r: (0, 0)),
                  pl.BlockSpec((tr, 2 * seq), lambda b, r: (r, 0))],
        out_specs=pl.BlockSpec((tr, FN_W), lambda b, r: (b * (seq // tr) + r, 0)),
        out_shape=jax.ShapeDtypeStruct((batch * seq, FN_W), f32),
        scratch_shapes=[pltpu.VMEM((2 * seq, FN_W), bf16)],
        compiler_params=_cp("arbitrary", "arbitrary"),
        name="fnet",
    )(p, cd, fmat)


def _dft_mats(n):
    k = jnp.arange(n, dtype=i32)
    ang = ((k[:, None] * k[None, :]) % n).astype(f32) * (2.0 * math.pi / n)
    return jnp.cos(ang), jnp.sin(ang)


def _rope(x, cos, sin_signed, first_half):
    partner = jnp.where(first_half, pltpu.roll(x, LANES - MLA_ROPE // 4, 1), pltpu.roll(x, MLA_ROPE // 4, 1))
    return x * cos + partner * sin_signed


def _mla_body(*refs, has_q, normalize, use_rope):
    it = iter(refs)
    if normalize:
        p_ref = next(it)
        qnw_ref = next(it)
        kvnw_ref = next(it)
    else:
        ckv_ref = next(it)
        krp_ref = next(it)
    if has_q:
        wqn_ref, wqr_ref, qhn_ref, qhr_ref = next(it), next(it), next(it), next(it)
    wkn_ref, wv_ref, khn_ref, khr_ref = next(it), next(it), next(it), next(it)
    if use_rope:
        cos_ref, sin_ref = next(it), next(it)
    if has_q:
        qf_ref = next(it)
    kf_ref, v_ref = next(it), next(it)
    if normalize:
        ckv_out, krp_out = next(it), next(it)

    def rms(x):
        return x * lax.rsqrt(jnp.mean(x * x, -1, keepdims=True) + EPS)

    if normalize:
        p = p_ref[...]
        cq = rms(p[:, :MLA_QL]) * qnw_ref[...]
        ckv = rms(p[:, MLA_QL:MLA_QL + MLA_KVL]) * kvnw_ref[...]
        krp = p[:, MLA_QL + MLA_KVL:]
        ckv_out[...] = ckv
        krp_out[...] = krp
    else:
        ckv = ckv_ref[...]
        krp = krp_ref[...]
    if use_rope:
        cos = cos_ref[...]
        sin = sin_ref[...]
        lane = lax.broadcasted_iota(i32, cos.shape, 1)
        first_half = (lane % (MLA_ROPE // 2)) < (MLA_ROPE // 4)

    def head_norm(a, b, wn, wr, scale):
        ss = jnp.sum(a * a, -1, keepdims=True) + jnp.sum(b * b, -1, keepdims=True)
        r = lax.rsqrt(ss * (1.0 / MLA_QK) + EPS)
        a = a * r * wn
        b = b * r * wr
        if use_rope:
            b = _rope(b, cos, sin, first_half)
        return (a * scale).astype(bf16), (b * scale).astype(bf16)

    if has_q:
        cqb = cq.astype(bf16)
        qn = _dot(cqb, wqn_ref[...])
        qr = _dot(cqb, wqr_ref[...])
        for h in range(MLA_H):
            a, b = head_norm(qn[:, h * LANES:(h + 1) * LANES], qr[:, h * LANES:(h + 1) * LANES],
                             qhn_ref[...], qhr_ref[...], MLA_QK ** -0.5)
            qf_ref[:, 2 * h * LANES:(2 * h + 1) * LANES] = a
            qf_ref[:, (2 * h + 1) * LANES:(2 * h + 2) * LANES] = b
    ckvb = ckv.astype(bf16)
    kn = _dot(ckvb, wkn_ref[...])
    v_ref[...] = _dot(ckvb, wv_ref[...]).astype(bf16)
    for h in range(MLA_H):
        a, b = head_norm(kn[:, h * LANES:(h + 1) * LANES], krp, khn_ref[...], khr_ref[...], 1.0)
        kf_ref[:, 2 * h * LANES:(2 * h + 1) * LANES] = a
        kf_ref[:, (2 * h + 1) * LANES:(2 * h + 2) * LANES] = b


def mla_prep(srcs, norm_ws, q_ws, kv_ws, rope_tabs, *, row_off, rows, seq, tm=256):
    normalize = norm_ws is not None
    has_q = q_ws is not None
    use_rope = rope_tabs is not None
    ro = row_off // tm
    nt = rows // tm
    hd = MLA_H * 2 * LANES

    def full(a):
        return pl.BlockSpec(a.shape, lambda i: (0,) * a.ndim)

    args, in_specs = [], []
    for s in srcs:
        args.append(s)
        in_specs.append(pl.BlockSpec((tm, s.shape[1]), lambda i: (ro + i, 0)))
    for group in (norm_ws, q_ws, kv_ws):
        if group is not None:
            for a in group:
                args.append(a)
                in_specs.append(full(a))
    if use_rope:
        tps = seq // tm
        for a in rope_tabs:
            args.append(a)
            in_specs.append(pl.BlockSpec((tm, LANES), lambda i: (i % tps, 0)))
    out_shape, out_specs = [], []

    def out(cols, dt):
        out_shape.append(jax.ShapeDtypeStruct((rows, cols), dt))
        out_specs.append(pl.BlockSpec((tm, cols), lambda i: (i, 0)))

    if has_q:
        out(hd, bf16)
    out(hd, bf16)
    out(MLA_H * LANES, bf16)
    if normalize:
        out(MLA_KVL, f32)
        out(LANES, f32)
    return pl.pallas_call(
        functools.partial(_mla_body, has_q=has_q, normalize=normalize, use_rope=use_rope),
        grid=(nt,),
        in_specs=in_specs,
        out_specs=out_specs,
        out_shape=out_shape,
        compiler_params=_cp("arbitrary"),
        name="mla_prep",
    )(*args)


ATTN_KCHUNK = 1024


def _attn_body(*refs, n_kv):
    q_ref = refs[0]
    k_refs = refs[1:1 + n_kv]
    v_refs = refs[1 + n_kv:1 + 2 * n_kv]
    o_ref = refs[1 + 2 * n_kv]
    q = q_ref[...]
    chunks = [(k_ref, v_ref, c0, min(c0 + ATTN_KCHUNK, k_ref.shape[0]))
              for k_ref, v_ref in zip(k_refs, v_refs) for c0 in range(0, k_ref.shape[0], ATTN_KCHUNK)]

    def scores(ch):
        return _dot(q, ch[0][ch[2]:ch[3], :], NT)

    s_next = scores(chunks[0])
    m = l = acc = None
    for c, ch in enumerate(chunks):
        s = s_next
        if c + 1 < len(chunks):
            s_next = scores(chunks[c + 1])
        ms = jnp.max(s, -1, keepdims=True)
        v = ch[1][ch[2]:ch[3], :]
        if m is None:
            m = ms
            e = jnp.exp(s - m)
            l = jnp.sum(e, -1, keepdims=True)
            acc = _dot(e.astype(bf16), v)
        else:
            m_new = jnp.maximum(m, ms)
            scale = jnp.exp(m - m_new)
            e = jnp.exp(s - m_new)
            l = l * scale + jnp.sum(e, -1, keepdims=True)
            acc = acc * scale + _dot(e.astype(bf16), v)
            m = m_new
    o_ref[...] = acc / l


def attend(qf, kfs, vs, *, batch, lq, lks, tq=256):
    n_kv = len(kfs)
    nq = lq // tq
    in_specs = [pl.BlockSpec((tq, 2 * LANES), lambda b, h, i: (b * nq + i, h))]
    in_specs += [pl.BlockSpec((lk, 2 * LANES), lambda b, h, i: (b, h)) for lk in lks]
    in_specs += [pl.BlockSpec((lk, LANES), lambda b, h, i: (b, h)) for lk in lks]
    return pl.pallas_call(
        functools.partial(_attn_body, n_kv=n_kv),
        grid=(batch, MLA_H, nq),
        in_specs=in_specs,
        out_specs=pl.BlockSpec((tq, LANES), lambda b, h, i: (b * nq + i, h)),
        out_shape=jax.ShapeDtypeStruct((batch * lq, MLA_H * LANES), f32),
        compiler_params=_cp("arbitrary", "arbitrary", "arbitrary"),
        name="attend",
    )(qf, *kfs, *vs)


def _topk_rows(s, k, val_scr, idx_scr, payload=None):
    rows = s.shape[0]
    iota = lax.broadcasted_iota(i32, s.shape, 0)
    for r in range(k):
        m = jnp.max(s, axis=0, keepdims=True)
        idx = jnp.min(jnp.where(s == m, iota, rows), axis=0, keepdims=True)
        hit = iota == idx
        val_scr[r:r + 1, :] = m
        if payload is None:
            idx_scr[r:r + 1, :] = idx
        else:
            idx_scr[r:r + 1, :] = jnp.max(jnp.where(hit, payload, -1), axis=0, keepdims=True)
        s = jnp.where(hit, -jnp.inf, s)


def _pair_candidates(sv0, sv1, si0, si1):
    K = PEER_K
    sub = lax.broadcasted_iota(i32, (SUBLANES, sv0.shape[1]), 0)
    vals, idxs = [], []
    for a in range(SUBLANES):
        nb = K // (a + 1)
        for b0 in range(0, nb, SUBLANES):
            v = sv0[a:a + 1, :] + sv1[b0:b0 + SUBLANES, :]
            e = si0[a:a + 1, :] * PEER_NK + si1[b0:b0 + SUBLANES, :]
            if nb - b0 < SUBLANES:
                v = jnp.where(sub < nb - b0, v, -jnp.inf)
            vals.append(v)
            idxs.append(e)
    vals.append(sv0[SUBLANES:K, :] + sv1[0:1, :])
    idxs.append(si0[SUBLANES:K, :] * PEER_NK + si1[0:1, :])
    return jnp.concatenate(vals, axis=0), jnp.concatenate(idxs, axis=0)


def _u_slot_rows(s):
    return SUBLANES * (s // SUBLANES) + HALF * (s % 2) + (s % SUBLANES) // 2


def _peer_topk_body(q_ref, keys_ref, cu_ref, gate_ref, sv, si, cv, ci, e_all, g_all):
    K = PEER_K
    tt = q_ref.shape[0]
    for h in range(PEER_H):
        for p in range(2):
            qhp = q_ref[:, (2 * h + p) * LANES:(2 * h + p + 1) * LANES]
            s = _dot(keys_ref[h, p], qhp, NT, precision=HI)
            _topk_rows(s, K, sv.at[p], si.at[p])
        cand, cidx = _pair_candidates(sv[0], sv[1], si[0], si[1])
        _topk_rows(cand, K, cv, ci, payload=cidx)
        cs = cv[...]
        e = jnp.exp(cs - cs[0:1, :])
        g_all[h * K:(h + 1) * K, :] = e / jnp.sum(e, axis=0, keepdims=True)
        code = (ci[...] + PACK_TE) * HALF
        for b in range(tt // GATHER_TB):
            e_all[b, h * K:(h + 1) * K, :] = code[:, b * GATHER_TB:(b + 1) * GATHER_TB]
    gate_ref[...] = g_all[...].T
    for b in range(tt // GATHER_TB):
        for s in range(U_SLOTS):
            cu_ref[s, b] = e_all[b, pl.ds(_u_slot_rows(s), U_GROUPS, stride=U_SLOTS), :] - HALF * (s % 2)


def peer_topk(q, sub_keys, *, tt=512):
    n = q.shape[0]
    K = PEER_K
    nb = tt // GATHER_TB
    return pl.pallas_call(
        _peer_topk_body,
        grid=(n // tt,),
        in_specs=[pl.BlockSpec((tt, q.shape[1]), lambda i: (i, 0)),
                  pl.BlockSpec(sub_keys.shape, lambda i: (0, 0, 0, 0))],
        out_specs=[pl.BlockSpec((U_SLOTS, nb, U_GROUPS, GATHER_TB), lambda i: (0, i, 0, 0)),
                   pl.BlockSpec((tt, PEER_SEL), lambda i: (i, 0))],
        out_shape=[jax.ShapeDtypeStruct((U_SLOTS, n // GATHER_TB, U_GROUPS, GATHER_TB), i32),
                   jax.ShapeDtypeStruct((n, PEER_SEL), f32)],
        scratch_shapes=[pltpu.VMEM((2, K, tt), f32), pltpu.VMEM((2, K, tt), i32),
                        pltpu.VMEM((K, tt), f32), pltpu.VMEM((K, tt), i32),
                        pltpu.VMEM((nb, PEER_SEL, GATHER_TB), i32), pltpu.VMEM((PEER_SEL, tt), f32)],
        compiler_params=_cp("arbitrary"),
        name="peer_topk",
    )(q, sub_keys)


def _pack_body(x_ref, o_ref):
    i = pl.program_id(0)
    guard = jnp.logical_or(i == 0, i == pl.num_programs(0) - 1)

    @pl.when(guard)
    def _zero():
        o_ref[...] = jnp.zeros_like(o_ref)

    @pl.when(jnp.logical_not(guard))
    def _pack():
        half = x_ref.shape[1] // 2
        lo = lax.bitcast_convert_type(x_ref[:, :half].astype(bf16).astype(f32), i32)
        hi = lax.bitcast_convert_type(x_ref[:, half:].astype(bf16).astype(f32), i32)
        words = (hi & jnp.int32(HI_MASK)) | lax.shift_right_logical(lo, jnp.int32(16))
        for s in range(HALF):
            o_ref[pl.ds(s, PACK_TE, stride=HALF), :] = words[:, s * LANES:(s + 1) * LANES]


def pack_expert_table(tab):
    e, d = tab.shape
    nb = e // PACK_TE
    return pl.pallas_call(
        _pack_body,
        grid=(nb + 2,),
        in_specs=[pl.BlockSpec((PACK_TE, d), lambda i: (jnp.clip(i - 1, 0, nb - 1), 0))],
        out_specs=pl.BlockSpec((PACK_TE * HALF, LANES), lambda i: (i, 0)),
        out_shape=jax.ShapeDtypeStruct(((nb + 2) * PACK_TE * HALF, LANES), i32),
        compiler_params=_cp("arbitrary"),
        name="pack_table",
    )(tab)


def _merged(tab_ref, ca, cb, lo_half):
    a = tab_ref[pl.ds(pl.multiple_of(ca, HALF), SUBLANES), :]
    b = tab_ref[pl.ds(pl.multiple_of(cb, HALF), SUBLANES), :]
    return jnp.where(lo_half, a, b)


def _lo(words):
    return lax.bitcast_convert_type(words << 16, f32)


def _hi(words):
    return lax.bitcast_convert_type(words & jnp.int32(HI_MASK), f32)


def _gelu_tanh(x):
    return 0.5 * x * (1.0 + jnp.tanh(math.sqrt(2.0 / math.pi) * (x + 0.044715 * (x * x * x))))


U_FIN_UNROLL = 16


def _fold4(ms, sub):
    lo2 = (sub % 4) < 2
    b = [jnp.where(lo2, ms[k] + pltpu.roll(ms[k], SUBLANES - 2, 0), ms[k + 2] + pltpu.roll(ms[k + 2], 2, 0))
         for k in range(2)]
    lo1 = (sub % 2) < 1
    return jnp.where(lo1, b[0] + pltpu.roll(b[0], SUBLANES - 1, 0), b[1] + pltpu.roll(b[1], 1, 0))


def _peer_u_body(*refs, tb):
    c_refs = refs[:U_SLOTS]
    gate_ref, h_ref, tab_ref, wt_ref, z_ref, h8_ref, w_ref = refs[U_SLOTS:]
    sub = lax.broadcasted_iota(i32, (SUBLANES, LANES), 0)
    lo_half = sub < HALF
    for s in range(SUBLANES):
        h8_ref[pl.ds(s, tb, stride=SUBLANES), :] = h_ref[:, s * LANES:(s + 1) * LANES]

    def token(t, carry):
        r8 = pl.multiple_of(t * SUBLANES, SUBLANES)
        hrow = h8_ref[pl.ds(r8, SUBLANES), :]
        hswap = pltpu.roll(hrow, HALF, 0)
        hl = jnp.where(lo_half, hrow, hswap)
        hh = jnp.where(lo_half, hswap, hrow)
        for g in range(U_GROUPS):
            for fold in range(2):
                ms = []
                for k in range(4):
                    s0 = fold * SUBLANES + 2 * k
                    words = _merged(tab_ref, c_refs[s0][g * tb + t], c_refs[s0 + 1][g * tb + t], lo_half)
                    ms.append(_lo(words) * hl + _hi(words) * hh)
                row0 = t * PEER_SEL + (2 * g + fold) * SUBLANES
                z_ref[pl.ds(pl.multiple_of(row0, SUBLANES), SUBLANES), :] = _fold4(ms, sub)
        return carry

    lax.fori_loop(0, tb, token, 0)

    def finish(tg, carry):
        for i in range(U_FIN_UNROLL):
            t = tg * U_FIN_UNROLL + i
            zt = z_ref[pl.ds(pl.multiple_of(t * PEER_SEL, PEER_SEL), PEER_SEL), :]
            w_ref[pl.ds(t, 1), :] = jnp.sum(zt.T, axis=0, keepdims=True)
        return carry

    lax.fori_loop(0, tb // U_FIN_UNROLL, finish, 0)
    wt_ref[...] = (gate_ref[...] * _gelu_tanh(w_ref[...])).T


def peer_u_gather(codes, gate, h, tab):
    tb = GATHER_TB
    n = gate.shape[0]
    codes = codes.reshape(U_SLOTS, n * U_GROUPS)
    smem = [pl.BlockSpec((tb * U_GROUPS,), lambda i: (i,), memory_space=pltpu.SMEM) for _ in range(U_SLOTS)]
    return pl.pallas_call(
        functools.partial(_peer_u_body, tb=tb),
        grid=(n // tb,),
        in_specs=smem + [pl.BlockSpec((tb, PEER_SEL), lambda i: (i, 0)),
                         pl.BlockSpec((tb, D), lambda i: (i, 0)),
                         pl.BlockSpec(tab.shape, lambda i: (0, 0), pipeline_mode=pl.Buffered(1))],
        out_specs=pl.BlockSpec((None, PEER_SEL, tb), lambda i: (i, 0, 0)),
        out_shape=jax.ShapeDtypeStruct((n // tb, PEER_SEL, tb), f32),
        scratch_shapes=[pltpu.VMEM((tb * PEER_SEL, LANES), f32), pltpu.VMEM((tb * SUBLANES, LANES), f32),
                        pltpu.VMEM((tb, PEER_SEL), f32)],
        compiler_params=_cp("arbitrary"),
        name="peer_u",
    )(*[codes[s] for s in range(U_SLOTS)], gate, h, tab)


def _peer_v_body(*refs, tb):
    c_refs = refs[:U_SLOTS]
    wt_ref, x_ref, m_ref, tab_ref, o_ref, y8_ref, wx_ref = refs[U_SLOTS:]
    sub = lax.broadcasted_iota(i32, (SUBLANES, LANES), 0)
    lo_half = sub < HALF

    def expanded(t):
        return jnp.take_along_axis(wt_ref[...], jnp.full((PEER_SEL, tb), t, i32), axis=1)

    wx_ref[...] = expanded(0)

    def token(t, carry):
        nxt = expanded(jnp.minimum(t + 1, tb - 1))
        accs = [jnp.zeros((SUBLANES, LANES), f32) for _ in range(4)]
        for g in range(U_GROUPS):
            for fold in range(2):
                for k in range(4):
                    s0 = fold * SUBLANES + 2 * k
                    ja = g * U_SLOTS + _u_slot_rows(s0)
                    jb = g * U_SLOTS + _u_slot_rows(s0 + 1)
                    words = _merged(tab_ref, c_refs[s0][g * tb + t], c_refs[s0 + 1][g * tb + t], lo_half)
                    wm = jnp.where(lo_half, jnp.broadcast_to(wx_ref[ja:ja + 1, :], (SUBLANES, LANES)),
                                   jnp.broadcast_to(wx_ref[jb:jb + 1, :], (SUBLANES, LANES)))
                    accs[2 * (k % 2)] = accs[2 * (k % 2)] + wm * _lo(words)
                    accs[2 * (k % 2) + 1] = accs[2 * (k % 2) + 1] + wm * _hi(words)
        lo = accs[0] + accs[2]
        hi = accs[1] + accs[3]
        lo = lo + pltpu.roll(lo, HALF, 0)
        hi = hi + pltpu.roll(hi, HALF, 0)
        y8_ref[pl.ds(pl.multiple_of(t * SUBLANES, SUBLANES), SUBLANES), :] = jnp.where(lo_half, lo, hi)
        wx_ref[...] = nxt
        return carry

    lax.fori_loop(0, tb, token, 0)
    for s in range(SUBLANES):
        cols = slice(s * LANES, (s + 1) * LANES)
        o_ref[:, cols] = x_ref[:, cols] + m_ref[5:6, cols] * y8_ref[pl.ds(s, tb, stride=SUBLANES), :]


def peer_v_gather(codes, wt, x, mods_l, tab, *, prompt_rows, seq_len):
    tb = GATHER_TB
    n = x.shape[0]
    pt, tps = prompt_rows // tb, seq_len // tb
    codes = codes.reshape(U_SLOTS, n * U_GROUPS)
    smem = [pl.BlockSpec((tb * U_GROUPS,), lambda i: (i,), memory_space=pltpu.SMEM) for _ in range(U_SLOTS)]
    return pl.pallas_call(
        functools.partial(_peer_v_body, tb=tb),
        grid=(n // tb,),
        in_specs=smem + [pl.BlockSpec((None, PEER_SEL, tb), lambda i: (i, 0, 0)),
                         pl.BlockSpec((tb, D), lambda i: (i, 0)),
                         pl.BlockSpec((None, 6, D), lambda i: (_mod_row(i, pt, tps), 0, 0)),
                         pl.BlockSpec(tab.shape, lambda i: (0, 0), pipeline_mode=pl.Buffered(1))],
        out_specs=pl.BlockSpec((tb, D), lambda i: (i, 0)),
        out_shape=jax.ShapeDtypeStruct((n, D), f32),
        scratch_shapes=[pltpu.VMEM((tb * SUBLANES, LANES), f32), pltpu.VMEM((PEER_SEL, LANES), f32)],
        compiler_params=_cp("arbitrary"),
        name="peer_v",
    )(*[codes[s] for s in range(U_SLOTS)], wt, x, mods_l, tab)


def _rope_tables(seq):
    rows = seq // GRID_W
    row = jnp.repeat(jnp.arange(rows, dtype=f32), GRID_W)
    col = jnp.tile(jnp.arange(GRID_W, dtype=f32), rows)
    nfreq = MLA_ROPE // 4
    inv = jnp.power(ROPE_BASE, -jnp.arange(nfreq, dtype=f32) / nfreq)
    ar = row[:, None] * inv
    ac = col[:, None] * inv
    pad1 = jnp.ones((seq, LANES - MLA_ROPE), f32)
    pad0 = jnp.zeros((seq, LANES - MLA_ROPE), f32)
    cos = jnp.concatenate([jnp.cos(ar), jnp.cos(ar), jnp.cos(ac), jnp.cos(ac), pad1], -1)
    sin = jnp.concatenate([-jnp.sin(ar), jnp.sin(ar), -jnp.sin(ac), jnp.sin(ac), pad0], -1)
    return cos, sin


def _pad_heads(w, head_w, lo, hi):
    k = w.shape[0]
    w = w.reshape(k, -1, head_w)[:, :, lo:hi]
    return jnp.pad(w, ((0, 0), (0, 0), (0, LANES - (hi - lo)))).reshape(k, -1)


def kernel(x_prompt, x_sample, state_gdn, cache_mla_ckv, cache_mla_krope, c, c_ctx, w_mod, b_mod, norm_mix, norm_ffn, even_w_in, even_conv_w, gdn_a_log, gdn_dt_bias, gdn_o_norm, even_w_out, odd_w_in, mla_q_norm, mla_kv_norm, mla_w_uq, mla_w_ukv, mla_q_headnorm, mla_k_headnorm, odd_w_out, peer_w_q, peer_sub_keys, peer_u, peer_v):
    B, L, _ = x_prompt.shape
    BS, LS, _ = x_sample.shape
    depth = w_mod.shape[0]
    NP, NS = B * L, BS * LS
    past = cache_mla_ckv.shape[2]
    geo = dict(prompt_rows=NP, seq_len=LS)

    x = jnp.concatenate([x_prompt.reshape(NP, D), x_sample.reshape(NS, D)], 0)
    nrow = 1 + BS
    rpad = -nrow % SUBLANES
    cv = jnp.concatenate([c_ctx[None, :], c, jnp.zeros((rpad, D), f32)], 0)
    mods = ada_mods_all(cv, w_mod, b_mod).reshape(depth, nrow + rpad, 6, D)

    cd_c, cd_s = _dft_mats(LANES)
    cd = jnp.concatenate([cd_c, cd_s], 1).astype(bf16)
    fmats = {}
    for n in (L, LS):
        cl, sl = _dft_mats(n)
        fmats[n] = jnp.concatenate([cl, -sl], 1).astype(bf16)
    cos_t, sin_t = _rope_tables(LS)

    new_gdn, new_ckv, new_kr = [], [], []
    for l in range(depth):
        j = l // 2
        m_l = mods[l]
        if l % 2 == 0:
            wi = even_w_in[j]
            o0 = 3 * A_QK + A_QK
            ba = wi[:, o0:o0 + 4 * GDN_H].reshape(D, 4, GDN_H).transpose(0, 2, 1)
            ba = jnp.pad(ba, ((0, 0), (0, 0), (0, LANES - 4))).reshape(D, GDN_H * LANES)
            w_in = jnp.concatenate([wi[:, :o0], wi[:, o0 + 4 * GDN_H:], ba], 1).astype(bf16)
            p = normmm(x, m_l, 0, norm_mix[l], w_in, **geo)
            nega = -jnp.exp(gdn_a_log[j])
            kw = dict(ba_col0=(o0 + FN_W) // LANES)
            mix_p, st = gdn(p, nega, gdn_dt_bias[j], even_conv_w[j], gdn_o_norm[j], None,
                            row_off=0, batch=B, seq=L, **kw)
            mix_s, _ = gdn(p, nega, gdn_dt_bias[j], even_conv_w[j], gdn_o_norm[j], state_gdn[:, j],
                           row_off=NP, batch=BS, seq=LS, **kw)
            new_gdn.append(st)
            fb_p = fnet(p, cd, fmats[L], row_off=0, batch=B, seq=L, col_block=o0 // FN_W, tr=L)
            fb_s = fnet(p, cd, fmats[LS], row_off=NP, batch=BS, seq=LS, col_block=o0 // FN_W, tr=256)
            wo = even_w_out[j].astype(bf16)
            x = outproj(x, m_l, 2, [jnp.concatenate([mix_p, mix_s], 0), jnp.concatenate([fb_p, fb_s], 0)],
                        [wo[:A_QK], wo[A_QK:]], **geo)
        else:
            wi = odd_w_in[j]
            w_in = jnp.pad(wi, ((0, 0), (0, LANES - MLA_ROPE))).astype(bf16)
            p = normmm(x, m_l, 0, norm_mix[l], w_in, **geo)
            norm_ws = (mla_q_norm[j].reshape(1, -1), mla_kv_norm[j].reshape(1, -1))
            qh, kh = mla_q_headnorm[j], mla_k_headnorm[j]

            def split_hw(hw):
                return (hw[:MLA_NOPE].reshape(1, LANES),
                        jnp.pad(hw[MLA_NOPE:], (0, LANES - MLA_ROPE)).reshape(1, LANES))

            q_ws = (_pad_heads(mla_w_uq[j], MLA_QK, 0, MLA_NOPE).astype(bf16),
                    _pad_heads(mla_w_uq[j], MLA_QK, MLA_NOPE, MLA_QK).astype(bf16)) + split_hw(qh)
            kv_ws = (_pad_heads(mla_w_ukv[j], 2 * LANES, 0, LANES).astype(bf16),
                     _pad_heads(mla_w_ukv[j], 2 * LANES, LANES, 2 * LANES).astype(bf16)) + split_hw(kh)
            qf_p, kf_p, v_p, ckv_p, krp_p = mla_prep((p,), norm_ws, q_ws, kv_ws, None, row_off=0, rows=NP, seq=L)
            qf_s, kf_s, v_s, _, _ = mla_prep((p,), norm_ws, q_ws, kv_ws, (cos_t, sin_t), row_off=NP, rows=NS, seq=LS)
            ckv_c = cache_mla_ckv[:, j].reshape(BS * past, MLA_KVL)
            krp_c = jnp.pad(cache_mla_krope[:, j].reshape(BS * past, MLA_ROPE), ((0, 0), (0, LANES - MLA_ROPE)))
            kf_c, v_c = mla_prep((ckv_c, krp_c), None, None, kv_ws, None, row_off=0, rows=BS * past, seq=past)
            o_p = attend(qf_p, [kf_p], [v_p], batch=B, lq=L, lks=[L])
            o_s = attend(qf_s, [kf_s, kf_c], [v_s, v_c], batch=BS, lq=LS, lks=[LS, past])
            new_ckv.append(ckv_p.reshape(B, L, MLA_KVL))
            new_kr.append(krp_p[:, :MLA_ROPE].reshape(B, L, MLA_ROPE))
            x = outproj(x, m_l, 2, [jnp.concatenate([o_p, o_s], 0)], [odd_w_out[j].astype(bf16)], **geo)
        q, h = normmm(x, m_l, 3, norm_ffn[l], peer_w_q[l], with_h=True, **geo)
        codes, gate = peer_topk(q, peer_sub_keys[l])
        wt = peer_u_gather(codes, gate, h, pack_expert_table(peer_u[l]))
        x = peer_v_gather(codes, wt, x, m_l, pack_expert_table(peer_v[l]), **geo)

    y_prompt = x[:NP].reshape(B, L, D)
    y_sample = x[NP:].reshape(BS, LS, D)
    return (y_prompt, y_sample, jnp.stack(new_gdn, 1), jnp.stack(new_ckv, 1), jnp.stack(new_kr, 1))
```

```python
import functools
import math

import jax
import jax.numpy as jnp
from jax import lax
from jax.experimental import pallas as pl
from jax.experimental.pallas import tpu as pltpu

f32 = jnp.float32
bf16 = jnp.bfloat16
i32 = jnp.int32

LANES = 128
SUBLANES = 8
VMEM_LIMIT = 56 * 1024 * 1024

EPS = 1e-6
D = 1024
GRID_W = 64
GDN_H = 4
GDN_DK = 128
GDN_CHUNK = 64
A_QK = GDN_H * GDN_DK
FN_W = 512
MLA_H = 8
MLA_QL = 512
MLA_KVL = 256
MLA_NOPE = 128
MLA_ROPE = 64
MLA_QK = MLA_NOPE + MLA_ROPE
ROPE_BASE = 10000.0
PEER_H = 8
PEER_NK = 128
PEER_K = 16
PEER_SEL = PEER_H * PEER_K
HALF = SUBLANES // 2
HI_MASK = -65536
PACK_TE = 256
GATHER_TB = 128
U_SLOTS = 16
U_GROUPS = PEER_SEL // U_SLOTS

HI = lax.Precision.HIGHEST
NT = (((1,), (1,)), ((), ()))
TN = (((0,), (0,)), ((), ()))
NN = (((1,), (0,)), ((), ()))


def _dot(a, b, dims=NN, precision=None):
    return lax.dot_general(a, b, dims, preferred_element_type=f32, precision=precision)


def _split(a):
    hi = a.astype(bf16)
    return hi, (a - hi.astype(f32)).astype(bf16)


def _split_mm(parts, w_hi, w_lo):
    rows = parts[0][0].shape[0]
    n = len(parts)
    his = [p[0] for p in parts]
    top = _dot(jnp.concatenate(his + [p[1] for p in parts], axis=0), w_hi)
    bot = _dot(jnp.concatenate(his, axis=0), w_lo) if n > 1 else _dot(his[0], w_lo)
    return [top[i * rows:(i + 1) * rows] + top[(n + i) * rows:(n + i + 1) * rows] + bot[i * rows:(i + 1) * rows]
            for i in range(n)]


def _cp(*sem):
    return pltpu.CompilerParams(dimension_semantics=sem, vmem_limit_bytes=VMEM_LIMIT)


def _mod_row(i, prompt_tiles, tiles_per_seq):
    return jnp.where(i < prompt_tiles, 0, 1 + (i - prompt_tiles) // tiles_per_seq)


def _mods_body(c_ref, w_ref, b_ref, o_ref):
    c = c_ref[...]
    o_ref[...] = _dot(c * jax.nn.sigmoid(c), w_ref[...], precision=HI) + b_ref[...]


def ada_mods_all(cv, w_mod, b_mod):
    depth, _, n6 = w_mod.shape
    r = cv.shape[0]
    tn = 1536
    return pl.pallas_call(
        _mods_body,
        grid=(depth, n6 // tn),
        in_specs=[pl.BlockSpec((r, D), lambda l, j: (0, 0)),
                  pl.BlockSpec((None, D, tn), lambda l, j: (l, 0, j)),
                  pl.BlockSpec((None, 1, tn), lambda l, j: (l, 0, j))],
        out_specs=pl.BlockSpec((None, r, tn), lambda l, j: (l, 0, j)),
        out_shape=jax.ShapeDtypeStruct((depth, r, n6), f32),
        compiler_params=_cp("arbitrary", "arbitrary"),
        name="ada_mods",
    )(cv, w_mod, b_mod.reshape(depth, 1, n6))


def _normmm_body(x_ref, m_ref, nw_ref, w_ref, *rest, shift_idx, split):
    if split:
        wlo_ref, rest = rest[0], rest[1:]
    o_ref, h_refs = rest[0], rest[1:]
    x = x_ref[...]
    y = x * lax.rsqrt(jnp.mean(x * x, -1, keepdims=True) + EPS) * nw_ref[...]
    h = y * (1.0 + m_ref[shift_idx + 1:shift_idx + 2, :]) + m_ref[shift_idx:shift_idx + 1, :]
    hb = h.astype(bf16)
    acc = _dot(hb, w_ref[...])
    if split:
        acc = acc + _dot((h - hb.astype(f32)).astype(bf16), w_ref[...]) + _dot(hb, wlo_ref[...])
    o_ref[...] = acc
    if h_refs:
        h_refs[0][...] = h


def normmm(x, mods_l, shift_idx, norm_w, w, *, prompt_rows, seq_len, tm=256, with_h=False):
    n = x.shape[0]
    nout = w.shape[1]
    split = w.dtype == f32
    ws = [w]
    if split:
        w_hi = w.astype(bf16)
        ws = [w_hi, (w - w_hi.astype(f32)).astype(bf16)]
    pt, tps = prompt_rows // tm, seq_len // tm
    out_shape = [jax.ShapeDtypeStruct((n, nout), f32)]
    out_specs = [pl.BlockSpec((tm, nout), lambda i: (i, 0))]
    if with_h:
        out_shape.append(jax.ShapeDtypeStruct((n, D), f32))
        out_specs.append(pl.BlockSpec((tm, D), lambda i: (i, 0)))
    res = pl.pallas_call(
        functools.partial(_normmm_body, shift_idx=shift_idx, split=split),
        grid=(n // tm,),
        in_specs=[pl.BlockSpec((tm, D), lambda i: (i, 0)),
                  pl.BlockSpec((None, 6, D), lambda i: (_mod_row(i, pt, tps), 0, 0)),
                  pl.BlockSpec((1, D), lambda i: (0, 0))] + [pl.BlockSpec((D, nout), lambda i: (0, 0)) for _ in ws],
        out_specs=out_specs,
        out_shape=out_shape,
        compiler_params=_cp("arbitrary"),
        name="normmm",
    )(x, mods_l, norm_w.reshape(1, D), *ws)
    return res if with_h else res[0]


def _outproj_body(*refs, n_in, gate_idx):
    x_ref, m_ref = refs[0], refs[1]
    a_refs = refs[2:2 + n_in]
    w_refs = refs[2 + n_in:2 + 2 * n_in]
    o_ref = refs[2 + 2 * n_in]
    y = _dot(a_refs[0][...].astype(bf16), w_refs[0][...])
    for a_ref, w_ref in zip(a_refs[1:], w_refs[1:]):
        y = y + _dot(a_ref[...].astype(bf16), w_ref[...])
    o_ref[...] = x_ref[...] + m_ref[gate_idx:gate_idx + 1, :] * y


def outproj(x, mods_l, gate_idx, acts, ws, *, prompt_rows, seq_len, tm=256):
    n = x.shape[0]
    pt, tps = prompt_rows // tm, seq_len // tm
    in_specs = [pl.BlockSpec((tm, D), lambda i: (i, 0)),
                pl.BlockSpec((None, 6, D), lambda i: (_mod_row(i, pt, tps), 0, 0))]
    in_specs += [pl.BlockSpec((tm, a.shape[1]), lambda i: (i, 0)) for a in acts]
    in_specs += [pl.BlockSpec(w.shape, lambda i: (0, 0)) for w in ws]
    return pl.pallas_call(
        functools.partial(_outproj_body, n_in=len(acts), gate_idx=gate_idx),
        grid=(n // tm,),
        in_specs=in_specs,
        out_specs=pl.BlockSpec((tm, D), lambda i: (i, 0)),
        out_shape=jax.ShapeDtypeStruct((n, D), f32),
        compiler_params=_cp("arbitrary"),
        name="outproj",
    )(x, mods_l, *acts, *ws)


def _softplus(x):
    return jnp.maximum(x, 0.0) + jnp.log1p(jnp.exp(-jnp.abs(x)))


def _gdn_body(nega_ref, dt_ref, q_ref, k_ref, v_ref, gate_ref, ba_ref, cwq_ref, cwk_ref, cwv_ref, onw_ref,
              *rest, seq, has_s0):
    if has_s0:
        s0_ref, rest = rest[0], rest[1:]
    out_ref, sout_ref, qs, ks, vs, us, ws, qes, kes, qks, egs = rest
    h = pl.program_id(1)
    C = GDN_CHUNK
    nch = seq // C

    row = lax.broadcasted_iota(i32, (seq, LANES), 0)
    first = row == 0
    last = row == seq - 1

    def conv_silu(x_ref, w_ref):
        x = x_ref[...]
        w = w_ref[...]
        prev = jnp.where(first, 0.0, pltpu.roll(x, 1, 0))
        nxt = jnp.where(last, 0.0, pltpu.roll(x, seq - 1, 0))
        y = prev * w[0:1, :] + x * w[1:2, :] + nxt * w[2:3, :]
        return y * jax.nn.sigmoid(y)

    def l2n(x):
        return x * lax.rsqrt(jnp.sum(x * x, -1, keepdims=True) + EPS)

    qs[...] = l2n(conv_silu(q_ref, cwq_ref)) * (GDN_DK ** -0.5)
    ks[...] = l2n(conv_silu(k_ref, cwk_ref))
    vs[...] = conv_silu(v_ref, cwv_ref)

    ii = lax.broadcasted_iota(i32, (C, C), 0)
    jj = lax.broadcasted_iota(i32, (C, C), 1)
    eye = (ii == jj).astype(f32)

    row_c = lax.broadcasted_iota(i32, (C, LANES), 0)
    NB = 4
    same_block = (lax.broadcasted_iota(i32, (NB * C, NB * C), 0) // C
                  == lax.broadcasted_iota(i32, (NB * C, NB * C), 1) // C)

    def block_diag(m_cat):
        return jnp.where(same_block, jnp.concatenate([m_cat] * NB, axis=0), jnp.zeros((), m_cat.dtype))

    def chunk_algebra(cp):
        a_blocks, rhs_blocks, dest = [], [], []
        for cc in range(2):
            r0 = pl.multiple_of((2 * cp + cc) * C, C)
            qc = qs[pl.ds(r0, C), :]
            kc = ks[pl.ds(r0, C), :]
            vc = vs[pl.ds(r0, C), :]
            ba = ba_ref[pl.ds(r0, C), :]
            betas = [jax.nn.sigmoid(ba[:, d:d + 1]) for d in range(2)]
            kbs = [kc * b for b in betas]
            aq = _dot(jnp.concatenate(kbs + [qc], axis=0).astype(bf16), kc.astype(bf16), NT)
            for d in range(2):
                rel = (ii - jj) if d == 0 else (jj - ii)
                m_incl = rel >= 0
                g = nega_ref[d, h] * _softplus(ba[:, 2 + d:3 + d] + dt_ref[d, h])
                gcB = jnp.broadcast_to(g, (C, LANES))
                for sh in (1, 2, 4, 8, 16, 32):
                    if d == 0:
                        gcB = gcB + jnp.where(row_c >= sh, pltpu.roll(gcB, sh, 0), 0.0)
                    else:
                        gcB = gcB + jnp.where(row_c < C - sh, pltpu.roll(gcB, C - sh, 0), 0.0)
                gc = gcB[:, 0:1]
                gc_row = gcB.T[:C, :]
                decay = jnp.exp(jnp.where(m_incl, gcB[:, :C] - gc_row, -jnp.inf))
                a_blocks.append(jnp.where(rel > 0, aq[d * C:(d + 1) * C] * decay, 0.0))
                qks[d, pl.ds(r0, C), 0:C] = jnp.where(m_incl, aq[2 * C:] * decay, 0.0)
                rhs_blocks.append(jnp.concatenate([vc * betas[d], kbs[d] * jnp.exp(gc)], axis=1))
                dest.append((d, r0))
                qes[d, pl.ds(r0, C), :] = qc * jnp.exp(gc)
                g_last = gcB[C - 1:C, :] if d == 0 else gcB[0:1, :]
                kes[d, pl.ds(r0, C), :] = kc * jnp.exp(g_last - gcB)
                egs[d, pl.ds(pl.multiple_of((2 * cp + cc) * SUBLANES, SUBLANES), SUBLANES), :] = jnp.broadcast_to(
                    jnp.exp(g_last), (SUBLANES, LANES))
        return jnp.concatenate(a_blocks, axis=1), rhs_blocks, dest

    GROUPS = min(4, nch // 2)

    def prep_step(it, carry):
        groups = [chunk_algebra(GROUPS * it + gi) for gi in range(GROUPS)]
        eyes = jnp.concatenate([eye] * NB, axis=1)
        Xs = [eyes - g[0] for g in groups]
        pps = [_split(g[0]) for g in groups]
        Ps = [_split_mm([pp], block_diag(pp[0]), block_diag(pp[1]))[0] for pp in pps]
        for _ in range(4):
            pps = [_split(P) for P in Ps]
            res = [_split_mm([_split(X), pp], block_diag(pp[0]), block_diag(pp[1])) for X, pp in zip(Xs, pps)]
            Xs = [X + r[0] for X, r in zip(Xs, res)]
            Ps = [r[1] for r in res]
        pps = [_split(P) for P in Ps]
        Xs = [X + _split_mm([_split(X)], block_diag(pp[0]), block_diag(pp[1]))[0] for X, pp in zip(Xs, pps)]
        for X, (_, rhs_blocks, dest) in zip(Xs, groups):
            x_hi, x_lo = _split(X)
            for k, (d, r0) in enumerate(dest):
                xs = slice(k * C, (k + 1) * C)
                (sol,) = _split_mm([(x_hi[:, xs], x_lo[:, xs])], *_split(rhs_blocks[k]))
                us[d, pl.ds(r0, C), :] = sol[:, :GDN_DK]
                ws[d, pl.ds(r0, C), :] = sol[:, GDN_DK:]
        return carry

    lax.fori_loop(0, nch // (2 * GROUPS), prep_step, 0)

    def step(i, carry):
        cs = (i, nch - 1 - i)
        r0s = [pl.multiple_of(c * C, C) for c in cs]
        wqs = [_dot(jnp.concatenate([ws[d, pl.ds(r0s[d], C), :], qes[d, pl.ds(r0s[d], C), :]], axis=0).astype(bf16),
                    carry[d].astype(bf16)) for d in range(2)]
        vbs = [(us[d, pl.ds(r0s[d], C), :] - wqs[d][:C]).astype(bf16) for d in range(2)]
        outs = [wqs[d][C:] + _dot(qks[d, pl.ds(r0s[d], C), 0:C].astype(bf16), vbs[d]) for d in range(2)]
        upd = [_dot(kes[d, pl.ds(r0s[d], C), :].astype(bf16), vbs[d], TN) for d in range(2)]
        new = []
        for d in range(2):
            us[d, pl.ds(r0s[d], C), :] = outs[d]
            eg = egs[d, pl.ds(pl.multiple_of(cs[d] * SUBLANES, SUBLANES), 1), :]
            new.append(carry[d] * eg + upd[d])
        return tuple(new)

    if has_s0:
        S0 = (s0_ref[0], s0_ref[1])
    else:
        S0 = (jnp.zeros((GDN_DK, GDN_DK), f32),) * 2
    Sf, Sb = lax.fori_loop(0, nch, step, S0)
    sout_ref[0] = Sf
    sout_ref[1] = Sb

    o = us[0] + us[1]
    y = o * lax.rsqrt(jnp.mean(o * o, -1, keepdims=True) + EPS) * onw_ref[...]
    gt = gate_ref[...]
    out_ref[...] = y * (gt * jax.nn.sigmoid(gt))


def gdn(p, nega, dtb, conv_w, onw, s0, *, row_off, batch, seq, ba_col0):
    rb = row_off // seq
    H = GDN_H
    nch = seq // GDN_CHUNK

    def col(cb):
        return pl.BlockSpec((seq, LANES), lambda b, h: (rb + b, cb + h), pipeline_mode=pl.Buffered(1))

    state_spec = pl.BlockSpec((None, 2, None, GDN_DK, GDN_DK), lambda b, h: (b, 0, h, 0, 0))
    in_specs = [pl.BlockSpec(memory_space=pltpu.SMEM), pl.BlockSpec(memory_space=pltpu.SMEM),
                col(0), col(H), col(2 * H), col(3 * H), col(ba_col0),
                pl.BlockSpec((3, LANES), lambda b, h: (0, h)),
                pl.BlockSpec((3, LANES), lambda b, h: (0, H + h)),
                pl.BlockSpec((3, LANES), lambda b, h: (0, 2 * H + h)),
                pl.BlockSpec((1, LANES), lambda b, h: (0, 0))]
    args = [nega, dtb, p, p, p, p, p, conv_w, conv_w, conv_w, onw.reshape(1, LANES)]
    if s0 is not None:
        in_specs.append(state_spec)
        args.append(s0)
    seq_buf = pltpu.VMEM((seq, LANES), f32)
    dir_buf = pltpu.VMEM((2, seq, LANES), f32)
    return pl.pallas_call(
        functools.partial(_gdn_body, seq=seq, has_s0=s0 is not None),
        grid=(batch, H),
        in_specs=in_specs,
        out_specs=[pl.BlockSpec((seq, LANES), lambda b, h: (b, h)), state_spec],
        out_shape=[jax.ShapeDtypeStruct((batch * seq, H * LANES), f32),
                   jax.ShapeDtypeStruct((batch, 2, H, GDN_DK, GDN_DK), f32)],
        scratch_shapes=[seq_buf, seq_buf, seq_buf, dir_buf, dir_buf, dir_buf, dir_buf, dir_buf,
                        pltpu.VMEM((2, nch * SUBLANES, LANES), f32)],
        compiler_params=_cp("arbitrary", "arbitrary"),
        name="gdn",
    )(*args)


def _fnet_body(x_ref, cd_ref, f_ref, o_ref, z_scr, *, seq):
    r = pl.program_id(1)

    @pl.when(r == 0)
    def _stage1():
        for g in range(FN_W // LANES):
            xg = x_ref[:, g * LANES:(g + 1) * LANES].astype(bf16)
            y = _dot(xg, cd_ref[...])
            z_scr[0:seq, g * LANES:(g + 1) * LANES] = y[:, :LANES].astype(bf16)
            z_scr[seq:2 * seq, g * LANES:(g + 1) * LANES] = y[:, LANES:].astype(bf16)

    scale = 1.0 / math.sqrt(seq * LANES)
    o_ref[...] = _dot(f_ref[...], z_scr[...]) * scale


def fnet(p, cd, fmat, *, row_off, batch, seq, col_block, tr):
    rb = row_off // seq
    return pl.pallas_call(
        functools.partial(_fnet_body, seq=seq),
        grid=(batch, seq // tr),
        in_specs=[pl.BlockSpec((seq, FN_W), lambda b, r: (rb + b, col_block)),
                  pl.BlockSpec((LANES, 2 * LANES), lambda b, r: (0, 0)),
                  pl.BlockSpec((tr, 2 * seq), lambda b, r: (r, 0))],
        out_specs=pl.BlockSpec((tr, FN_W), lambda b, r: (b * (seq // tr) + r, 0)),
        out_shape=jax.ShapeDtypeStruct((batch * seq, FN_W), f32),
        scratch_shapes=[pltpu.VMEM((2 * seq, FN_W), bf16)],
        compiler_params=_cp("arbitrary", "arbitrary"),
        name="fnet",
    )(p, cd, fmat)


def _dft_mats(n):
    k = jnp.arange(n, dtype=i32)
    ang = ((k[:, None] * k[None, :]) % n).astype(f32) * (2.0 * math.pi / n)
    return jnp.cos(ang), jnp.sin(ang)


def _rope(x, cos, sin_signed, first_half):
    partner = jnp.where(first_half, pltpu.roll(x, LANES - MLA_ROPE // 4, 1), pltpu.roll(x, MLA_ROPE // 4, 1))
    return x * cos + partner * sin_signed


def _mla_body(*refs, has_q, normalize, use_rope):
    it = iter(refs)
    if normalize:
        p_ref = next(it)
        qnw_ref = next(it)
        kvnw_ref = next(it)
    else:
        ckv_ref = next(it)
        krp_ref = next(it)
    if has_q:
        wqn_ref, wqr_ref, qhn_ref, qhr_ref = next(it), next(it), next(it), next(it)
    wkn_ref, wv_ref, khn_ref, khr_ref = next(it), next(it), next(it), next(it)
    if use_rope:
        cos_ref, sin_ref = next(it), next(it)
    if has_q:
        qf_ref = next(it)
    kf_ref, v_ref = next(it), next(it)
    if normalize:
        ckv_out, krp_out = next(it), next(it)

    def rms(x):
        return x * lax.rsqrt(jnp.mean(x * x, -1, keepdims=True) + EPS)

    if normalize:
        p = p_ref[...]
        cq = rms(p[:, :MLA_QL]) * qnw_ref[...]
        ckv = rms(p[:, MLA_QL:MLA_QL + MLA_KVL]) * kvnw_ref[...]
        krp = p[:, MLA_QL + MLA_KVL:]
        ckv_out[...] = ckv
        krp_out[...] = krp
    else:
        ckv = ckv_ref[...]
        krp = krp_ref[...]
    if use_rope:
        cos = cos_ref[...]
        sin = sin_ref[...]
        lane = lax.broadcasted_iota(i32, cos.shape, 1)
        first_half = (lane % (MLA_ROPE // 2)) < (MLA_ROPE // 4)

    def head_norm(a, b, wn, wr, scale):
        ss = jnp.sum(a * a, -1, keepdims=True) + jnp.sum(b * b, -1, keepdims=True)
        r = lax.rsqrt(ss * (1.0 / MLA_QK) + EPS)
        a = a * r * wn
        b = b * r * wr
        if use_rope:
            b = _rope(b, cos, sin, first_half)
        return (a * scale).astype(bf16), (b * scale).astype(bf16)

    if has_q:
        cqb = cq.astype(bf16)
        qn = _dot(cqb, wqn_ref[...])
        qr = _dot(cqb, wqr_ref[...])
        for h in range(MLA_H):
            a, b = head_norm(qn[:, h * LANES:(h + 1) * LANES], qr[:, h * LANES:(h + 1) * LANES],
                             qhn_ref[...], qhr_ref[...], MLA_QK ** -0.5)
            qf_ref[:, 2 * h * LANES:(2 * h + 1) * LANES] = a
            qf_ref[:, (2 * h + 1) * LANES:(2 * h + 2) * LANES] = b
    ckvb = ckv.astype(bf16)
    kn = _dot(ckvb, wkn_ref[...])
    v_ref[...] = _dot(ckvb, wv_ref[...]).astype(bf16)
    for h in range(MLA_H):
        a, b = head_norm(kn[:, h * LANES:(h + 1) * LANES], krp, khn_ref[...], khr_ref[...], 1.0)
        kf_ref[:, 2 * h * LANES:(2 * h + 1) * LANES] = a
        kf_ref[:, (2 * h + 1) * LANES:(2 * h + 2) * LANES] = b


def mla_prep(srcs, norm_ws, q_ws, kv_ws, rope_tabs, *, row_off, rows, seq, tm=256):
    normalize = norm_ws is not None
    has_q = q_ws is not None
    use_rope = rope_tabs is not None
    ro = row_off // tm
    nt = rows // tm
    hd = MLA_H * 2 * LANES

    def full(a):
        return pl.BlockSpec(a.shape, lambda i: (0,) * a.ndim)

    args, in_specs = [], []
    for s in srcs:
        args.append(s)
        in_specs.append(pl.BlockSpec((tm, s.shape[1]), lambda i: (ro + i, 0)))
    for group in (norm_ws, q_ws, kv_ws):
        if group is not None:
            for a in group:
                args.append(a)
                in_specs.append(full(a))
    if use_rope:
        tps = seq // tm
        for a in rope_tabs:
            args.append(a)
            in_specs.append(pl.BlockSpec((tm, LANES), lambda i: (i % tps, 0)))
    out_shape, out_specs = [], []

    def out(cols, dt):
        out_shape.append(jax.ShapeDtypeStruct((rows, cols), dt))
        out_specs.append(pl.BlockSpec((tm, cols), lambda i: (i, 0)))

    if has_q:
        out(hd, bf16)
    out(hd, bf16)
    out(MLA_H * LANES, bf16)
    if normalize:
        out(MLA_KVL, f32)
        out(LANES, f32)
    return pl.pallas_call(
        functools.partial(_mla_body, has_q=has_q, normalize=normalize, use_rope=use_rope),
        grid=(nt,),
        in_specs=in_specs,
        out_specs=out_specs,
        out_shape=out_shape,
        compiler_params=_cp("arbitrary"),
        name="mla_prep",
    )(*args)


ATTN_KCHUNK = 1024


def _attn_body(*refs, n_kv):
    q_ref = refs[0]
    k_refs = refs[1:1 + n_kv]
    v_refs = refs[1 + n_kv:1 + 2 * n_kv]
    o_ref = refs[1 + 2 * n_kv]
    q = q_ref[...]
    chunks = [(k_ref, v_ref, c0, min(c0 + ATTN_KCHUNK, k_ref.shape[0]))
              for k_ref, v_ref in zip(k_refs, v_refs) for c0 in range(0, k_ref.shape[0], ATTN_KCHUNK)]

    def scores(ch):
        return _dot(q, ch[0][ch[2]:ch[3], :], NT)

    s_next = scores(chunks[0])
    m = l = acc = None
    for c, ch in enumerate(chunks):
        s = s_next
        if c + 1 < len(chunks):
            s_next = scores(chunks[c + 1])
        ms = jnp.max(s, -1, keepdims=True)
        v = ch[1][ch[2]:ch[3], :]
        if m is None:
            m = ms
            e = jnp.exp(s - m)
            l = jnp.sum(e, -1, keepdims=True)
            acc = _dot(e.astype(bf16), v)
        else:
            m_new = jnp.maximum(m, ms)
            scale = jnp.exp(m - m_new)
            e = jnp.exp(s - m_new)
            l = l * scale + jnp.sum(e, -1, keepdims=True)
            acc = acc * scale + _dot(e.astype(bf16), v)
            m = m_new
    o_ref[...] = acc / l


def attend(qf, kfs, vs, *, batch, lq, lks, tq=256):
    n_kv = len(kfs)
    nq = lq // tq
    in_specs = [pl.BlockSpec((tq, 2 * LANES), lambda b, h, i: (b * nq + i, h))]
    in_specs += [pl.BlockSpec((lk, 2 * LANES), lambda b, h, i: (b, h)) for lk in lks]
    in_specs += [pl.BlockSpec((lk, LANES), lambda b, h, i: (b, h)) for lk in lks]
    return pl.pallas_call(
        functools.partial(_attn_body, n_kv=n_kv),
        grid=(batch, MLA_H, nq),
        in_specs=in_specs,
        out_specs=pl.BlockSpec((tq, LANES), lambda b, h, i: (b * nq + i, h)),
        out_shape=jax.ShapeDtypeStruct((batch * lq, MLA_H * LANES), f32),
        compiler_params=_cp("arbitrary", "arbitrary", "arbitrary"),
        name="attend",
    )(qf, *kfs, *vs)


def _topk_rows(s, k, val_scr, idx_scr, payload=None):
    rows = s.shape[0]
    iota = lax.broadcasted_iota(i32, s.shape, 0)
    for r in range(k):
        m = jnp.max(s, axis=0, keepdims=True)
        idx = jnp.min(jnp.where(s == m, iota, rows), axis=0, keepdims=True)
        hit = iota == idx
        val_scr[r:r + 1, :] = m
        if payload is None:
            idx_scr[r:r + 1, :] = idx
        else:
            idx_scr[r:r + 1, :] = jnp.max(jnp.where(hit, payload, -1), axis=0, keepdims=True)
        s = jnp.where(hit, -jnp.inf, s)


def _pair_candidates(sv0, sv1, si0, si1):
    K = PEER_K
    sub = lax.broadcasted_iota(i32, (SUBLANES, sv0.shape[1]), 0)
    vals, idxs = [], []
    for a in range(SUBLANES):
        nb = K // (a + 1)
        for b0 in range(0, nb, SUBLANES):
            v = sv0[a:a + 1, :] + sv1[b0:b0 + SUBLANES, :]
            e = si0[a:a + 1, :] * PEER_NK + si1[b0:b0 + SUBLANES, :]
            if nb - b0 < SUBLANES:
                v = jnp.where(sub < nb - b0, v, -jnp.inf)
            vals.append(v)
            idxs.append(e)
    vals.append(sv0[SUBLANES:K, :] + sv1[0:1, :])
    idxs.append(si0[SUBLANES:K, :] * PEER_NK + si1[0:1, :])
    return jnp.concatenate(vals, axis=0), jnp.concatenate(idxs, axis=0)


def _u_slot_rows(s):
    return SUBLANES * (s // SUBLANES) + HALF * (s % 2) + (s % SUBLANES) // 2


def _peer_topk_body(q_ref, keys_ref, cu_ref, gate_ref, sv, si, cv, ci, e_all, g_all):
    K = PEER_K
    tt = q_ref.shape[0]
    for h in range(PEER_H):
        for p in range(2):
            qhp = q_ref[:, (2 * h + p) * LANES:(2 * h + p + 1) * LANES]
            s = _dot(keys_ref[h, p], qhp, NT, precision=HI)
            _topk_rows(s, K, sv.at[p], si.at[p])
        cand, cidx = _pair_candidates(sv[0], sv[1], si[0], si[1])
        _topk_rows(cand, K, cv, ci, payload=cidx)
        cs = cv[...]
        e = jnp.exp(cs - cs[0:1, :])
        g_all[h * K:(h + 1) * K, :] = e / jnp.sum(e, axis=0, keepdims=True)
        code = (ci[...] + PACK_TE) * HALF
        for b in range(tt // GATHER_TB):
            e_all[b, h * K:(h + 1) * K, :] = code[:, b * GATHER_TB:(b + 1) * GATHER_TB]
    gate_ref[...] = g_all[...].T
    for b in range(tt // GATHER_TB):
        for s in range(U_SLOTS):
            cu_ref[s, b] = e_all[b, pl.ds(_u_slot_rows(s), U_GROUPS, stride=U_SLOTS), :] - HALF * (s % 2)


def peer_topk(q, sub_keys, *, tt=512):
    n = q.shape[0]
    K = PEER_K
    nb = tt // GATHER_TB
    return pl.pallas_call(
        _peer_topk_body,
        grid=(n // tt,),
        in_specs=[pl.BlockSpec((tt, q.shape[1]), lambda i: (i, 0)),
                  pl.BlockSpec(sub_keys.shape, lambda i: (0, 0, 0, 0))],
        out_specs=[pl.BlockSpec((U_SLOTS, nb, U_GROUPS, GATHER_TB), lambda i: (0, i, 0, 0)),
                   pl.BlockSpec((tt, PEER_SEL), lambda i: (i, 0))],
        out_shape=[jax.ShapeDtypeStruct((U_SLOTS, n // GATHER_TB, U_GROUPS, GATHER_TB), i32),
                   jax.ShapeDtypeStruct((n, PEER_SEL), f32)],
        scratch_shapes=[pltpu.VMEM((2, K, tt), f32), pltpu.VMEM((2, K, tt), i32),
                        pltpu.VMEM((K, tt), f32), pltpu.VMEM((K, tt), i32),
                        pltpu.VMEM((nb, PEER_SEL, GATHER_TB), i32), pltpu.VMEM((PEER_SEL, tt), f32)],
        compiler_params=_cp("arbitrary"),
        name="peer_topk",
    )(q, sub_keys)


def _pack_body(x_ref, o_ref):
    i = pl.program_id(0)
    guard = jnp.logical_or(i == 0, i == pl.num_programs(0) - 1)

    @pl.when(guard)
    def _zero():
        o_ref[...] = jnp.zeros_like(o_ref)

    @pl.when(jnp.logical_not(guard))
    def _pack():
        half = x_ref.shape[1] // 2
        lo = lax.bitcast_convert_type(x_ref[:, :half].astype(bf16).astype(f32), i32)
        hi = lax.bitcast_convert_type(x_ref[:, half:].astype(bf16).astype(f32), i32)
        words = (hi & jnp.int32(HI_MASK)) | lax.shift_right_logical(lo, jnp.int32(16))
        for s in range(HALF):
            o_ref[pl.ds(s, PACK_TE, stride=HALF), :] = words[:, s * LANES:(s + 1) * LANES]


def pack_expert_table(tab):
    e, d = tab.shape
    nb = e // PACK_TE
    return pl.pallas_call(
        _pack_body,
        grid=(nb + 2,),
        in_specs=[pl.BlockSpec((PACK_TE, d), lambda i: (jnp.clip(i - 1, 0, nb - 1), 0))],
        out_specs=pl.BlockSpec((PACK_TE * HALF, LANES), lambda i: (i, 0)),
        out_shape=jax.ShapeDtypeStruct(((nb + 2) * PACK_TE * HALF, LANES), i32),
        compiler_params=_cp("arbitrary"),
        name="pack_table",
    )(tab)


def _merged(tab_ref, ca, cb, lo_half):
    a = tab_ref[pl.ds(pl.multiple_of(ca, HALF), SUBLANES), :]
    b = tab_ref[pl.ds(pl.multiple_of(cb, HALF), SUBLANES), :]
    return jnp.where(lo_half, a, b)


def _lo(words):
    return lax.bitcast_convert_type(words << 16, f32)


def _hi(words):
    return lax.bitcast_convert_type(words & jnp.int32(HI_MASK), f32)


def _gelu_tanh(x):
    return 0.5 * x * (1.0 + jnp.tanh(math.sqrt(2.0 / math.pi) * (x + 0.044715 * (x * x * x))))


def _fold4(ms, sub):
    lo2 = (sub % 4) < 2
    b = [jnp.where(lo2, ms[k] + pltpu.roll(ms[k], SUBLANES - 2, 0), ms[k + 2] + pltpu.roll(ms[k + 2], 2, 0))
         for k in range(2)]
    lo1 = (sub % 2) < 1
    return jnp.where(lo1, b[0] + pltpu.roll(b[0], SUBLANES - 1, 0), b[1] + pltpu.roll(b[1], 1, 0))


def _peer_u_body(*refs, tb):
    c_refs = refs[:U_SLOTS]
    gate_ref, h_ref, tab_ref, wt_ref, z_ref, h8_ref, w_ref = refs[U_SLOTS:]
    sub = lax.broadcasted_iota(i32, (SUBLANES, LANES), 0)
    lo_half = sub < HALF
    for s in range(SUBLANES):
        h8_ref[pl.ds(s, tb, stride=SUBLANES), :] = h_ref[:, s * LANES:(s + 1) * LANES]

    def lane_totals(t):
        zt = z_ref[pl.ds(pl.multiple_of(t * PEER_SEL, PEER_SEL), PEER_SEL), :]
        return jnp.sum(zt.T, axis=0, keepdims=True)

    z_ref[0:PEER_SEL, :] = jnp.zeros((PEER_SEL, LANES), f32)

    def token(t, carry):
        prev = jnp.maximum(t - 1, 0)
        w_ref[pl.ds(prev, 1), :] = lane_totals(prev)
        r8 = pl.multiple_of(t * SUBLANES, SUBLANES)
        hrow = h8_ref[pl.ds(r8, SUBLANES), :]
        hswap = pltpu.roll(hrow, HALF, 0)
        hl = jnp.where(lo_half, hrow, hswap)
        hh = jnp.where(lo_half, hswap, hrow)
        for g in range(U_GROUPS):
            for fold in range(2):
                ms = []
                for k in range(4):
                    s0 = fold * SUBLANES + 2 * k
                    words = _merged(tab_ref, c_refs[s0][g * tb + t], c_refs[s0 + 1][g * tb + t], lo_half)
                    ms.append(_lo(words) * hl + _hi(words) * hh)
                row0 = t * PEER_SEL + (2 * g + fold) * SUBLANES
                z_ref[pl.ds(pl.multiple_of(row0, SUBLANES), SUBLANES), :] = _fold4(ms, sub)
        return carry

    lax.fori_loop(0, tb, token, 0)
    w_ref[tb - 1:tb, :] = lane_totals(tb - 1)
    wt_ref[...] = (gate_ref[...] * _gelu_tanh(w_ref[...])).T


def peer_u_gather(codes, gate, h, tab):
    tb = GATHER_TB
    n = gate.shape[0]
    codes = codes.reshape(U_SLOTS, n * U_GROUPS)
    smem = [pl.BlockSpec((tb * U_GROUPS,), lambda i: (i,), memory_space=pltpu.SMEM) for _ in range(U_SLOTS)]
    return pl.pallas_call(
        functools.partial(_peer_u_body, tb=tb),
        grid=(n // tb,),
        in_specs=smem + [pl.BlockSpec((tb, PEER_SEL), lambda i: (i, 0)),
                         pl.BlockSpec((tb, D), lambda i: (i, 0)),
                         pl.BlockSpec(tab.shape, lambda i: (0, 0), pipeline_mode=pl.Buffered(1))],
        out_specs=pl.BlockSpec((None, PEER_SEL, tb), lambda i: (i, 0, 0)),
        out_shape=jax.ShapeDtypeStruct((n // tb, PEER_SEL, tb), f32),
        scratch_shapes=[pltpu.VMEM((tb * PEER_SEL, LANES), f32), pltpu.VMEM((tb * SUBLANES, LANES), f32),
                        pltpu.VMEM((tb, PEER_SEL), f32)],
        compiler_params=_cp("arbitrary"),
        name="peer_u",
    )(*[codes[s] for s in range(U_SLOTS)], gate, h, tab)


def _peer_v_body(*refs, tb):
    c_refs = refs[:U_SLOTS]
    wt_ref, x_ref, m_ref, tab_ref, o_ref, y8_ref, wx_ref = refs[U_SLOTS:]
    sub = lax.broadcasted_iota(i32, (SUBLANES, LANES), 0)
    lo_half = sub < HALF

    def expanded(t):
        return jnp.take_along_axis(wt_ref[...], jnp.full((PEER_SEL, tb), t, i32), axis=1)

    wx_ref[...] = expanded(0)

    def token(t, carry):
        nxt = expanded(jnp.minimum(t + 1, tb - 1))
        accs = [jnp.zeros((SUBLANES, LANES), f32) for _ in range(4)]
        for g in range(U_GROUPS):
            for fold in range(2):
                for k in range(4):
                    s0 = fold * SUBLANES + 2 * k
                    ja = g * U_SLOTS + _u_slot_rows(s0)
                    jb = g * U_SLOTS + _u_slot_rows(s0 + 1)
                    words = _merged(tab_ref, c_refs[s0][g * tb + t], c_refs[s0 + 1][g * tb + t], lo_half)
                    wm = jnp.where(lo_half, jnp.broadcast_to(wx_ref[ja:ja + 1, :], (SUBLANES, LANES)),
                                   jnp.broadcast_to(wx_ref[jb:jb + 1, :], (SUBLANES, LANES)))
                    accs[2 * (k % 2)] = accs[2 * (k % 2)] + wm * _lo(words)
                    accs[2 * (k % 2) + 1] = accs[2 * (k % 2) + 1] + wm * _hi(words)
        lo = accs[0] + accs[2]
        hi = accs[1] + accs[3]
        lo = lo + pltpu.roll(lo, HALF, 0)
        hi = hi + pltpu.roll(hi, HALF, 0)
        y8_ref[pl.ds(pl.multiple_of(t * SUBLANES, SUBLANES), SUBLANES), :] = jnp.where(lo_half, lo, hi)
        wx_ref[...] = nxt
        return carry

    lax.fori_loop(0, tb, token, 0)
    for s in range(SUBLANES):
        cols = slice(s * LANES, (s + 1) * LANES)
        o_ref[:, cols] = x_ref[:, cols] + m_ref[5:6, cols] * y8_ref[pl.ds(s, tb, stride=SUBLANES), :]


def peer_v_gather(codes, wt, x, mods_l, tab, *, prompt_rows, seq_len):
    tb = GATHER_TB
    n = x.shape[0]
    pt, tps = prompt_rows // tb, seq_len // tb
    codes = codes.reshape(U_SLOTS, n * U_GROUPS)
    smem = [pl.BlockSpec((tb * U_GROUPS,), lambda i: (i,), memory_space=pltpu.SMEM) for _ in range(U_SLOTS)]
    return pl.pallas_call(
        functools.partial(_peer_v_body, tb=tb),
        grid=(n // tb,),
        in_specs=smem + [pl.BlockSpec((None, PEER_SEL, tb), lambda i: (i, 0, 0)),
                         pl.BlockSpec((tb, D), lambda i: (i, 0)),
                         pl.BlockSpec((None, 6, D), lambda i: (_mod_row(i, pt, tps), 0, 0)),
                         pl.BlockSpec(tab.shape, lambda i: (0, 0), pipeline_mode=pl.Buffered(1))],
        out_specs=pl.BlockSpec((tb, D), lambda i: (i, 0)),
        out_shape=jax.ShapeDtypeStruct((n, D), f32),
        scratch_shapes=[pltpu.VMEM((tb * SUBLANES, LANES), f32), pltpu.VMEM((PEER_SEL, LANES), f32)],
        compiler_params=_cp("arbitrary"),
        name="peer_v",
    )(*[codes[s] for s in range(U_SLOTS)], wt, x, mods_l, tab)


def _rope_tables(seq):
    rows = seq // GRID_W
    row = jnp.repeat(jnp.arange(rows, dtype=f32), GRID_W)
    col = jnp.tile(jnp.arange(GRID_W, dtype=f32), rows)
    nfreq = MLA_ROPE // 4
    inv = jnp.power(ROPE_BASE, -jnp.arange(nfreq, dtype=f32) / nfreq)
    ar = row[:, None] * inv
    ac = col[:, None] * inv
    pad1 = jnp.ones((seq, LANES - MLA_ROPE), f32)
    pad0 = jnp.zeros((seq, LANES - MLA_ROPE), f32)
    cos = jnp.concatenate([jnp.cos(ar), jnp.cos(ar), jnp.cos(ac), jnp.cos(ac), pad1], -1)
    sin = jnp.concatenate([-jnp.sin(ar), jnp.sin(ar), -jnp.sin(ac), jnp.sin(ac), pad0], -1)
    return cos, sin


def _pad_heads(w, head_w, lo, hi):
    k = w.shape[0]
    w = w.reshape(k, -1, head_w)[:, :, lo:hi]
    return jnp.pad(w, ((0, 0), (0, 0), (0, LANES - (hi - lo)))).reshape(k, -1)


def kernel(x_prompt, x_sample, state_gdn, cache_mla_ckv, cache_mla_krope, c, c_ctx, w_mod, b_mod, norm_mix, norm_ffn, even_w_in, even_conv_w, gdn_a_log, gdn_dt_bias, gdn_o_norm, even_w_out, odd_w_in, mla_q_norm, mla_kv_norm, mla_w_uq, mla_w_ukv, mla_q_headnorm, mla_k_headnorm, odd_w_out, peer_w_q, peer_sub_keys, peer_u, peer_v):
    B, L, _ = x_prompt.shape
    BS, LS, _ = x_sample.shape
    depth = w_mod.shape[0]
    NP, NS = B * L, BS * LS
    past = cache_mla_ckv.shape[2]
    geo = dict(prompt_rows=NP, seq_len=LS)

    x = jnp.concatenate([x_prompt.reshape(NP, D), x_sample.reshape(NS, D)], 0)
    nrow = 1 + BS
    rpad = -nrow % SUBLANES
    cv = jnp.concatenate([c_ctx[None, :], c, jnp.zeros((rpad, D), f32)], 0)
    mods = ada_mods_all(cv, w_mod, b_mod).reshape(depth, nrow + rpad, 6, D)

    cd_c, cd_s = _dft_mats(LANES)
    cd = jnp.concatenate([cd_c, cd_s], 1).astype(bf16)
    fmats = {}
    for n in (L, LS):
        cl, sl = _dft_mats(n)
        fmats[n] = jnp.concatenate([cl, -sl], 1).astype(bf16)
    cos_t, sin_t = _rope_tables(LS)

    new_gdn, new_ckv, new_kr = [], [], []
    for l in range(depth):
        j = l // 2
        m_l = mods[l]
        if l % 2 == 0:
            wi = even_w_in[j]
            o0 = 3 * A_QK + A_QK
            ba = wi[:, o0:o0 + 4 * GDN_H].reshape(D, 4, GDN_H).transpose(0, 2, 1)
            ba = jnp.pad(ba, ((0, 0), (0, 0), (0, LANES - 4))).reshape(D, GDN_H * LANES)
            w_in = jnp.concatenate([wi[:, :o0], wi[:, o0 + 4 * GDN_H:], ba], 1).astype(bf16)
            p = normmm(x, m_l, 0, norm_mix[l], w_in, **geo)
            nega = -jnp.exp(gdn_a_log[j])
            kw = dict(ba_col0=(o0 + FN_W) // LANES)
            mix_p, st = gdn(p, nega, gdn_dt_bias[j], even_conv_w[j], gdn_o_norm[j], None,
                            row_off=0, batch=B, seq=L, **kw)
            mix_s, _ = gdn(p, nega, gdn_dt_bias[j], even_conv_w[j], gdn_o_norm[j], state_gdn[:, j],
                           row_off=NP, batch=BS, seq=LS, **kw)
            new_gdn.append(st)
            fb_p = fnet(p, cd, fmats[L], row_off=0, batch=B, seq=L, col_block=o0 // FN_W, tr=L)
            fb_s = fnet(p, cd, fmats[LS], row_off=NP, batch=BS, seq=LS, col_block=o0 // FN_W, tr=256)
            wo = even_w_out[j].astype(bf16)
            x = outproj(x, m_l, 2, [jnp.concatenate([mix_p, mix_s], 0), jnp.concatenate([fb_p, fb_s], 0)],
                        [wo[:A_QK], wo[A_QK:]], **geo)
        else:
            wi = odd_w_in[j]
            w_in = jnp.pad(wi, ((0, 0), (0, LANES - MLA_ROPE))).astype(bf16)
            p = normmm(x, m_l, 0, norm_mix[l], w_in, **geo)
            norm_ws = (mla_q_norm[j].reshape(1, -1), mla_kv_norm[j].reshape(1, -1))
            qh, kh = mla_q_headnorm[j], mla_k_headnorm[j]

            def split_hw(hw):
                return (hw[:MLA_NOPE].reshape(1, LANES),
                        jnp.pad(hw[MLA_NOPE:], (0, LANES - MLA_ROPE)).reshape(1, LANES))

            q_ws = (_pad_heads(mla_w_uq[j], MLA_QK, 0, MLA_NOPE).astype(bf16),
                    _pad_heads(mla_w_uq[j], MLA_QK, MLA_NOPE, MLA_QK).astype(bf16)) + split_hw(qh)
            kv_ws = (_pad_heads(mla_w_ukv[j], 2 * LANES, 0, LANES).astype(bf16),
                     _pad_heads(mla_w_ukv[j], 2 * LANES, LANES, 2 * LANES).astype(bf16)) + split_hw(kh)
            qf_p, kf_p, v_p, ckv_p, krp_p = mla_prep((p,), norm_ws, q_ws, kv_ws, None, row_off=0, rows=NP, seq=L)
            qf_s, kf_s, v_s, _, _ = mla_prep((p,), norm_ws, q_ws, kv_ws, (cos_t, sin_t), row_off=NP, rows=NS, seq=LS)
            ckv_c = cache_mla_ckv[:, j].reshape(BS * past, MLA_KVL)
            krp_c = jnp.pad(cache_mla_krope[:, j].reshape(BS * past, MLA_ROPE), ((0, 0), (0, LANES - MLA_ROPE)))
            kf_c, v_c = mla_prep((ckv_c, krp_c), None, None, kv_ws, None, row_off=0, rows=BS * past, seq=past)
            o_p = attend(qf_p, [kf_p], [v_p], batch=B, lq=L, lks=[L])
            o_s = attend(qf_s, [kf_s, kf_c], [v_s, v_c], batch=BS, lq=LS, lks=[LS, past])
            new_ckv.append(ckv_p.reshape(B, L, MLA_KVL))
            new_kr.append(krp_p[:, :MLA_ROPE].reshape(B, L, MLA_ROPE))
            x = outproj(x, m_l, 2, [jnp.concatenate([o_p, o_s], 0)], [odd_w_out[j].astype(bf16)], **geo)
        q, h = normmm(x, m_l, 3, norm_ffn[l], peer_w_q[l], with_h=True, **geo)
        codes, gate = peer_topk(q, peer_sub_keys[l])
        wt = peer_u_gather(codes, gate, h, pack_expert_table(peer_u[l]))
        x = peer_v_gather(codes, wt, x, m_l, pack_expert_table(peer_v[l]), **geo)

    y_prompt = x[:NP].reshape(B, L, D)
    y_sample = x[NP:].reshape(BS, LS, D)
    return (y_prompt, y_sample, jnp.stack(new_gdn, 1), jnp.stack(new_ckv, 1), jnp.stack(new_kr, 1))
```

```python
import functools
import math

import jax
import jax.numpy as jnp
from jax import lax
from jax.experimental import pallas as pl
from jax.experimental.pallas import tpu as pltpu

f32 = jnp.float32
bf16 = jnp.bfloat16
i32 = jnp.int32

LANES = 128
SUBLANES = 8
VMEM_LIMIT = 56 * 1024 * 1024

EPS = 1e-6
D = 1024
GRID_W = 64
GDN_H = 4
GDN_DK = 128
GDN_CHUNK = 64
A_QK = GDN_H * GDN_DK
FN_W = 512
MLA_H = 8
MLA_QL = 512
MLA_KVL = 256
MLA_NOPE = 128
MLA_ROPE = 64
MLA_QK = MLA_NOPE + MLA_ROPE
ROPE_BASE = 10000.0
PEER_H = 8
PEER_NK = 128
PEER_K = 16
PEER_SEL = PEER_H * PEER_K
HALF = SUBLANES // 2
HI_MASK = -65536
PACK_TE = 256
GATHER_TB = 128
U_SLOTS = 16
U_GROUPS = PEER_SEL // U_SLOTS

HI = lax.Precision.HIGHEST
NT = (((1,), (1,)), ((), ()))
TN = (((0,), (0,)), ((), ()))
NN = (((1,), (0,)), ((), ()))


def _dot(a, b, dims=NN, precision=None):
    return lax.dot_general(a, b, dims, preferred_element_type=f32, precision=precision)


def _split(a):
    hi = a.astype(bf16)
    return hi, (a - hi.astype(f32)).astype(bf16)


def _split_mm(parts, w_hi, w_lo):
    rows = parts[0][0].shape[0]
    n = len(parts)
    his = [p[0] for p in parts]
    top = _dot(jnp.concatenate(his + [p[1] for p in parts], axis=0), w_hi)
    bot = _dot(jnp.concatenate(his, axis=0), w_lo) if n > 1 else _dot(his[0], w_lo)
    return [top[i * rows:(i + 1) * rows] + top[(n + i) * rows:(n + i + 1) * rows] + bot[i * rows:(i + 1) * rows]
            for i in range(n)]


def _cp(*sem):
    return pltpu.CompilerParams(dimension_semantics=sem, vmem_limit_bytes=VMEM_LIMIT)


def _mod_row(i, prompt_tiles, tiles_per_seq):
    return jnp.where(i < prompt_tiles, 0, 1 + (i - prompt_tiles) // tiles_per_seq)


def _mods_body(c_ref, w_ref, b_ref, o_ref):
    c = c_ref[...]
    o_ref[...] = _dot(c * jax.nn.sigmoid(c), w_ref[...], precision=HI) + b_ref[...]


def ada_mods_all(cv, w_mod, b_mod):
    depth, _, n6 = w_mod.shape
    r = cv.shape[0]
    tn = 1536
    return pl.pallas_call(
        _mods_body,
        grid=(depth, n6 // tn),
        in_specs=[pl.BlockSpec((r, D), lambda l, j: (0, 0)),
                  pl.BlockSpec((None, D, tn), lambda l, j: (l, 0, j)),
                  pl.BlockSpec((None, 1, tn), lambda l, j: (l, 0, j))],
        out_specs=pl.BlockSpec((None, r, tn), lambda l, j: (l, 0, j)),
        out_shape=jax.ShapeDtypeStruct((depth, r, n6), f32),
        compiler_params=_cp("arbitrary", "arbitrary"),
        name="ada_mods",
    )(cv, w_mod, b_mod.reshape(depth, 1, n6))


def _normmm_body(x_ref, m_ref, nw_ref, w_ref, *rest, shift_idx, split):
    if split:
        wlo_ref, rest = rest[0], rest[1:]
    o_ref, h_refs = rest[0], rest[1:]
    x = x_ref[...]
    y = x * lax.rsqrt(jnp.mean(x * x, -1, keepdims=True) + EPS) * nw_ref[...]
    h = y * (1.0 + m_ref[shift_idx + 1:shift_idx + 2, :]) + m_ref[shift_idx:shift_idx + 1, :]
    hb = h.astype(bf16)
    acc = _dot(hb, w_ref[...])
    if split:
        acc = acc + _dot((h - hb.astype(f32)).astype(bf16), w_ref[...]) + _dot(hb, wlo_ref[...])
    o_ref[...] = acc
    if h_refs:
        h_refs[0][...] = h


def normmm(x, mods_l, shift_idx, norm_w, w, *, prompt_rows, seq_len, tm=256, with_h=False):
    n = x.shape[0]
    nout = w.shape[1]
    split = w.dtype == f32
    ws = [w]
    if split:
        w_hi = w.astype(bf16)
        ws = [w_hi, (w - w_hi.astype(f32)).astype(bf16)]
    pt, tps = prompt_rows // tm, seq_len // tm
    out_shape = [jax.ShapeDtypeStruct((n, nout), f32)]
    out_specs = [pl.BlockSpec((tm, nout), lambda i: (i, 0))]
    if with_h:
        out_shape.append(jax.ShapeDtypeStruct((n, D), f32))
        out_specs.append(pl.BlockSpec((tm, D), lambda i: (i, 0)))
    res = pl.pallas_call(
        functools.partial(_normmm_body, shift_idx=shift_idx, split=split),
        grid=(n // tm,),
        in_specs=[pl.BlockSpec((tm, D), lambda i: (i, 0)),
                  pl.BlockSpec((None, 6, D), lambda i: (_mod_row(i, pt, tps), 0, 0)),
                  pl.BlockSpec((1, D), lambda i: (0, 0))] + [pl.BlockSpec((D, nout), lambda i: (0, 0)) for _ in ws],
        out_specs=out_specs,
        out_shape=out_shape,
        compiler_params=_cp("arbitrary"),
        name="normmm",
    )(x, mods_l, norm_w.reshape(1, D), *ws)
    return res if with_h else res[0]


def _outproj_body(*refs, n_in, gate_idx, prompt_tiles):
    x_ref, m_ref = refs[0], refs[1]
    a_refs = refs[2:2 + 2 * n_in]
    w_refs = refs[2 + 2 * n_in:2 + 3 * n_in]
    o_ref = refs[2 + 3 * n_in]
    is_ctx = pl.program_id(0) < prompt_tiles
    y = None
    for k, w_ref in enumerate(w_refs):
        a = jnp.where(is_ctx, a_refs[2 * k][...], a_refs[2 * k + 1][...]).astype(bf16)
        yk = _dot(a, w_ref[...])
        y = yk if y is None else y + yk
    o_ref[...] = x_ref[...] + m_ref[gate_idx:gate_idx + 1, :] * y


def outproj(x, mods_l, gate_idx, act_pairs, ws, *, prompt_rows, seq_len, tm=256):
    n = x.shape[0]
    pt, tps = prompt_rows // tm, seq_len // tm
    st = (n - prompt_rows) // tm
    in_specs = [pl.BlockSpec((tm, D), lambda i: (i, 0)),
                pl.BlockSpec((None, 6, D), lambda i: (_mod_row(i, pt, tps), 0, 0))]
    acts = []
    for a_p, a_s in act_pairs:
        in_specs.append(pl.BlockSpec((tm, a_p.shape[1]), lambda i: (jnp.minimum(i, pt - 1), 0)))
        in_specs.append(pl.BlockSpec((tm, a_s.shape[1]), lambda i: (jnp.clip(i - pt, 0, st - 1), 0)))
        acts += [a_p, a_s]
    in_specs += [pl.BlockSpec(w.shape, lambda i: (0, 0)) for w in ws]
    return pl.pallas_call(
        functools.partial(_outproj_body, n_in=len(ws), gate_idx=gate_idx, prompt_tiles=pt),
        grid=(n // tm,),
        in_specs=in_specs,
        out_specs=pl.BlockSpec((tm, D), lambda i: (i, 0)),
        out_shape=jax.ShapeDtypeStruct((n, D), f32),
        compiler_params=_cp("arbitrary"),
        name="outproj",
    )(x, mods_l, *acts, *ws)


def _softplus(x):
    return jnp.maximum(x, 0.0) + jnp.log1p(jnp.exp(-jnp.abs(x)))


def _gdn_body(nega_ref, dt_ref, q_ref, k_ref, v_ref, gate_ref, ba_ref, cwq_ref, cwk_ref, cwv_ref, onw_ref,
              *rest, seq, has_s0):
    if has_s0:
        s0_ref, rest = rest[0], rest[1:]
    out_ref, sout_ref, qs, ks, vs, us, ws, qes, kes, qks, egs = rest
    h = pl.program_id(1)
    C = GDN_CHUNK
    nch = seq // C

    row = lax.broadcasted_iota(i32, (seq, LANES), 0)
    first = row == 0
    last = row == seq - 1

    def conv_silu(x_ref, w_ref):
        x = x_ref[...]
        w = w_ref[...]
        prev = jnp.where(first, 0.0, pltpu.roll(x, 1, 0))
        nxt = jnp.where(last, 0.0, pltpu.roll(x, seq - 1, 0))
        y = prev * w[0:1, :] + x * w[1:2, :] + nxt * w[2:3, :]
        return y * jax.nn.sigmoid(y)

    def l2n(x):
        return x * lax.rsqrt(jnp.sum(x * x, -1, keepdims=True) + EPS)

    qs[...] = l2n(conv_silu(q_ref, cwq_ref)) * (GDN_DK ** -0.5)
    ks[...] = l2n(conv_silu(k_ref, cwk_ref))
    vs[...] = conv_silu(v_ref, cwv_ref)

    ii = lax.broadcasted_iota(i32, (C, C), 0)
    jj = lax.broadcasted_iota(i32, (C, C), 1)
    eye = (ii == jj).astype(f32)

    row_c = lax.broadcasted_iota(i32, (C, LANES), 0)
    NB = 4
    same_block = (lax.broadcasted_iota(i32, (NB * C, NB * C), 0) // C
                  == lax.broadcasted_iota(i32, (NB * C, NB * C), 1) // C)

    def block_diag(m_cat):
        return jnp.where(same_block, jnp.concatenate([m_cat] * NB, axis=0), jnp.zeros((), m_cat.dtype))

    def chunk_algebra(cp):
        a_blocks, rhs_blocks, dest = [], [], []
        for cc in range(2):
            r0 = pl.multiple_of((2 * cp + cc) * C, C)
            qc = qs[pl.ds(r0, C), :]
            kc = ks[pl.ds(r0, C), :]
            vc = vs[pl.ds(r0, C), :]
            ba = ba_ref[pl.ds(r0, C), :]
            betas = [jax.nn.sigmoid(ba[:, d:d + 1]) for d in range(2)]
            kbs = [kc * b for b in betas]
            aq = _dot(jnp.concatenate(kbs + [qc], axis=0).astype(bf16), kc.astype(bf16), NT)
            for d in range(2):
                rel = (ii - jj) if d == 0 else (jj - ii)
                m_incl = rel >= 0
                g = nega_ref[d, h] * _softplus(ba[:, 2 + d:3 + d] + dt_ref[d, h])
                gcB = jnp.broadcast_to(g, (C, LANES))
                for sh in (1, 2, 4, 8, 16, 32):
                    if d == 0:
                        gcB = gcB + jnp.where(row_c >= sh, pltpu.roll(gcB, sh, 0), 0.0)
                    else:
                        gcB = gcB + jnp.where(row_c < C - sh, pltpu.roll(gcB, C - sh, 0), 0.0)
                gc = gcB[:, 0:1]
                gc_row = gcB.T[:C, :]
                decay = jnp.exp(jnp.where(m_incl, gcB[:, :C] - gc_row, -jnp.inf))
                a_blocks.append(jnp.where(rel > 0, aq[d * C:(d + 1) * C] * decay, 0.0))
                qks[d, pl.ds(r0, C), 0:C] = jnp.where(m_incl, aq[2 * C:] * decay, 0.0)
                rhs_blocks.append(jnp.concatenate([vc * betas[d], kbs[d] * jnp.exp(gc)], axis=1))
                dest.append((d, r0))
                qes[d, pl.ds(r0, C), :] = qc * jnp.exp(gc)
                g_last = gcB[C - 1:C, :] if d == 0 else gcB[0:1, :]
                kes[d, pl.ds(r0, C), :] = kc * jnp.exp(g_last - gcB)
                egs[d, pl.ds(pl.multiple_of((2 * cp + cc) * SUBLANES, SUBLANES), SUBLANES), :] = jnp.broadcast_to(
                    jnp.exp(g_last), (SUBLANES, LANES))
        return jnp.concatenate(a_blocks, axis=1), rhs_blocks, dest

    GROUPS = min(4, nch // 2)

    def prep_step(it, carry):
        groups = [chunk_algebra(GROUPS * it + gi) for gi in range(GROUPS)]
        eyes = jnp.concatenate([eye] * NB, axis=1)
        Xs = [eyes - g[0] for g in groups]
        pps = [_split(g[0]) for g in groups]
        Ps = [_split_mm([pp], block_diag(pp[0]), block_diag(pp[1]))[0] for pp in pps]
        for _ in range(4):
            pps = [_split(P) for P in Ps]
            res = [_split_mm([_split(X), pp], block_diag(pp[0]), block_diag(pp[1])) for X, pp in zip(Xs, pps)]
            Xs = [X + r[0] for X, r in zip(Xs, res)]
            Ps = [r[1] for r in res]
        pps = [_split(P) for P in Ps]
        Xs = [X + _split_mm([_split(X)], block_diag(pp[0]), block_diag(pp[1]))[0] for X, pp in zip(Xs, pps)]
        for X, (_, rhs_blocks, dest) in zip(Xs, groups):
            x_hi, x_lo = _split(X)
            for k, (d, r0) in enumerate(dest):
                xs = slice(k * C, (k + 1) * C)
                (sol,) = _split_mm([(x_hi[:, xs], x_lo[:, xs])], *_split(rhs_blocks[k]))
                us[d, pl.ds(r0, C), :] = sol[:, :GDN_DK]
                ws[d, pl.ds(r0, C), :] = sol[:, GDN_DK:]
        return carry

    lax.fori_loop(0, nch // (2 * GROUPS), prep_step, 0)

    def step(i, carry):
        cs = (i, nch - 1 - i)
        r0s = [pl.multiple_of(c * C, C) for c in cs]
        wqs = [_dot(jnp.concatenate([ws[d, pl.ds(r0s[d], C), :], qes[d, pl.ds(r0s[d], C), :]], axis=0).astype(bf16),
                    carry[d].astype(bf16)) for d in range(2)]
        vbs = [(us[d, pl.ds(r0s[d], C), :] - wqs[d][:C]).astype(bf16) for d in range(2)]
        outs = [wqs[d][C:] + _dot(qks[d, pl.ds(r0s[d], C), 0:C].astype(bf16), vbs[d]) for d in range(2)]
        upd = [_dot(kes[d, pl.ds(r0s[d], C), :].astype(bf16), vbs[d], TN) for d in range(2)]
        new = []
        for d in range(2):
            us[d, pl.ds(r0s[d], C), :] = outs[d]
            eg = egs[d, pl.ds(pl.multiple_of(cs[d] * SUBLANES, SUBLANES), 1), :]
            new.append(carry[d] * eg + upd[d])
        return tuple(new)

    if has_s0:
        S0 = (s0_ref[0], s0_ref[1])
    else:
        S0 = (jnp.zeros((GDN_DK, GDN_DK), f32),) * 2
    Sf, Sb = lax.fori_loop(0, nch, step, S0)
    sout_ref[0] = Sf
    sout_ref[1] = Sb

    o = us[0] + us[1]
    y = o * lax.rsqrt(jnp.mean(o * o, -1, keepdims=True) + EPS) * onw_ref[...]
    gt = gate_ref[...]
    out_ref[...] = y * (gt * jax.nn.sigmoid(gt))


def gdn(p, nega, dtb, conv_w, onw, s0, *, row_off, batch, seq, ba_col0):
    rb = row_off // seq
    H = GDN_H
    nch = seq // GDN_CHUNK

    def col(cb):
        return pl.BlockSpec((seq, LANES), lambda b, h: (rb + b, cb + h), pipeline_mode=pl.Buffered(1))

    state_spec = pl.BlockSpec((None, 2, None, GDN_DK, GDN_DK), lambda b, h: (b, 0, h, 0, 0))
    in_specs = [pl.BlockSpec(memory_space=pltpu.SMEM), pl.BlockSpec(memory_space=pltpu.SMEM),
                col(0), col(H), col(2 * H), col(3 * H), col(ba_col0),
                pl.BlockSpec((3, LANES), lambda b, h: (0, h)),
                pl.BlockSpec((3, LANES), lambda b, h: (0, H + h)),
                pl.BlockSpec((3, LANES), lambda b, h: (0, 2 * H + h)),
                pl.BlockSpec((1, LANES), lambda b, h: (0, 0))]
    args = [nega, dtb, p, p, p, p, p, conv_w, conv_w, conv_w, onw.reshape(1, LANES)]
    if s0 is not None:
        in_specs.append(state_spec)
        args.append(s0)
    seq_buf = pltpu.VMEM((seq, LANES), f32)
    dir_buf = pltpu.VMEM((2, seq, LANES), f32)
    return pl.pallas_call(
        functools.partial(_gdn_body, seq=seq, has_s0=s0 is not None),
        grid=(batch, H),
        in_specs=in_specs,
        out_specs=[pl.BlockSpec((seq, LANES), lambda b, h: (b, h)), state_spec],
        out_shape=[jax.ShapeDtypeStruct((batch * seq, H * LANES), f32),
                   jax.ShapeDtypeStruct((batch, 2, H, GDN_DK, GDN_DK), f32)],
        scratch_shapes=[seq_buf, seq_buf, seq_buf, dir_buf, dir_buf, dir_buf, dir_buf, dir_buf,
                        pltpu.VMEM((2, nch * SUBLANES, LANES), f32)],
        compiler_params=_cp("arbitrary", "arbitrary"),
        name="gdn",
    )(*args)


def _fnet_body(x_ref, cd_ref, f_ref, o_ref, z_scr, *, seq):
    r = pl.program_id(1)

    @pl.when(r == 0)
    def _stage1():
        for g in range(FN_W // LANES):
            xg = x_ref[:, g * LANES:(g + 1) * LANES].astype(bf16)
            y = _dot(xg, cd_ref[...])
            z_scr[0:seq, g * LANES:(g + 1) * LANES] = y[:, :LANES].astype(bf16)
            z_scr[seq:2 * seq, g * LANES:(g + 1) * LANES] = y[:, LANES:].astype(bf16)

    scale = 1.0 / math.sqrt(seq * LANES)
    o_ref[...] = _dot(f_ref[...], z_scr[...]) * scale


def fnet(p, cd, fmat, *, row_off, batch, seq, col_block, tr):
    rb = row_off // seq
    return pl.pallas_call(
        functools.partial(_fnet_body, seq=seq),
        grid=(batch, seq // tr),
        in_specs=[pl.BlockSpec((seq, FN_W), lambda b, r: (rb + b, col_block)),
                  pl.BlockSpec((LANES, 2 * LANES), lambda b, r: (0, 0)),
                  pl.BlockSpec((tr, 2 * seq), lambda b, r: (r, 0))],
        out_specs=pl.BlockSpec((tr, FN_W), lambda b, r: (b * (seq // tr) + r, 0)),
        out_shape=jax.ShapeDtypeStruct((batch * seq, FN_W), f32),
        scratch_shapes=[pltpu.VMEM((2 * seq, FN_W), bf16)],
        compiler_params=_cp("arbitrary", "arbitrary"),
        name="fnet",
    )(p, cd, fmat)


def _dft_mats(n):
    k = jnp.arange(n, dtype=i32)
    ang = ((k[:, None] * k[None, :]) % n).astype(f32) * (2.0 * math.pi / n)
    return jnp.cos(ang), jnp.sin(ang)


def _rope(x, cos, sin_signed, first_half):
    partner = jnp.where(first_half, pltpu.roll(x, LANES - MLA_ROPE // 4, 1), pltpu.roll(x, MLA_ROPE // 4, 1))
    return x * cos + partner * sin_signed


def _mla_body(*refs, has_q, normalize, use_rope):
    it = iter(refs)
    if normalize:
        p_ref = next(it)
        qnw_ref = next(it)
        kvnw_ref = next(it)
    else:
        ckv_ref = next(it)
        krp_ref = next(it)
    if has_q:
        wqn_ref, wqr_ref, qhn_ref, qhr_ref = next(it), next(it), next(it), next(it)
    wkn_ref, wv_ref, khn_ref, khr_ref = next(it), next(it), next(it), next(it)
    if use_rope:
        cos_ref, sin_ref = next(it), next(it)
    if has_q:
        qf_ref = next(it)
    kf_ref, v_ref = next(it), next(it)
    if normalize:
        ckv_out, krp_out = next(it), next(it)

    def rms(x):
        return x * lax.rsqrt(jnp.mean(x * x, -1, keepdims=True) + EPS)

    if normalize:
        p = p_ref[...]
        cq = rms(p[:, :MLA_QL]) * qnw_ref[...]
        ckv = rms(p[:, MLA_QL:MLA_QL + MLA_KVL]) * kvnw_ref[...]
        krp = p[:, MLA_QL + MLA_KVL:]
        ckv_out[...] = ckv
        krp_out[...] = krp
    else:
        ckv = ckv_ref[...]
        krp = krp_ref[...]
    if use_rope:
        cos = cos_ref[...]
        sin = sin_ref[...]
        lane = lax.broadcasted_iota(i32, cos.shape, 1)
        first_half = (lane % (MLA_ROPE // 2)) < (MLA_ROPE // 4)

    def head_norm(a, b, wn, wr, scale):
        ss = jnp.sum(a * a, -1, keepdims=True) + jnp.sum(b * b, -1, keepdims=True)
        r = lax.rsqrt(ss * (1.0 / MLA_QK) + EPS)
        a = a * r * wn
        b = b * r * wr
        if use_rope:
            b = _rope(b, cos, sin, first_half)
        return (a * scale).astype(bf16), (b * scale).astype(bf16)

    if has_q:
        cqb = cq.astype(bf16)
        qn = _dot(cqb, wqn_ref[...])
        qr = _dot(cqb, wqr_ref[...])
        for h in range(MLA_H):
            a, b = head_norm(qn[:, h * LANES:(h + 1) * LANES], qr[:, h * LANES:(h + 1) * LANES],
                             qhn_ref[...], qhr_ref[...], MLA_QK ** -0.5)
            qf_ref[:, 2 * h * LANES:(2 * h + 1) * LANES] = a
            qf_ref[:, (2 * h + 1) * LANES:(2 * h + 2) * LANES] = b
    ckvb = ckv.astype(bf16)
    kn = _dot(ckvb, wkn_ref[...])
    v_ref[...] = _dot(ckvb, wv_ref[...]).astype(bf16)
    for h in range(MLA_H):
        a, b = head_norm(kn[:, h * LANES:(h + 1) * LANES], krp, khn_ref[...], khr_ref[...], 1.0)
        kf_ref[:, 2 * h * LANES:(2 * h + 1) * LANES] = a
        kf_ref[:, (2 * h + 1) * LANES:(2 * h + 2) * LANES] = b


def mla_prep(srcs, norm_ws, q_ws, kv_ws, rope_tabs, *, row_off, rows, seq, tm=256):
    normalize = norm_ws is not None
    has_q = q_ws is not None
    use_rope = rope_tabs is not None
    ro = row_off // tm
    nt = rows // tm
    hd = MLA_H * 2 * LANES

    def full(a):
        return pl.BlockSpec(a.shape, lambda i: (0,) * a.ndim)

    args, in_specs = [], []
    for s in srcs:
        args.append(s)
        in_specs.append(pl.BlockSpec((tm, s.shape[1]), lambda i: (ro + i, 0)))
    for group in (norm_ws, q_ws, kv_ws):
        if group is not None:
            for a in group:
                args.append(a)
                in_specs.append(full(a))
    if use_rope:
        tps = seq // tm
        for a in rope_tabs:
            args.append(a)
            in_specs.append(pl.BlockSpec((tm, LANES), lambda i: (i % tps, 0)))
    out_shape, out_specs = [], []

    def out(cols, dt):
        out_shape.append(jax.ShapeDtypeStruct((rows, cols), dt))
        out_specs.append(pl.BlockSpec((tm, cols), lambda i: (i, 0)))

    if has_q:
        out(hd, bf16)
    out(hd, bf16)
    out(MLA_H * LANES, bf16)
    if normalize:
        out(MLA_KVL, f32)
        out(LANES, f32)
    return pl.pallas_call(
        functools.partial(_mla_body, has_q=has_q, normalize=normalize, use_rope=use_rope),
        grid=(nt,),
        in_specs=in_specs,
        out_specs=out_specs,
        out_shape=out_shape,
        compiler_params=_cp("arbitrary"),
        name="mla_prep",
    )(*args)


ATTN_KCHUNK = 1024


def _attn_body(*refs, n_kv):
    q_ref = refs[0]
    k_refs = refs[1:1 + n_kv]
    v_refs = refs[1 + n_kv:1 + 2 * n_kv]
    o_ref = refs[1 + 2 * n_kv]
    q = q_ref[...]
    chunks = [(k_ref, v_ref, c0, min(c0 + ATTN_KCHUNK, k_ref.shape[0]))
              for k_ref, v_ref in zip(k_refs, v_refs) for c0 in range(0, k_ref.shape[0], ATTN_KCHUNK)]

    def scores(ch):
        return _dot(q, ch[0][ch[2]:ch[3], :], NT)

    s_next = scores(chunks[0])
    m = l = acc = None
    for c, ch in enumerate(chunks):
        s = s_next
        if c + 1 < len(chunks):
            s_next = scores(chunks[c + 1])
        ms = jnp.max(s, -1, keepdims=True)
        v = ch[1][ch[2]:ch[3], :]
        if m is None:
            m = ms
            e = jnp.exp(s - m)
            l = jnp.sum(e, -1, keepdims=True)
            acc = _dot(e.astype(bf16), v)
        else:
            m_new = jnp.maximum(m, ms)
            scale = jnp.exp(m - m_new)
            e = jnp.exp(s - m_new)
            l = l * scale + jnp.sum(e, -1, keepdims=True)
            acc = acc * scale + _dot(e.astype(bf16), v)
            m = m_new
    o_ref[...] = acc / l


def attend(qf, kfs, vs, *, batch, lq, lks, tq=256):
    n_kv = len(kfs)
    nq = lq // tq
    in_specs = [pl.BlockSpec((tq, 2 * LANES), lambda b, h, i: (b * nq + i, h))]
    in_specs += [pl.BlockSpec((lk, 2 * LANES), lambda b, h, i: (b, h)) for lk in lks]
    in_specs += [pl.BlockSpec((lk, LANES), lambda b, h, i: (b, h)) for lk in lks]
    return pl.pallas_call(
        functools.partial(_attn_body, n_kv=n_kv),
        grid=(batch, MLA_H, nq),
        in_specs=in_specs,
        out_specs=pl.BlockSpec((tq, LANES), lambda b, h, i: (b * nq + i, h)),
        out_shape=jax.ShapeDtypeStruct((batch * lq, MLA_H * LANES), f32),
        compiler_params=_cp("arbitrary", "arbitrary", "arbitrary"),
        name="attend",
    )(qf, *kfs, *vs)


def _topk_rows(s, k, val_scr, idx_scr, payload=None):
    rows = s.shape[0]
    iota = lax.broadcasted_iota(i32, s.shape, 0)
    for r in range(k):
        m = jnp.max(s, axis=0, keepdims=True)
        idx = jnp.min(jnp.where(s == m, iota, rows), axis=0, keepdims=True)
        hit = iota == idx
        val_scr[r:r + 1, :] = m
        if payload is None:
            idx_scr[r:r + 1, :] = idx
        else:
            idx_scr[r:r + 1, :] = jnp.max(jnp.where(hit, payload, -1), axis=0, keepdims=True)
        s = jnp.where(hit, -jnp.inf, s)


def _pair_candidates(sv0, sv1, si0, si1):
    K = PEER_K
    sub = lax.broadcasted_iota(i32, (SUBLANES, sv0.shape[1]), 0)
    vals, idxs = [], []
    for a in range(SUBLANES):
        nb = K // (a + 1)
        for b0 in range(0, nb, SUBLANES):
            v = sv0[a:a + 1, :] + sv1[b0:b0 + SUBLANES, :]
            e = si0[a:a + 1, :] * PEER_NK + si1[b0:b0 + SUBLANES, :]
            if nb - b0 < SUBLANES:
                v = jnp.where(sub < nb - b0, v, -jnp.inf)
            vals.append(v)
            idxs.append(e)
    vals.append(sv0[SUBLANES:K, :] + sv1[0:1, :])
    idxs.append(si0[SUBLANES:K, :] * PEER_NK + si1[0:1, :])
    return jnp.concatenate(vals, axis=0), jnp.concatenate(idxs, axis=0)


def _u_slot_rows(s):
    return SUBLANES * (s // SUBLANES) + HALF * (s % 2) + (s % SUBLANES) // 2


def _peer_topk_body(q_ref, keys_ref, cu_ref, gate_ref, sv, si, cv, ci, e_all, g_all):
    K = PEER_K
    tt = q_ref.shape[0]
    for h in range(PEER_H):
        for p in range(2):
            qhp = q_ref[:, (2 * h + p) * LANES:(2 * h + p + 1) * LANES]
            s = _dot(keys_ref[h, p], qhp, NT, precision=HI)
            _topk_rows(s, K, sv.at[p], si.at[p])
        cand, cidx = _pair_candidates(sv[0], sv[1], si[0], si[1])
        _topk_rows(cand, K, cv, ci, payload=cidx)
        cs = cv[...]
        e = jnp.exp(cs - cs[0:1, :])
        g_all[h * K:(h + 1) * K, :] = e / jnp.sum(e, axis=0, keepdims=True)
        code = (ci[...] + PACK_TE) * HALF
        for b in range(tt // GATHER_TB):
            e_all[b, h * K:(h + 1) * K, :] = code[:, b * GATHER_TB:(b + 1) * GATHER_TB]
    gate_ref[...] = g_all[...].T
    for b in range(tt // GATHER_TB):
        for s in range(U_SLOTS):
            cu_ref[s, b] = e_all[b, pl.ds(_u_slot_rows(s), U_GROUPS, stride=U_SLOTS), :] - HALF * (s % 2)


def peer_topk(q, sub_keys, *, tt=512):
    n = q.shape[0]
    K = PEER_K
    nb = tt // GATHER_TB
    return pl.pallas_call(
        _peer_topk_body,
        grid=(n // tt,),
        in_specs=[pl.BlockSpec((tt, q.shape[1]), lambda i: (i, 0)),
                  pl.BlockSpec(sub_keys.shape, lambda i: (0, 0, 0, 0))],
        out_specs=[pl.BlockSpec((U_SLOTS, nb, U_GROUPS, GATHER_TB), lambda i: (0, i, 0, 0)),
                   pl.BlockSpec((tt, PEER_SEL), lambda i: (i, 0))],
        out_shape=[jax.ShapeDtypeStruct((U_SLOTS, n // GATHER_TB, U_GROUPS, GATHER_TB), i32),
                   jax.ShapeDtypeStruct((n, PEER_SEL), f32)],
        scratch_shapes=[pltpu.VMEM((2, K, tt), f32), pltpu.VMEM((2, K, tt), i32),
                        pltpu.VMEM((K, tt), f32), pltpu.VMEM((K, tt), i32),
                        pltpu.VMEM((nb, PEER_SEL, GATHER_TB), i32), pltpu.VMEM((PEER_SEL, tt), f32)],
        compiler_params=_cp("arbitrary"),
        name="peer_topk",
    )(q, sub_keys)


def _pack_body(x_ref, o_ref):
    i = pl.program_id(0)
    guard = jnp.logical_or(i == 0, i == pl.num_programs(0) - 1)

    @pl.when(guard)
    def _zero():
        o_ref[...] = jnp.zeros_like(o_ref)

    @pl.when(jnp.logical_not(guard))
    def _pack():
        half = x_ref.shape[1] // 2
        lo = lax.bitcast_convert_type(x_ref[:, :half].astype(bf16).astype(f32), i32)
        hi = lax.bitcast_convert_type(x_ref[:, half:].astype(bf16).astype(f32), i32)
        words = (hi & jnp.int32(HI_MASK)) | lax.shift_right_logical(lo, jnp.int32(16))
        for s in range(HALF):
            o_ref[pl.ds(s, PACK_TE, stride=HALF), :] = words[:, s * LANES:(s + 1) * LANES]


def pack_expert_table(tab):
    e, d = tab.shape
    nb = e // PACK_TE
    return pl.pallas_call(
        _pack_body,
        grid=(nb + 2,),
        in_specs=[pl.BlockSpec((PACK_TE, d), lambda i: (jnp.clip(i - 1, 0, nb - 1), 0))],
        out_specs=pl.BlockSpec((PACK_TE * HALF, LANES), lambda i: (i, 0)),
        out_shape=jax.ShapeDtypeStruct(((nb + 2) * PACK_TE * HALF, LANES), i32),
        compiler_params=_cp("arbitrary"),
        name="pack_table",
    )(tab)


def _merged(tab_ref, ca, cb, lo_half):
    a = tab_ref[pl.ds(pl.multiple_of(ca, HALF), SUBLANES), :]
    b = tab_ref[pl.ds(pl.multiple_of(cb, HALF), SUBLANES), :]
    return jnp.where(lo_half, a, b)


def _lo(words):
    return lax.bitcast_convert_type(words << 16, f32)


def _hi(words):
    return lax.bitcast_convert_type(words & jnp.int32(HI_MASK), f32)


def _gelu_tanh(x):
    return 0.5 * x * (1.0 + jnp.tanh(math.sqrt(2.0 / math.pi) * (x + 0.044715 * (x * x * x))))


def _fold4(ms, sub):
    lo2 = (sub % 4) < 2
    b = [jnp.where(lo2, ms[k] + pltpu.roll(ms[k], SUBLANES - 2, 0), ms[k + 2] + pltpu.roll(ms[k + 2], 2, 0))
         for k in range(2)]
    lo1 = (sub % 2) < 1
    return jnp.where(lo1, b[0] + pltpu.roll(b[0], SUBLANES - 1, 0), b[1] + pltpu.roll(b[1], 1, 0))


def _peer_u_body(*refs, tb):
    c_refs = refs[:U_SLOTS]
    gate_ref, h_ref, tab_ref, wt_ref, z_ref, h8_ref, w_ref = refs[U_SLOTS:]
    sub = lax.broadcasted_iota(i32, (SUBLANES, LANES), 0)
    lo_half = sub < HALF
    for s in range(SUBLANES):
        h8_ref[pl.ds(s, tb, stride=SUBLANES), :] = h_ref[:, s * LANES:(s + 1) * LANES]

    def lane_totals(t):
        zt = z_ref[pl.ds(pl.multiple_of(t * PEER_SEL, PEER_SEL), PEER_SEL), :]
        return jnp.sum(zt.T, axis=0, keepdims=True)

    z_ref[0:PEER_SEL, :] = jnp.zeros((PEER_SEL, LANES), f32)

    def token(t, carry):
        prev = jnp.maximum(t - 1, 0)
        w_ref[pl.ds(prev, 1), :] = lane_totals(prev)
        r8 = pl.multiple_of(t * SUBLANES, SUBLANES)
        hrow = h8_ref[pl.ds(r8, SUBLANES), :]
        hswap = pltpu.roll(hrow, HALF, 0)
        hl = jnp.where(lo_half, hrow, hswap)
        hh = jnp.where(lo_half, hswap, hrow)
        for g in range(U_GROUPS):
            for fold in range(2):
                ms = []
                for k in range(4):
                    s0 = fold * SUBLANES + 2 * k
                    words = _merged(tab_ref, c_refs[s0][g * tb + t], c_refs[s0 + 1][g * tb + t], lo_half)
                    ms.append(_lo(words) * hl + _hi(words) * hh)
                row0 = t * PEER_SEL + (2 * g + fold) * SUBLANES
                z_ref[pl.ds(pl.multiple_of(row0, SUBLANES), SUBLANES), :] = _fold4(ms, sub)
        return carry

    lax.fori_loop(0, tb, token, 0)
    w_ref[tb - 1:tb, :] = lane_totals(tb - 1)
    wt_ref[...] = (gate_ref[...] * _gelu_tanh(w_ref[...])).T


def peer_u_gather(codes, gate, h, tab):
    tb = GATHER_TB
    n = gate.shape[0]
    codes = codes.reshape(U_SLOTS, n * U_GROUPS)
    smem = [pl.BlockSpec((tb * U_GROUPS,), lambda i: (i,), memory_space=pltpu.SMEM) for _ in range(U_SLOTS)]
    return pl.pallas_call(
        functools.partial(_peer_u_body, tb=tb),
        grid=(n // tb,),
        in_specs=smem + [pl.BlockSpec((tb, PEER_SEL), lambda i: (i, 0)),
                         pl.BlockSpec((tb, D), lambda i: (i, 0)),
                         pl.BlockSpec(tab.shape, lambda i: (0, 0), pipeline_mode=pl.Buffered(1))],
        out_specs=pl.BlockSpec((None, PEER_SEL, tb), lambda i: (i, 0, 0)),
        out_shape=jax.ShapeDtypeStruct((n // tb, PEER_SEL, tb), f32),
        scratch_shapes=[pltpu.VMEM((tb * PEER_SEL, LANES), f32), pltpu.VMEM((tb * SUBLANES, LANES), f32),
                        pltpu.VMEM((tb, PEER_SEL), f32)],
        compiler_params=_cp("arbitrary"),
        name="peer_u",
    )(*[codes[s] for s in range(U_SLOTS)], gate, h, tab)


def _peer_v_body(*refs, tb):
    c_refs = refs[:U_SLOTS]
    wt_ref, x_ref, m_ref, tab_ref, o_ref, y8_ref, wx_ref = refs[U_SLOTS:]
    sub = lax.broadcasted_iota(i32, (SUBLANES, LANES), 0)
    lo_half = sub < HALF

    def expanded(t):
        return jnp.take_along_axis(wt_ref[...], jnp.full((PEER_SEL, tb), t, i32), axis=1)

    wx_ref[...] = expanded(0)

    def token(t, carry):
        nxt = expanded(jnp.minimum(t + 1, tb - 1))
        accs = [jnp.zeros((SUBLANES, LANES), f32) for _ in range(4)]
        for g in range(U_GROUPS):
            for fold in range(2):
                for k in range(4):
                    s0 = fold * SUBLANES + 2 * k
                    ja = g * U_SLOTS + _u_slot_rows(s0)
                    jb = g * U_SLOTS + _u_slot_rows(s0 + 1)
                    words = _merged(tab_ref, c_refs[s0][g * tb + t], c_refs[s0 + 1][g * tb + t], lo_half)
                    wm = jnp.where(lo_half, jnp.broadcast_to(wx_ref[ja:ja + 1, :], (SUBLANES, LANES)),
                                   jnp.broadcast_to(wx_ref[jb:jb + 1, :], (SUBLANES, LANES)))
                    accs[2 * (k % 2)] = accs[2 * (k % 2)] + wm * _lo(words)
                    accs[2 * (k % 2) + 1] = accs[2 * (k % 2) + 1] + wm * _hi(words)
        lo = accs[0] + accs[2]
        hi = accs[1] + accs[3]
        lo = lo + pltpu.roll(lo, HALF, 0)
        hi = hi + pltpu.roll(hi, HALF, 0)
        y8_ref[pl.ds(pl.multiple_of(t * SUBLANES, SUBLANES), SUBLANES), :] = jnp.where(lo_half, lo, hi)
        wx_ref[...] = nxt
        return carry

    lax.fori_loop(0, tb, token, 0)
    for s in range(SUBLANES):
        cols = slice(s * LANES, (s + 1) * LANES)
        o_ref[:, cols] = x_ref[:, cols] + m_ref[5:6, cols] * y8_ref[pl.ds(s, tb, stride=SUBLANES), :]


def peer_v_gather(codes, wt, x, mods_l, tab, *, prompt_rows, seq_len):
    tb = GATHER_TB
    n = x.shape[0]
    pt, tps = prompt_rows // tb, seq_len // tb
    codes = codes.reshape(U_SLOTS, n * U_GROUPS)
    smem = [pl.BlockSpec((tb * U_GROUPS,), lambda i: (i,), memory_space=pltpu.SMEM) for _ in range(U_SLOTS)]
    return pl.pallas_call(
        functools.partial(_peer_v_body, tb=tb),
        grid=(n // tb,),
        in_specs=smem + [pl.BlockSpec((None, PEER_SEL, tb), lambda i: (i, 0, 0)),
                         pl.BlockSpec((tb, D), lambda i: (i, 0)),
                         pl.BlockSpec((None, 6, D), lambda i: (_mod_row(i, pt, tps), 0, 0)),
                         pl.BlockSpec(tab.shape, lambda i: (0, 0), pipeline_mode=pl.Buffered(1))],
        out_specs=pl.BlockSpec((tb, D), lambda i: (i, 0)),
        out_shape=jax.ShapeDtypeStruct((n, D), f32),
        scratch_shapes=[pltpu.VMEM((tb * SUBLANES, LANES), f32), pltpu.VMEM((PEER_SEL, LANES), f32)],
        compiler_params=_cp("arbitrary"),
        name="peer_v",
    )(*[codes[s] for s in range(U_SLOTS)], wt, x, mods_l, tab)


def _rope_tables(seq):
    rows = seq // GRID_W
    row = jnp.repeat(jnp.arange(rows, dtype=f32), GRID_W)
    col = jnp.tile(jnp.arange(GRID_W, dtype=f32), rows)
    nfreq = MLA_ROPE // 4
    inv = jnp.power(ROPE_BASE, -jnp.arange(nfreq, dtype=f32) / nfreq)
    ar = row[:, None] * inv
    ac = col[:, None] * inv
    pad1 = jnp.ones((seq, LANES - MLA_ROPE), f32)
    pad0 = jnp.zeros((seq, LANES - MLA_ROPE), f32)
    cos = jnp.concatenate([jnp.cos(ar), jnp.cos(ar), jnp.cos(ac), jnp.cos(ac), pad1], -1)
    sin = jnp.concatenate([-jnp.sin(ar), jnp.sin(ar), -jnp.sin(ac), jnp.sin(ac), pad0], -1)
    return cos, sin


def _pad_heads(w, head_w, lo, hi):
    k = w.shape[0]
    w = w.reshape(k, -1, head_w)[:, :, lo:hi]
    return jnp.pad(w, ((0, 0), (0, 0), (0, LANES - (hi - lo)))).reshape(k, -1)


def kernel(x_prompt, x_sample, state_gdn, cache_mla_ckv, cache_mla_krope, c, c_ctx, w_mod, b_mod, norm_mix, norm_ffn, even_w_in, even_conv_w, gdn_a_log, gdn_dt_bias, gdn_o_norm, even_w_out, odd_w_in, mla_q_norm, mla_kv_norm, mla_w_uq, mla_w_ukv, mla_q_headnorm, mla_k_headnorm, odd_w_out, peer_w_q, peer_sub_keys, peer_u, peer_v):
    B, L, _ = x_prompt.shape
    BS, LS, _ = x_sample.shape
    depth = w_mod.shape[0]
    NP, NS = B * L, BS * LS
    past = cache_mla_ckv.shape[2]
    geo = dict(prompt_rows=NP, seq_len=LS)

    x = jnp.concatenate([x_prompt.reshape(NP, D), x_sample.reshape(NS, D)], 0)
    nrow = 1 + BS
    rpad = -nrow % SUBLANES
    cv = jnp.concatenate([c_ctx[None, :], c, jnp.zeros((rpad, D), f32)], 0)
    mods = ada_mods_all(cv, w_mod, b_mod).reshape(depth, nrow + rpad, 6, D)

    cd_c, cd_s = _dft_mats(LANES)
    cd = jnp.concatenate([cd_c, cd_s], 1).astype(bf16)
    fmats = {}
    for n in (L, LS):
        cl, sl = _dft_mats(n)
        fmats[n] = jnp.concatenate([cl, -sl], 1).astype(bf16)
    cos_t, sin_t = _rope_tables(LS)

    new_gdn, new_ckv, new_kr = [], [], []
    for l in range(depth):
        j = l // 2
        m_l = mods[l]
        if l % 2 == 0:
            wi = even_w_in[j]
            o0 = 3 * A_QK + A_QK
            ba = wi[:, o0:o0 + 4 * GDN_H].reshape(D, 4, GDN_H).transpose(0, 2, 1)
            ba = jnp.pad(ba, ((0, 0), (0, 0), (0, LANES - 4))).reshape(D, GDN_H * LANES)
            w_in = jnp.concatenate([wi[:, :o0], wi[:, o0 + 4 * GDN_H:], ba], 1).astype(bf16)
            p = normmm(x, m_l, 0, norm_mix[l], w_in, **geo)
            nega = -jnp.exp(gdn_a_log[j])
            kw = dict(ba_col0=(o0 + FN_W) // LANES)
            mix_p, st = gdn(p, nega, gdn_dt_bias[j], even_conv_w[j], gdn_o_norm[j], None,
                            row_off=0, batch=B, seq=L, **kw)
            mix_s, _ = gdn(p, nega, gdn_dt_bias[j], even_conv_w[j], gdn_o_norm[j], state_gdn[:, j],
                           row_off=NP, batch=BS, seq=LS, **kw)
            new_gdn.append(st)
            fb_p = fnet(p, cd, fmats[L], row_off=0, batch=B, seq=L, col_block=o0 // FN_W, tr=L)
            fb_s = fnet(p, cd, fmats[LS], row_off=NP, batch=BS, seq=LS, col_block=o0 // FN_W, tr=256)
            wo = even_w_out[j].astype(bf16)
            x = outproj(x, m_l, 2, [(mix_p, mix_s), (fb_p, fb_s)],
                        [wo[:A_QK], wo[A_QK:]], **geo)
        else:
            wi = odd_w_in[j]
            w_in = jnp.pad(wi, ((0, 0), (0, LANES - MLA_ROPE))).astype(bf16)
            p = normmm(x, m_l, 0, norm_mix[l], w_in, **geo)
            norm_ws = (mla_q_norm[j].reshape(1, -1), mla_kv_norm[j].reshape(1, -1))
            qh, kh = mla_q_headnorm[j], mla_k_headnorm[j]

            def split_hw(hw):
                return (hw[:MLA_NOPE].reshape(1, LANES),
                        jnp.pad(hw[MLA_NOPE:], (0, LANES - MLA_ROPE)).reshape(1, LANES))

            q_ws = (_pad_heads(mla_w_uq[j], MLA_QK, 0, MLA_NOPE).astype(bf16),
                    _pad_heads(mla_w_uq[j], MLA_QK, MLA_NOPE, MLA_QK).astype(bf16)) + split_hw(qh)
            kv_ws = (_pad_heads(mla_w_ukv[j], 2 * LANES, 0, LANES).astype(bf16),
                     _pad_heads(mla_w_ukv[j], 2 * LANES, LANES, 2 * LANES).astype(bf16)) + split_hw(kh)
            qf_p, kf_p, v_p, ckv_p, krp_p = mla_prep((p,), norm_ws, q_ws, kv_ws, None, row_off=0, rows=NP, seq=L)
            qf_s, kf_s, v_s, _, _ = mla_prep((p,), norm_ws, q_ws, kv_ws, (cos_t, sin_t), row_off=NP, rows=NS, seq=LS)
            ckv_c = cache_mla_ckv[:, j].reshape(BS * past, MLA_KVL)
            krp_c = jnp.pad(cache_mla_krope[:, j].reshape(BS * past, MLA_ROPE), ((0, 0), (0, LANES - MLA_ROPE)))
            kf_c, v_c = mla_prep((ckv_c, krp_c), None, None, kv_ws, None, row_off=0, rows=BS * past, seq=past)
            o_p = attend(qf_p, [kf_p], [v_p], batch=B, lq=L, lks=[L])
            o_s = attend(qf_s, [kf_s, kf_c], [v_s, v_c], batch=BS, lq=LS, lks=[LS, past])
            new_ckv.append(ckv_p.reshape(B, L, MLA_KVL))
            new_kr.append(krp_p[:, :MLA_ROPE].reshape(B, L, MLA_ROPE))
            x = outproj(x, m_l, 2, [(o_p, o_s)], [odd_w_out[j].astype(bf16)], **geo)
        q, h = normmm(x, m_l, 3, norm_ffn[l], peer_w_q[l], with_h=True, **geo)
        codes, gate = peer_topk(q, peer_sub_keys[l])
        wt = peer_u_gather(codes, gate, h, pack_expert_table(peer_u[l]))
        x = peer_v_gather(codes, wt, x, m_l, pack_expert_table(peer_v[l]), **geo)

    y_prompt = x[:NP].reshape(B, L, D)
    y_sample = x[NP:].reshape(BS, LS, D)
    return (y_prompt, y_sample, jnp.stack(new_gdn, 1), jnp.stack(new_ckv, 1), jnp.stack(new_kr, 1))
```
